```python
import math
import jax
import jax.numpy as jnp
from jax import lax
import numpy as np

D_MODEL = 2048
BATCH = 4
SEQ = 2048
DEPTH = 2

N_MIXERS = 2
N_HEADS = 8
HEAD_DIM = 128
V_DIM = 2 * HEAD_DIM
ATTN_WIDTH = N_HEADS * V_DIM
DIFF_QKV_COLS = 2 * N_HEADS * HEAD_DIM * 2 + N_HEADS * V_DIM
SW_GROUPS = ((128, 1), (512, 4), (2048, 16))
N_SW_GROUPS = len(SW_GROUPS)
SW_STEPS = 128
SW_GROUP_COLS = 2 * N_HEADS * HEAD_DIM + N_HEADS * V_DIM
BLOCK = 128
NUM_BUCKETS = 32
MAX_DISTANCE = 2048
D_FF = 5632
CONV_WIDTH = 3
EPS = 1e-6
NEG = -1e30
N_DIFF_LAYERS = (DEPTH + 1) // 2
N_DIL_LAYERS = DEPTH // 2

kernel_name = "hybrid_diffattn_dilated_convffn"


def rmsnorm(x, g):
    xf = x.astype(jnp.float32)
    y = xf * lax.rsqrt(jnp.mean(xf * xf, axis=-1, keepdims=True) + EPS)
    return (y * g.astype(jnp.float32)).astype(x.dtype)


def t5_bucket(dist):
    max_exact = NUM_BUCKETS // 2
    n = jnp.maximum(dist, 0)
    nf = jnp.maximum(n, 1).astype(jnp.float32)
    large = max_exact + (jnp.log(nf / max_exact) / math.log(MAX_DISTANCE / max_exact)
                         * (NUM_BUCKETS - max_exact)).astype(jnp.int32)
    large = jnp.minimum(large, NUM_BUCKETS - 1)
    return jnp.where(n < max_exact, n, large)


def diff_attention(h, w_qkv, q_gain, k_gain, lam, subln_g, w_o, rel_bias, layer_idx):
    B, S, _ = h.shape
    qkv = h @ w_qkv
    nqk = 2 * N_HEADS * HEAD_DIM
    q, k, v = jnp.split(qkv, [nqk, 2 * nqk], axis=-1)
    q = rmsnorm(q.reshape(B, S, N_HEADS, 2, HEAD_DIM), q_gain)
    k = rmsnorm(k.reshape(B, S, N_HEADS, 2, HEAD_DIM), k_gain)
    v = v.reshape(B, S, N_HEADS, V_DIM)
    lam_init = 0.8 - 0.6 * math.exp(-0.3 * layer_idx)
    lf = lam.astype(jnp.float32)
    lam_full = jnp.exp(jnp.sum(lf[0] * lf[1])) - jnp.exp(jnp.sum(lf[2] * lf[3])) + lam_init
    scale = HEAD_DIM ** -0.5
    nq = S // BLOCK
    q_blocks = q.reshape(B, nq, BLOCK, N_HEADS, 2, HEAD_DIM).transpose(1, 0, 2, 3, 4, 5)
    k_pos = jnp.arange(S)

    def block_fn(args):
        qb, start = args
        q_pos = start + jnp.arange(BLOCK)
        dist = q_pos[:, None] - k_pos[None, :]
        bias = jnp.moveaxis(rel_bias[t5_bucket(dist)], -1, 0).astype(jnp.float32)
        logits = jnp.einsum('bqhme,bkhme->bhmqk', qb, k).astype(jnp.float32) * scale + bias[:, None]
        logits = jnp.where(dist >= 0, logits, NEG)
        p = jax.nn.softmax(logits, axis=-1)
        a = p[:, :, 0] - lam_full * p[:, :, 1]
        return jnp.einsum('bhqk,bkhe->bqhe', a.astype(v.dtype), v)

    o = lax.map(block_fn, (q_blocks, jnp.arange(nq, dtype=jnp.int32) * BLOCK))
    o = o.transpose(1, 0, 2, 3, 4).reshape(B, S, N_HEADS, V_DIM)
    o = rmsnorm(o, subln_g) * (1.0 - lam_init)
    return o.reshape(B, S, ATTN_WIDTH) @ w_o


def dilated_attention(h, w_qkv, q_gain, k_gain, w_o, rel_bias):
    B, S, _ = h.shape
    qkv = (h @ w_qkv).reshape(B, S, N_SW_GROUPS, SW_GROUP_COLS)
    nqk = N_HEADS * HEAD_DIM
    scale = HEAD_DIM ** -0.5
    a_idx = jnp.arange(BLOCK)
    b_idx = jnp.arange(2 * BLOCK)
    dist_sub = a_idx[:, None] + BLOCK - b_idx[None, :]
    in_window = (dist_sub >= 0) & (dist_sub <= SW_STEPS)
    outs, lses = [], []
    for g, (window, dil) in enumerate(SW_GROUPS):
        q, k, v = jnp.split(qkv[:, :, g], [nqk, 2 * nqk], axis=-1)
        q = rmsnorm(q.reshape(B, S, N_HEADS, HEAD_DIM), q_gain[g])
        k = rmsnorm(k.reshape(B, S, N_HEADS, HEAD_DIM), k_gain[g])
        v = v.reshape(B, S, N_HEADS, V_DIM)
        L = S // dil
        nb = -(-L // BLOCK)
        Lp = nb * BLOCK

        def to_blocks(t):
            e = t.shape[-1]
            t = t.reshape(B, L, dil, N_HEADS, e)
            t = jnp.pad(t, ((0, 0), (0, Lp - L), (0, 0), (0, 0), (0, 0)))
            return t.reshape(B, nb, BLOCK, dil, N_HEADS, e)

        def with_prev(t):
            prev = jnp.pad(t[:, :-1], ((0, 0), (1, 0), (0, 0), (0, 0), (0, 0), (0, 0)))
            return jnp.concatenate([prev, t], axis=2)

        qb = to_blocks(q)
        kk = with_prev(to_blocks(k))
        vv = with_prev(to_blocks(v))
        bias = jnp.moveaxis(rel_bias[t5_bucket(dist_sub * dil)], -1, 0).astype(jnp.float32)
        key_valid = (jnp.arange(nb)[:, None] * BLOCK - BLOCK + b_idx[None, :]) >= 0
        mask = in_window[None] & key_valid[:, None, :]
        logits = jnp.einsum('bnqrhe,bnkrhe->bnrhqk', qb, kk).astype(jnp.float32) * scale + bias
        logits = jnp.where(mask[None, :, None, None], logits, NEG)
        m = jnp.max(logits, axis=-1, keepdims=True)
        p = jnp.exp(logits - m)
        l = jnp.sum(p, axis=-1, keepdims=True)
        o = jnp.einsum('bnrhqk,bnkrhe->bnqrhe', (p / l).astype(v.dtype), vv)
        lse = (m + jnp.log(l))[..., 0]
        o = o.reshape(B, Lp, dil, N_HEADS, V_DIM)[:, :L].reshape(B, S, N_HEADS, V_DIM)
        lse = lse.transpose(0, 1, 4, 2, 3).reshape(B, Lp, dil, N_HEADS)[:, :L].reshape(B, S, N_HEADS)
        outs.append(o)
        lses.append(lse)
    alpha = jax.nn.softmax(jnp.stack(lses, axis=0), axis=0)
    o = alpha[0][..., None] * outs[0] + alpha[1][..., None] * outs[1] + alpha[2][..., None] * outs[2]
    return o.astype(h.dtype).reshape(B, S, ATTN_WIDTH) @ w_o


def conv_ffn(h, w_up, conv_w, conv_b, w_down):
    u = h @ w_up
    c = u.shape[-1]
    u = lax.conv_general_dilated(u, conv_w[:, None, :].astype(u.dtype), window_strides=(1,),
                                 padding=[(CONV_WIDTH - 1, 0)],
                                 dimension_numbers=('NWC', 'WIO', 'NWC'),
                                 feature_group_count=c) + conv_b
    gate, val = jnp.split(u, 2, axis=-1)
    return (jax.nn.silu(gate) * val) @ w_down


def setup_inputs(seed: int = 0) -> dict:
    key = jax.random.key(seed)
    ks = jax.random.split(key, 20)
    f32 = jnp.float32
    nrm = lambda k, shape, s: jax.random.normal(k, shape, f32) * s
    gain = lambda k, shape: 1.0 + 0.02 * jax.random.normal(k, shape, f32)
    return {
        "x": jax.random.normal(ks[0], (BATCH, SEQ, D_MODEL), f32),
        "rel_bias": nrm(ks[1], (NUM_BUCKETS, N_HEADS), 0.2),
        "ln_mix": gain(ks[2], (DEPTH, D_MODEL)),
        "ln_ffn": gain(ks[3], (DEPTH, D_MODEL)),
        "a_w_qkv": nrm(ks[4], (N_DIFF_LAYERS, D_MODEL, DIFF_QKV_COLS), D_MODEL ** -0.5),
        "a_q_norm": gain(ks[5], (N_DIFF_LAYERS, HEAD_DIM)),
        "a_k_norm": gain(ks[6], (N_DIFF_LAYERS, HEAD_DIM)),
        "a_lambda": nrm(ks[7], (N_DIFF_LAYERS, 4, HEAD_DIM), 0.1),
        "a_subln": gain(ks[8], (N_DIFF_LAYERS, V_DIM)),
        "a_w_o": nrm(ks[9], (N_DIFF_LAYERS, ATTN_WIDTH, D_MODEL), ATTN_WIDTH ** -0.5),
        "b_w_qkv": nrm(ks[10], (N_DIL_LAYERS, D_MODEL, N_SW_GROUPS * SW_GROUP_COLS), D_MODEL ** -0.5),
        "b_q_norm": gain(ks[11], (N_DIL_LAYERS, N_SW_GROUPS, HEAD_DIM)),
        "b_k_norm": gain(ks[12], (N_DIL_LAYERS, N_SW_GROUPS, HEAD_DIM)),
        "b_w_o": nrm(ks[13], (N_DIL_LAYERS, ATTN_WIDTH, D_MODEL), ATTN_WIDTH ** -0.5),
        "f_w_up": nrm(ks[14], (DEPTH, D_MODEL, 2 * D_FF), D_MODEL ** -0.5),
        "f_conv_w": nrm(ks[15], (DEPTH, CONV_WIDTH, 2 * D_FF), CONV_WIDTH ** -0.5),
        "f_conv_b": nrm(ks[16], (DEPTH, 2 * D_FF), 0.02),
        "f_w_down": nrm(ks[17], (DEPTH, D_FF, D_MODEL), D_FF ** -0.5),
    }


def reference(x, rel_bias, ln_mix, ln_ffn, a_w_qkv, a_q_norm, a_k_norm, a_lambda, a_subln, a_w_o,
              b_w_qkv, b_q_norm, b_k_norm, b_w_o, f_w_up, f_conv_w, f_conv_b, f_w_down):
    h = x
    for i in range(DEPTH):
        y = rmsnorm(h, ln_mix[i])
        j = i // N_MIXERS
        if i % N_MIXERS == 0:
            h = h + diff_attention(y, a_w_qkv[j], a_q_norm[j], a_k_norm[j], a_lambda[j],
                                   a_subln[j], a_w_o[j], rel_bias, i)
        else:
            h = h + dilated_attention(y, b_w_qkv[j], b_q_norm[j], b_k_norm[j], b_w_o[j], rel_bias)
        h = h + conv_ffn(rmsnorm(h, ln_ffn[i]), f_w_up[i], f_conv_w[i], f_conv_b[i], f_w_down[i])
    return h
```

```python
import functools
import math

import jax
import jax.numpy as jnp
from jax import lax
from jax.experimental import pallas as pl
from jax.experimental.pallas import tpu as pltpu

F32 = jnp.float32
BF16 = jnp.bfloat16

N_HEADS = 8
HEAD_DIM = 128
V_DIM = 2 * HEAD_DIM
SW_DILATIONS = (1, 4, 16)
SW_STEPS = 128
NUM_BUCKETS = 32
MAX_DISTANCE = 2048
EPS = 1e-6
NEG = -1e30

LANES = 128
COL_TILE = 256
ATTN_TILE = 256
CONV_HALO = 16
VMEM_LIMIT = 60 * 1024 * 1024


def _tiles():
    return dict(tm=1024, tn_qkv=512, tn_proj=512, tn_up=512, tn_down=256, norm_chunk=256)


def _params(*sem):
    return pltpu.CompilerParams(dimension_semantics=sem, vmem_limit_bytes=VMEM_LIMIT)


def _rmsnorm_rows(x_ref, g_ref, out_ref, row0, rows, chunk):
    g = g_ref[...]
    for r in range(0, rows, chunk):
        x = x_ref[r:r + chunk, :]
        rs = lax.rsqrt(jnp.mean(x * x, axis=-1, keepdims=True) + EPS)
        out_ref[row0 + r:row0 + r + chunk, :] = (x * rs * g).astype(BF16)


def _t5_bucket(dist):
    max_exact = NUM_BUCKETS // 2
    n = jnp.maximum(dist, 0)
    nf = jnp.maximum(n, 1).astype(F32)
    large = max_exact + (jnp.log(nf / max_exact) / math.log(MAX_DISTANCE / max_exact)
                         * (NUM_BUCKETS - max_exact)).astype(jnp.int32)
    large = jnp.minimum(large, NUM_BUCKETS - 1)
    return jnp.where(n < max_exact, n, large)


def _bias_lookup(rb_ref, bucket, h):
    val = jnp.zeros(bucket.shape, F32)
    for b in range(NUM_BUCKETS):
        val = jnp.where(bucket == b, rb_ref[b, h], val)
    return val


def _diff_bias_kernel(rb_ref, o_ref):
    h = pl.program_id(0)
    d = pl.program_id(1)
    t = ATTN_TILE
    dist = (d * t + lax.broadcasted_iota(jnp.int32, (t, t), 0)
            - lax.broadcasted_iota(jnp.int32, (t, t), 1))
    val = _bias_lookup(rb_ref, _t5_bucket(dist), h)
    o_ref[0, 0] = jnp.where(dist >= 0, val, NEG)


def _diff_bias_table(rel_bias, seq):
    nd = seq // ATTN_TILE
    return pl.pallas_call(
        _diff_bias_kernel,
        out_shape=jax.ShapeDtypeStruct((N_HEADS, nd, ATTN_TILE, ATTN_TILE), F32),
        grid=(N_HEADS, nd),
        in_specs=[pl.BlockSpec(memory_space=pltpu.SMEM)],
        out_specs=pl.BlockSpec((1, 1, ATTN_TILE, ATTN_TILE), lambda h, d: (h, d, 0, 0)),
        compiler_params=_params("parallel", "parallel"),
        name="diff_bias_table",
    )(rel_bias)


def _dil_bias_kernel(rb_ref, o_ref):
    g = pl.program_id(0)
    h = pl.program_id(1)
    v = pl.program_id(2)
    dil = jnp.left_shift(1, 2 * g)
    a = lax.broadcasted_iota(jnp.int32, (SW_STEPS, 2 * SW_STEPS), 0)
    b = lax.broadcasted_iota(jnp.int32, (SW_STEPS, 2 * SW_STEPS), 1)
    dist_sub = a + SW_STEPS * (1 - v) - b
    val = _bias_lookup(rb_ref, _t5_bucket(dist_sub * dil), h)
    o_ref[0, 0, 0] = jnp.where((dist_sub >= 0) & (dist_sub <= SW_STEPS), val, NEG)


def _dil_bias_table(rel_bias):
    ng = len(SW_DILATIONS)
    return pl.pallas_call(
        _dil_bias_kernel,
        out_shape=jax.ShapeDtypeStruct((ng, N_HEADS, 2, SW_STEPS, 2 * SW_STEPS), F32),
        grid=(ng, N_HEADS, 2),
        in_specs=[pl.BlockSpec(memory_space=pltpu.SMEM)],
        out_specs=pl.BlockSpec((1, 1, 1, SW_STEPS, 2 * SW_STEPS), lambda g, h, v: (g, h, v, 0, 0)),
        compiler_params=_params("parallel", "parallel", "parallel"),
        name="dil_bias_table",
    )(rel_bias)


def _qkv_kernel(flag_ref, x_ref, g_ref, w_ref, cg_ref, o_ref, xn_ref, *, tm, tn, chunk):
    j = pl.program_id(1)

    @pl.when(j == 0)
    def _():
        _rmsnorm_rows(x_ref, g_ref, xn_ref, 0, tm, chunk)

    acc = jnp.dot(xn_ref[...], w_ref[...].astype(BF16), preferred_element_type=F32)
    normed = flag_ref[j] != 0
    per_tile = COL_TILE // LANES
    for c in range(tn // LANES):
        y = acc[:, c * LANES:(c + 1) * LANES]
        rs = lax.rsqrt(jnp.mean(y * y, axis=-1, keepdims=True) + EPS)
        scale = jnp.where(normed, rs, 1.0)
        lo = (c % per_tile) * LANES
        o_ref[c // per_tile, :, lo:lo + LANES] = (y * scale * cg_ref[:, c * LANES:(c + 1) * LANES]).astype(BF16)


def _qkv_proj(x2d, ln_g, w, col_gain, tile_flags):
    t = _tiles()
    tm, tn = t["tm"], t["tn_qkv"]
    m, d = x2d.shape
    n = w.shape[1]
    kern = functools.partial(_qkv_kernel, tm=tm, tn=tn, chunk=t["norm_chunk"])
    return pl.pallas_call(
        kern,
        out_shape=jax.ShapeDtypeStruct((n // COL_TILE, m, COL_TILE), BF16),
        grid_spec=pltpu.PrefetchScalarGridSpec(
            num_scalar_prefetch=1,
            grid=(m // tm, n // tn),
            in_specs=[
                pl.BlockSpec((tm, d), lambda i, j, f: (i, 0)),
                pl.BlockSpec((1, d), lambda i, j, f: (0, 0)),
                pl.BlockSpec((d, tn), lambda i, j, f: (0, j)),
                pl.BlockSpec((1, tn), lambda i, j, f: (0, j)),
            ],
            out_specs=pl.BlockSpec((tn // COL_TILE, tm, COL_TILE), lambda i, j, f: (j, i, 0)),
            scratch_shapes=[pltpu.VMEM((tm, d), BF16)],
        ),
        compiler_params=_params("parallel", "arbitrary"),
        name="qkv_proj",
    )(tile_flags, x2d, ln_g.reshape(1, d), w, col_gain.reshape(1, n))


def _diff_attn_kernel(q_ref, k_ref, v_ref, bias_ref, lam_ref, sg_ref, o_ref,
                      acc_ref, m_ref, l_ref, *, lam_init):
    t = ATTN_TILE
    qi = pl.program_id(2)
    m_ref[...] = jnp.full(m_ref.shape, -jnp.inf, F32)
    l_ref[...] = jnp.zeros(l_ref.shape, F32)
    acc_ref[...] = jnp.zeros(acc_ref.shape, F32)
    q = q_ref[0]

    def body(ki, carry):
        row = pl.multiple_of(ki * t, t)
        kb = k_ref[0, pl.ds(row, t), :]
        vb = v_ref[0, pl.ds(row, t), :]
        bias = bias_ref[0, qi - ki]
        for mp in range(2):
            lo = mp * HEAD_DIM
            s = lax.dot_general(q[:, lo:lo + HEAD_DIM], kb[:, lo:lo + HEAD_DIM],
                                (((1,), (1,)), ((), ())), preferred_element_type=F32) + bias
            m_old = m_ref[mp]
            m_new = jnp.maximum(m_old, jnp.max(s, axis=-1, keepdims=True))
            alpha = jnp.exp(m_old - m_new)
            p = jnp.exp(s - m_new)
            l_ref[mp] = alpha * l_ref[mp] + jnp.sum(p, axis=-1, keepdims=True)
            acc_ref[mp] = alpha * acc_ref[mp] + jnp.dot(p.astype(BF16), vb, preferred_element_type=F32)
            m_ref[mp] = m_new
        return carry

    lax.fori_loop(0, qi + 1, body, 0)

    lam = lam_ref[...]
    lam_full = (jnp.exp(jnp.sum(lam[0:1] * lam[1:2], axis=-1, keepdims=True))
                - jnp.exp(jnp.sum(lam[2:3] * lam[3:4], axis=-1, keepdims=True)) + lam_init)
    o = acc_ref[0] / l_ref[0] - lam_full * (acc_ref[1] / l_ref[1])
    rs = lax.rsqrt(jnp.mean(o * o, axis=-1, keepdims=True) + EPS)
    o_ref[...] = ((o * rs * sg_ref[...]) * (1.0 - lam_init)).astype(BF16)


def _diff_attention(qkv, bias_tab, lam, subln_g, batch, seq, lam_init):
    t = ATTN_TILE
    nq = seq // t
    m = batch * seq
    kern = functools.partial(_diff_attn_kernel, lam_init=lam_init)
    return pl.pallas_call(
        kern,
        out_shape=jax.ShapeDtypeStruct((m, N_HEADS * V_DIM), BF16),
        grid=(N_HEADS, batch, nq),
        in_specs=[
            pl.BlockSpec((1, t, COL_TILE), lambda h, b, qi: (h, b * nq + qi, 0)),
            pl.BlockSpec((1, seq, COL_TILE), lambda h, b, qi: (N_HEADS + h, b, 0)),
            pl.BlockSpec((1, seq, COL_TILE), lambda h, b, qi: (2 * N_HEADS + h, b, 0)),
            pl.BlockSpec((1, nq, t, t), lambda h, b, qi: (h, 0, 0, 0)),
            pl.BlockSpec((4, HEAD_DIM), lambda h, b, qi: (0, 0)),
            pl.BlockSpec((1, V_DIM), lambda h, b, qi: (0, 0)),
        ],
        out_specs=pl.BlockSpec((t, V_DIM), lambda h, b, qi: (b * nq + qi, h)),
        scratch_shapes=[pltpu.VMEM((2, t, V_DIM), F32), pltpu.VMEM((2, t, 1), F32), pltpu.VMEM((2, t, 1), F32)],
        compiler_params=_params("parallel", "parallel", "parallel"),
        name="diff_attention",
    )(qkv, qkv, qkv, bias_tab, lam, subln_g.reshape(1, V_DIM))


def _dil_attn_kernel(q_ref, k_ref, v_ref, bias_ref, o_ref, lse_ref, *, sub_len, rows):
    blk = SW_STEPS
    c = pl.program_id(2)
    nblk = rows // blk
    single = sub_len == blk
    lane = lax.broadcasted_iota(jnp.int32, (blk, LANES), 1)
    for n in range(nblk):
        ng = c * nblk + n
        if single:
            start, variant, width = 0, 1, blk
        else:
            start = pl.multiple_of(jnp.maximum(ng - 1, 0) * blk, blk)
            variant = (ng == 0).astype(jnp.int32)
            width = 2 * blk
        lse_tile = jnp.zeros((blk, LANES), F32)
        for h in range(N_HEADS):
            tile, lo = h // 2, (h % 2) * HEAD_DIM
            qt = q_ref[tile, n * blk:(n + 1) * blk, lo:lo + HEAD_DIM]
            kw = k_ref[tile, pl.ds(start, width), lo:lo + HEAD_DIM]
            vw = v_ref[h, pl.ds(start, width), :]
            bias = bias_ref[0, h, variant][:, :width]
            s = lax.dot_general(qt, kw, (((1,), (1,)), ((), ())), preferred_element_type=F32) + bias
            mx = jnp.max(s, axis=-1, keepdims=True)
            p = jnp.exp(s - mx)
            den = jnp.sum(p, axis=-1, keepdims=True)
            o = jnp.dot(p.astype(BF16), vw, preferred_element_type=F32) / den
            o_ref[n * blk:(n + 1) * blk, h * V_DIM:(h + 1) * V_DIM] = o.astype(BF16)
            lse_tile = jnp.where(lane == h, mx + jnp.log(den), lse_tile)
        lse_ref[n * blk:(n + 1) * blk, :] = lse_tile


def _dilated_group_attention(qkv, bias_tab, g, batch, seq):
    dil = SW_DILATIONS[g]
    sub_len = seq // dil
    rows = min(sub_len, 512)
    nc = sub_len // rows
    m = batch * seq
    ntiles = qkv.shape[0]
    qkv_v = qkv.reshape(ntiles, batch * sub_len, dil * COL_TILE)
    kern = functools.partial(_dil_attn_kernel, sub_len=sub_len, rows=rows)
    o, lse = pl.pallas_call(
        kern,
        out_shape=(jax.ShapeDtypeStruct((batch * sub_len, dil * N_HEADS * V_DIM), BF16),
                   jax.ShapeDtypeStruct((batch * sub_len, dil * LANES), F32)),
        grid=(batch, dil, nc),
        in_specs=[
            pl.BlockSpec((4, rows, COL_TILE), lambda b, r, c: (4 * g, b * nc + c, r)),
            pl.BlockSpec((4, sub_len, COL_TILE), lambda b, r, c: (4 * g + 1, b, r)),
            pl.BlockSpec((8, sub_len, COL_TILE), lambda b, r, c: (2 * g + 1, b, r)),
            pl.BlockSpec((1, N_HEADS, 2, SW_STEPS, 2 * SW_STEPS), lambda b, r, c: (g, 0, 0, 0, 0)),
        ],
        out_specs=(pl.BlockSpec((rows, N_HEADS * V_DIM), lambda b, r, c: (b * nc + c, r)),
                   pl.BlockSpec((rows, LANES), lambda b, r, c: (b * nc + c, r))),
        compiler_params=_params("parallel", "parallel", "parallel"),
        name=f"dilated_attention_g{g}",
    )(qkv_v, qkv_v, qkv_v, bias_tab)
    return o.reshape(m, N_HEADS * V_DIM), lse.reshape(m, LANES)


def _proj_res_kernel(a_ref, w_ref, r_ref, o_ref):
    o_ref[...] = r_ref[...] + jnp.dot(a_ref[...], w_ref[...].astype(BF16), preferred_element_type=F32)


def _proj_residual(a, w, res, tn):
    tm = _tiles()["tm"]
    m, k = a.shape
    n = w.shape[1]
    return pl.pallas_call(
        _proj_res_kernel,
        out_shape=jax.ShapeDtypeStruct((m, n), F32),
        grid=(m // tm, n // tn),
        in_specs=[
            pl.BlockSpec((tm, k), lambda i, j: (i, 0)),
            pl.BlockSpec((k, tn), lambda i, j: (0, j)),
            pl.BlockSpec((tm, tn), lambda i, j: (i, j)),
        ],
        out_specs=pl.BlockSpec((tm, tn), lambda i, j: (i, j)),
        compiler_params=_params("parallel", "arbitrary"),
        name="proj_residual",
    )(a, w, res)


def _combine_proj_kernel(o0_ref, o1_ref, o2_ref, l0_ref, l1_ref, l2_ref, w_ref, r_ref, out_ref, a_ref,
                         *, tm, chunk):
    j = pl.program_id(1)

    @pl.when(j == 0)
    def _():
        for r in range(0, tm, chunk):
            ls = [l_ref[r:r + chunk, :] for l_ref in (l0_ref, l1_ref, l2_ref)]
            mx = jnp.maximum(jnp.maximum(ls[0], ls[1]), ls[2])
            es = [jnp.exp(l - mx) for l in ls]
            den = es[0] + es[1] + es[2]
            alphas = [e / den for e in es]
            for h in range(N_HEADS):
                cols = slice(h * V_DIM, (h + 1) * V_DIM)
                acc = None
                for al, o_ref in zip(alphas, (o0_ref, o1_ref, o2_ref)):
                    term = al[:, h:h + 1] * o_ref[r:r + chunk, cols].astype(F32)
                    acc = term if acc is None else acc + term
                a_ref[r:r + chunk, cols] = acc.astype(BF16)

    out_ref[...] = r_ref[...] + jnp.dot(a_ref[...], w_ref[...].astype(BF16), preferred_element_type=F32)


def _combine_proj_residual(outs, lses, w, res, tn):
    t = _tiles()
    tm = t["tm"]
    m, k = outs[0].shape
    n = w.shape[1]
    kern = functools.partial(_combine_proj_kernel, tm=tm, chunk=t["norm_chunk"])
    o_spec = pl.BlockSpec((tm, k), lambda i, j: (i, 0))
    l_spec = pl.BlockSpec((tm, LANES), lambda i, j: (i, 0))
    return pl.pallas_call(
        kern,
        out_shape=jax.ShapeDtypeStruct((m, n), F32),
        grid=(m // tm, n // tn),
        in_specs=[o_spec, o_spec, o_spec, l_spec, l_spec, l_spec,
                  pl.BlockSpec((k, tn), lambda i, j: (0, j)),
                  pl.BlockSpec((tm, tn), lambda i, j: (i, j))],
        out_specs=pl.BlockSpec((tm, tn), lambda i, j: (i, j)),
        scratch_shapes=[pltpu.VMEM((tm, k), BF16)],
        compiler_params=_params("parallel", "arbitrary"),
        name="combine_proj_residual",
    )(*outs, *lses, w, res)


def _ffn_up_kernel(x_ref, halo_ref, g_ref, wg_ref, wv_ref, cwg_ref, cwv_ref, cbg_ref, cbv_ref, o_ref, xn_ref,
                   *, tm, tiles_per_seq, chunk):
    i = pl.program_id(0)
    j = pl.program_id(1)
    halo = CONV_HALO

    @pl.when(j == 0)
    def _():
        _rmsnorm_rows(x_ref, g_ref, xn_ref, halo, tm, chunk)
        first = (i % tiles_per_seq) == 0

        @pl.when(first)
        def _():
            xn_ref[0:halo, :] = jnp.zeros((halo, xn_ref.shape[1]), BF16)

        @pl.when(jnp.logical_not(first))
        def _():
            _rmsnorm_rows(halo_ref, g_ref, xn_ref, 0, halo, halo)

    xn = xn_ref[...]

    def conv_branch(w_ref, cw_ref, cb_ref):
        u = jnp.dot(xn, w_ref[...].astype(BF16), preferred_element_type=F32)
        cw = cw_ref[...]
        return (cw[0:1] * u[halo - 2:halo - 2 + tm] + cw[1:2] * u[halo - 1:halo - 1 + tm]
                + cw[2:3] * u[halo:halo + tm]) + cb_ref[...]

    gate = conv_branch(wg_ref, cwg_ref, cbg_ref)
    val = conv_branch(wv_ref, cwv_ref, cbv_ref)
    o_ref[...] = (gate / (1.0 + jnp.exp(-gate)) * val).astype(BF16)


def _ffn_up(h2d, ln_g, w_up, conv_w, conv_b, seq):
    t = _tiles()
    tm, tn = t["tm"], t["tn_up"]
    m, d = h2d.shape
    dff = w_up.shape[1] // 2
    nj = dff // tn
    halo_blocks = tm // CONV_HALO
    kern = functools.partial(_ffn_up_kernel, tm=tm, tiles_per_seq=seq // tm, chunk=t["norm_chunk"])
    return pl.pallas_call(
        kern,
        out_shape=jax.ShapeDtypeStruct((m, dff), BF16),
        grid=(m // tm, nj),
        in_specs=[
            pl.BlockSpec((tm, d), lambda i, j: (i, 0)),
            pl.BlockSpec((CONV_HALO, d), lambda i, j: (jnp.maximum(i * halo_blocks - 1, 0), 0)),
            pl.BlockSpec((1, d), lambda i, j: (0, 0)),
            pl.BlockSpec((d, tn), lambda i, j: (0, j)),
            pl.BlockSpec((d, tn), lambda i, j: (0, j + nj)),
            pl.BlockSpec((3, tn), lambda i, j: (0, j)),
            pl.BlockSpec((3, tn), lambda i, j: (0, j + nj)),
            pl.BlockSpec((1, tn), lambda i, j: (0, j)),
            pl.BlockSpec((1, tn), lambda i, j: (0, j + nj)),
        ],
        out_specs=pl.BlockSpec((tm, tn), lambda i, j: (i, j)),
        scratch_shapes=[pltpu.VMEM((tm + CONV_HALO, d), BF16)],
        compiler_params=_params("parallel", "arbitrary"),
        name="ffn_up",
    )(h2d, h2d, ln_g.reshape(1, d), w_up, w_up, conv_w, conv_w, conv_b.reshape(1, -1), conv_b.reshape(1, -1))


def _conv_ffn(h2d, ln_g, w_up, conv_w, conv_b, w_down, seq):
    act = _ffn_up(h2d, ln_g, w_up, conv_w, conv_b, seq)
    return _proj_residual(act, w_down, h2d, _tiles()["tn_down"])


def _diff_col_gain(q_gain, k_gain):
    scale = HEAD_DIM ** -0.5
    nqk = 2 * N_HEADS
    gain = jnp.concatenate([jnp.tile(q_gain * scale, nqk), jnp.tile(k_gain, nqk),
                            jnp.ones((N_HEADS * V_DIM,), F32)])
    return gain


def _dil_col_gain(q_gain, k_gain):
    scale = HEAD_DIM ** -0.5
    parts = []
    for g in range(len(SW_DILATIONS)):
        parts += [jnp.tile(q_gain[g] * scale, N_HEADS), jnp.tile(k_gain[g], N_HEADS),
                  jnp.ones((N_HEADS * V_DIM,), F32)]
    return jnp.concatenate(parts)


def _tile_flags(n_cols, tn, normed_ranges):
    flags = []
    for j in range(n_cols // tn):
        lo = j * tn
        flags.append(int(any(a <= lo < b for a, b in normed_ranges)))
    return jnp.asarray(flags, jnp.int32)


def kernel(x, rel_bias, ln_mix, ln_ffn, a_w_qkv, a_q_norm, a_k_norm, a_lambda, a_subln, a_w_o,
           b_w_qkv, b_q_norm, b_k_norm, b_w_o, f_w_up, f_conv_w, f_conv_b, f_w_down):
    batch, seq, d = x.shape
    m = batch * seq
    t = _tiles()
    h = x.reshape(m, d)

    nqk = 2 * N_HEADS * HEAD_DIM
    flags0 = _tile_flags(a_w_qkv.shape[2], t["tn_qkv"], [(0, 2 * nqk)])
    qkv0 = _qkv_proj(h, ln_mix[0], a_w_qkv[0], _diff_col_gain(a_q_norm[0], a_k_norm[0]), flags0)
    lam_init = 0.8 - 0.6 * math.exp(-0.3 * 0)
    attn0 = _diff_attention(qkv0, _diff_bias_table(rel_bias, seq), a_lambda[0], a_subln[0], batch, seq, lam_init)
    h = _proj_residual(attn0, a_w_o[0], h, t["tn_proj"])
    h = _conv_ffn(h, ln_ffn[0], f_w_up[0], f_conv_w[0], f_conv_b[0], f_w_down[0], seq)

    gcols = 2 * N_HEADS * HEAD_DIM + N_HEADS * V_DIM
    nq1 = N_HEADS * HEAD_DIM
    flags1 = _tile_flags(b_w_qkv.shape[2], t["tn_qkv"],
                         [(g * gcols, g * gcols + 2 * nq1) for g in range(len(SW_DILATIONS))])
    qkv1 = _qkv_proj(h, ln_mix[1], b_w_qkv[0], _dil_col_gain(b_q_norm[0], b_k_norm[0]), flags1)
    dil_bias = _dil_bias_table(rel_bias)
    outs, lses = [], []
    for g in range(len(SW_DILATIONS)):
        o, lse = _dilated_group_attention(qkv1, dil_bias, g, batch, seq)
        outs.append(o)
        lses.append(lse)
    h = _combine_proj_residual(outs, lses, b_w_o[0], h, t["tn_proj"])
    h = _conv_ffn(h, ln_ffn[1], f_w_up[1], f_conv_w[1], f_conv_b[1], f_w_down[1], seq)
    return h.reshape(batch, seq, d)
```

```python
import functools
import math

import jax
import jax.numpy as jnp
from jax import lax
from jax.experimental import pallas as pl
from jax.experimental.pallas import tpu as pltpu

F32 = jnp.float32
BF16 = jnp.bfloat16

N_HEADS = 8
HEAD_DIM = 128
V_DIM = 2 * HEAD_DIM
SW_DILATIONS = (1, 4, 16)
SW_STEPS = 128
NUM_BUCKETS = 32
MAX_DISTANCE = 2048
EPS = 1e-6
NEG = -1e30

LANES = 128
COL_TILE = 256
ATTN_TILE = 256
CONV_HALO = 16
VMEM_LIMIT = 60 * 1024 * 1024


def _tiles():
    return dict(tm=1024, tn_qkv=512, tn_proj=512, tn_up=512, tn_down=256, norm_chunk=256)


def _params(*sem):
    return pltpu.CompilerParams(dimension_semantics=sem, vmem_limit_bytes=VMEM_LIMIT)


def _rmsnorm_rows(x_ref, g_ref, out_ref, row0, rows, chunk):
    g = g_ref[...]
    for r in range(0, rows, chunk):
        x = x_ref[r:r + chunk, :]
        rs = lax.rsqrt(jnp.mean(x * x, axis=-1, keepdims=True) + EPS)
        out_ref[row0 + r:row0 + r + chunk, :] = (x * rs * g).astype(BF16)


def _t5_bucket(dist):
    max_exact = NUM_BUCKETS // 2
    n = jnp.maximum(dist, 0)
    nf = jnp.maximum(n, 1).astype(F32)
    large = max_exact + (jnp.log(nf / max_exact) / math.log(MAX_DISTANCE / max_exact)
                         * (NUM_BUCKETS - max_exact)).astype(jnp.int32)
    large = jnp.minimum(large, NUM_BUCKETS - 1)
    return jnp.where(n < max_exact, n, large)


def _bias_lookup(rb_ref, bucket, h):
    val = jnp.zeros(bucket.shape, F32)
    for b in range(NUM_BUCKETS):
        val = jnp.where(bucket == b, rb_ref[b, h], val)
    return val


def _diff_bias_kernel(rb_ref, o_ref):
    h = pl.program_id(0)
    d = pl.program_id(1)
    t = ATTN_TILE
    dist = (d * t + lax.broadcasted_iota(jnp.int32, (t, t), 0)
            - lax.broadcasted_iota(jnp.int32, (t, t), 1))
    val = _bias_lookup(rb_ref, _t5_bucket(dist), h)
    o_ref[0, 0] = jnp.where(dist >= 0, val, NEG)


def _diff_bias_table(rel_bias, seq):
    nd = seq // ATTN_TILE
    return pl.pallas_call(
        _diff_bias_kernel,
        out_shape=jax.ShapeDtypeStruct((N_HEADS, nd, ATTN_TILE, ATTN_TILE), F32),
        grid=(N_HEADS, nd),
        in_specs=[pl.BlockSpec(memory_space=pltpu.SMEM)],
        out_specs=pl.BlockSpec((1, 1, ATTN_TILE, ATTN_TILE), lambda h, d: (h, d, 0, 0)),
        compiler_params=_params("parallel", "parallel"),
        name="diff_bias_table",
    )(rel_bias)


def _dil_bias_kernel(rb_ref, o_ref):
    g = pl.program_id(0)
    h = pl.program_id(1)
    v = pl.program_id(2)
    dil = jnp.left_shift(1, 2 * g)
    a = lax.broadcasted_iota(jnp.int32, (SW_STEPS, 2 * SW_STEPS), 0)
    b = lax.broadcasted_iota(jnp.int32, (SW_STEPS, 2 * SW_STEPS), 1)
    dist_sub = a + SW_STEPS * (1 - v) - b
    val = _bias_lookup(rb_ref, _t5_bucket(dist_sub * dil), h)
    o_ref[0, 0, 0] = jnp.where((dist_sub >= 0) & (dist_sub <= SW_STEPS), val, NEG)


def _dil_bias_table(rel_bias):
    ng = len(SW_DILATIONS)
    return pl.pallas_call(
        _dil_bias_kernel,
        out_shape=jax.ShapeDtypeStruct((ng, N_HEADS, 2, SW_STEPS, 2 * SW_STEPS), F32),
        grid=(ng, N_HEADS, 2),
        in_specs=[pl.BlockSpec(memory_space=pltpu.SMEM)],
        out_specs=pl.BlockSpec((1, 1, 1, SW_STEPS, 2 * SW_STEPS), lambda g, h, v: (g, h, v, 0, 0)),
        compiler_params=_params("parallel", "parallel", "parallel"),
        name="dil_bias_table",
    )(rel_bias)


def _qkv_kernel(flag_ref, x_ref, g_ref, w_ref, cg_ref, *refs, tm, tn, chunk, dils, steps_per_group):
    o_refs = refs[:len(dils)]
    xn_ref, acc_ref = refs[len(dils):]
    j = pl.program_id(1)

    @pl.when(j == 0)
    def _():
        _rmsnorm_rows(x_ref, g_ref, xn_ref, 0, tm, chunk)

    acc = jnp.dot(xn_ref[...], w_ref[...].astype(BF16), preferred_element_type=F32)
    for c in range(tn // LANES):
        acc_ref[c] = acc[:, c * LANES:(c + 1) * LANES]
    normed = flag_ref[j] != 0
    per_tile = COL_TILE // LANES

    def write_group(o_ref, dil):
        n_rows = tm // dil
        for r in range(dil):
            rows = slice(None) if dil == 1 else pl.ds(r, n_rows, stride=dil)
            for c in range(tn // LANES):
                y = acc_ref[c, rows, :]
                rs = lax.rsqrt(jnp.mean(y * y, axis=-1, keepdims=True) + EPS)
                scale = jnp.where(normed, rs, 1.0)
                lo = r * COL_TILE + (c % per_tile) * LANES
                o_ref[c // per_tile, :, lo:lo + LANES] = (
                    y * scale * cg_ref[:, c * LANES:(c + 1) * LANES]).astype(BF16)

    if len(dils) == 1:
        write_group(o_refs[0], dils[0])
    else:
        for g, dil in enumerate(dils):
            pl.when(j // steps_per_group == g)(functools.partial(write_group, o_refs[g], dil))


def _qkv_proj(x2d, ln_g, w, col_gain, tile_flags, dils):
    t = _tiles()
    tm, tn = t["tm"], t["tn_qkv"]
    m, d = x2d.shape
    n = w.shape[2]
    ng = len(dils)
    steps_per_group = n // ng // tn
    group_tiles = n // ng // COL_TILE
    kern = functools.partial(_qkv_kernel, tm=tm, tn=tn, chunk=t["norm_chunk"], dils=dils,
                             steps_per_group=steps_per_group)

    def out_spec(g, dil):
        def index(i, j, f):
            return (jnp.clip(j - g * steps_per_group, 0, steps_per_group - 1), i, 0)
        return pl.BlockSpec((tn // COL_TILE, tm // dil, dil * COL_TILE), index)

    return pl.pallas_call(
        kern,
        out_shape=[jax.ShapeDtypeStruct((group_tiles, m // dil, dil * COL_TILE), BF16) for dil in dils],
        grid_spec=pltpu.PrefetchScalarGridSpec(
            num_scalar_prefetch=1,
            grid=(m // tm, n // tn),
            in_specs=[
                pl.BlockSpec((tm, d), lambda i, j, f: (i, 0)),
                pl.BlockSpec((1, d), lambda i, j, f: (0, 0)),
                pl.BlockSpec((None, d, tn), lambda i, j, f: (0, 0, j)),
                pl.BlockSpec((1, tn), lambda i, j, f: (0, j)),
            ],
            out_specs=[out_spec(g, dil) for g, dil in enumerate(dils)],
            scratch_shapes=[pltpu.VMEM((tm, d), BF16), pltpu.VMEM((tn // LANES, tm, LANES), F32)],
        ),
        compiler_params=_params("parallel", "arbitrary"),
        name="qkv_proj",
    )(tile_flags, x2d, ln_g.reshape(1, d), w, col_gain.reshape(1, n))


def _diff_attn_kernel(q_ref, k_ref, v_ref, bias_ref, lam_ref, sg_ref, o_ref, *, lam_init, seq):
    t = ATTN_TILE
    lam = lam_ref[...]
    lam_full = (jnp.exp(jnp.sum(lam[0:1] * lam[1:2], axis=-1, keepdims=True))
                - jnp.exp(jnp.sum(lam[2:3] * lam[3:4], axis=-1, keepdims=True)) + lam_init)
    gain = sg_ref[...] * (1.0 - lam_init)
    for c in range(seq // t):
        width = (c + 1) * t
        bias = jnp.concatenate([bias_ref[0, c - ki] for ki in range(c + 1)], axis=1)
        maps = []
        for mp in range(2):
            lo = mp * HEAD_DIM
            s = lax.dot_general(q_ref[0, c * t:(c + 1) * t, lo:lo + HEAD_DIM], k_ref[0, 0:width, lo:lo + HEAD_DIM],
                                (((1,), (1,)), ((), ())), preferred_element_type=F32) + bias
            p = jnp.exp(s - jnp.max(s, axis=-1, keepdims=True))
            den = jnp.sum(p, axis=-1, keepdims=True)
            maps.append(jnp.dot(p.astype(BF16), v_ref[0, 0:width, :], preferred_element_type=F32) / den)
        o = maps[0] - lam_full * maps[1]
        rs = lax.rsqrt(jnp.mean(o * o, axis=-1, keepdims=True) + EPS)
        o_ref[c * t:(c + 1) * t, :] = (o * rs * gain).astype(BF16)


def _diff_attention(qkv, bias_tab, lam, subln_g, batch, seq, lam_init):
    t = ATTN_TILE
    nq = seq // t
    m = batch * seq
    kern = functools.partial(_diff_attn_kernel, lam_init=lam_init, seq=seq)
    return pl.pallas_call(
        kern,
        out_shape=jax.ShapeDtypeStruct((m, N_HEADS * V_DIM), BF16),
        grid=(N_HEADS, batch),
        in_specs=[
            pl.BlockSpec((1, seq, COL_TILE), lambda h, b: (h, b, 0)),
            pl.BlockSpec((1, seq, COL_TILE), lambda h, b: (N_HEADS + h, b, 0)),
            pl.BlockSpec((1, seq, COL_TILE), lambda h, b: (2 * N_HEADS + h, b, 0)),
            pl.BlockSpec((1, nq, t, t), lambda h, b: (h, 0, 0, 0)),
            pl.BlockSpec((4, HEAD_DIM), lambda h, b: (0, 0)),
            pl.BlockSpec((1, V_DIM), lambda h, b: (0, 0)),
        ],
        out_specs=pl.BlockSpec((seq, V_DIM), lambda h, b: (b, h)),
        compiler_params=_params("parallel", "parallel"),
        name="diff_attention",
    )(qkv, qkv, qkv, bias_tab, lam, subln_g.reshape(1, V_DIM))


def _dil_attn_kernel(q_ref, k_ref, v_ref, bias_ref, o_ref, lse_ref, *scratch, sub_len, rows, dil):
    blk = SW_STEPS
    r = pl.program_id(1)
    c = pl.program_id(2)
    nblk = rows // blk
    single = sub_len == blk
    lane = lax.broadcasted_iota(jnp.int32, (blk, LANES), 1)
    for n in range(nblk):
        ng = c * nblk + n
        if single:
            start, variant, width = 0, 1, blk
        else:
            start = pl.multiple_of(jnp.maximum(ng - 1, 0) * blk, blk)
            variant = (ng == 0).astype(jnp.int32)
            width = 2 * blk
        if dil == 1:
            out_rows = slice(n * blk, (n + 1) * blk)
        else:
            out_rows = pl.ds(r + ng * (blk * dil), blk, stride=dil)
        lse_tile = jnp.zeros((blk, LANES), F32)
        for h in range(N_HEADS):
            tile, lo = h // 2, (h % 2) * HEAD_DIM
            qt = q_ref[tile, n * blk:(n + 1) * blk, lo:lo + HEAD_DIM]
            kw = k_ref[tile, pl.ds(start, width), lo:lo + HEAD_DIM]
            vw = v_ref[h, pl.ds(start, width), :]
            bias = bias_ref[0, h, variant][:, :width]
            s = lax.dot_general(qt, kw, (((1,), (1,)), ((), ())), preferred_element_type=F32) + bias
            mx = jnp.max(s, axis=-1, keepdims=True)
            p = jnp.exp(s - mx)
            den = jnp.sum(p, axis=-1, keepdims=True)
            o = jnp.dot(p.astype(BF16), vw, preferred_element_type=F32) / den
            if dil == 1:
                o_ref[out_rows, h * V_DIM:(h + 1) * V_DIM] = o.astype(BF16)
            else:
                for part in range(V_DIM // LANES):
                    scratch[0][h * (V_DIM // LANES) + part, out_rows, :] = o[:, part * LANES:(part + 1) * LANES]
            lse_tile = jnp.where(lane == h, mx + jnp.log(den), lse_tile)
        if dil == 1:
            lse_ref[out_rows, :] = lse_tile
        else:
            scratch[1][out_rows, :] = lse_tile

    if dil > 1:
        @pl.when((r == dil - 1) & (c == pl.num_programs(2) - 1))
        def _():
            for slab in range(scratch[0].shape[0]):
                o_ref[:, slab * LANES:(slab + 1) * LANES] = scratch[0][slab].astype(BF16)
            lse_ref[...] = scratch[1][...]


def _dilated_group_attention(qkv, bias_tab, g, batch, seq):
    dil = SW_DILATIONS[g]
    sub_len = seq // dil
    rows = min(sub_len, 512)
    nc = sub_len // rows
    m = batch * seq
    width = N_HEADS * V_DIM
    kern = functools.partial(_dil_attn_kernel, sub_len=sub_len, rows=rows, dil=dil)
    if dil == 1:
        out_specs = (pl.BlockSpec((rows, width), lambda b, r, c: (b * nc + c, 0)),
                     pl.BlockSpec((rows, LANES), lambda b, r, c: (b * nc + c, 0)))
        scratch, sem = [], ("parallel", "parallel", "parallel")
    else:
        out_specs = (pl.BlockSpec((seq, width), lambda b, r, c: (b, 0)),
                     pl.BlockSpec((seq, LANES), lambda b, r, c: (b, 0)))
        scratch = [pltpu.VMEM((width // LANES, seq, LANES), F32), pltpu.VMEM((seq, LANES), F32)]
        sem = ("parallel", "arbitrary", "arbitrary")
    return pl.pallas_call(
        kern,
        out_shape=(jax.ShapeDtypeStruct((m, width), BF16), jax.ShapeDtypeStruct((m, LANES), F32)),
        grid=(batch, dil, nc),
        in_specs=[
            pl.BlockSpec((4, rows, COL_TILE), lambda b, r, c: (0, b * nc + c, r)),
            pl.BlockSpec((4, sub_len, COL_TILE), lambda b, r, c: (1, b, r)),
            pl.BlockSpec((8, sub_len, COL_TILE), lambda b, r, c: (1, b, r)),
            pl.BlockSpec((1, N_HEADS, 2, SW_STEPS, 2 * SW_STEPS), lambda b, r, c: (g, 0, 0, 0, 0)),
        ],
        out_specs=out_specs,
        scratch_shapes=scratch,
        compiler_params=_params(*sem),
        name=f"dilated_attention_g{g}",
    )(qkv, qkv, qkv, bias_tab)


def _proj_res_kernel(a_ref, w_ref, r_ref, o_ref):
    o_ref[...] = r_ref[...] + jnp.dot(a_ref[...], w_ref[...].astype(BF16), preferred_element_type=F32)


def _proj_residual(a, w, layer, res, tn):
    tm = _tiles()["tm"]
    m, k = a.shape
    n = w.shape[2]
    return pl.pallas_call(
        _proj_res_kernel,
        out_shape=jax.ShapeDtypeStruct((m, n), F32),
        grid=(m // tm, n // tn),
        in_specs=[
            pl.BlockSpec((tm, k), lambda i, j: (i, 0)),
            pl.BlockSpec((None, k, tn), lambda i, j: (layer, 0, j)),
            pl.BlockSpec((tm, tn), lambda i, j: (i, j)),
        ],
        out_specs=pl.BlockSpec((tm, tn), lambda i, j: (i, j)),
        compiler_params=_params("parallel", "arbitrary"),
        name="proj_residual",
    )(a, w, res)


def _combine_proj_kernel(o0_ref, o1_ref, o2_ref, l0_ref, l1_ref, l2_ref, w_ref, r_ref, out_ref, a_ref,
                         *, tm, chunk):
    j = pl.program_id(1)

    @pl.when(j == 0)
    def _():
        for r in range(0, tm, chunk):
            ls = [l_ref[r:r + chunk, :] for l_ref in (l0_ref, l1_ref, l2_ref)]
            mx = jnp.maximum(jnp.maximum(ls[0], ls[1]), ls[2])
            es = [jnp.exp(l - mx) for l in ls]
            den = es[0] + es[1] + es[2]
            alphas = [e / den for e in es]
            for h in range(N_HEADS):
                cols = slice(h * V_DIM, (h + 1) * V_DIM)
                acc = None
                for al, o_ref in zip(alphas, (o0_ref, o1_ref, o2_ref)):
                    term = al[:, h:h + 1] * o_ref[r:r + chunk, cols].astype(F32)
                    acc = term if acc is None else acc + term
                a_ref[r:r + chunk, cols] = acc.astype(BF16)

    out_ref[...] = r_ref[...] + jnp.dot(a_ref[...], w_ref[...].astype(BF16), preferred_element_type=F32)


def _combine_proj_residual(outs, lses, w, res, tn):
    t = _tiles()
    tm = t["tm"]
    m, k = outs[0].shape
    n = w.shape[2]
    kern = functools.partial(_combine_proj_kernel, tm=tm, chunk=t["norm_chunk"])
    o_spec = pl.BlockSpec((tm, k), lambda i, j: (i, 0))
    l_spec = pl.BlockSpec((tm, LANES), lambda i, j: (i, 0))
    return pl.pallas_call(
        kern,
        out_shape=jax.ShapeDtypeStruct((m, n), F32),
        grid=(m // tm, n // tn),
        in_specs=[o_spec, o_spec, o_spec, l_spec, l_spec, l_spec,
                  pl.BlockSpec((None, k, tn), lambda i, j: (0, 0, j)),
                  pl.BlockSpec((tm, tn), lambda i, j: (i, j))],
        out_specs=pl.BlockSpec((tm, tn), lambda i, j: (i, j)),
        scratch_shapes=[pltpu.VMEM((tm, k), BF16)],
        compiler_params=_params("parallel", "arbitrary"),
        name="combine_proj_residual",
    )(*outs, *lses, w, res)


def _ffn_up_kernel(x_ref, halo_ref, g_ref, wg_ref, wv_ref, cwg_ref, cwv_ref, cbg_ref, cbv_ref, o_ref, xn_ref,
                   *, tm, tiles_per_seq, chunk):
    i = pl.program_id(0)
    j = pl.program_id(1)
    halo = CONV_HALO

    @pl.when(j == 0)
    def _():
        _rmsnorm_rows(x_ref, g_ref, xn_ref, halo, tm, chunk)
        first = (i % tiles_per_seq) == 0

        @pl.when(first)
        def _():
            xn_ref[0:halo, :] = jnp.zeros((halo, xn_ref.shape[1]), BF16)

        @pl.when(jnp.logical_not(first))
        def _():
            _rmsnorm_rows(halo_ref, g_ref, xn_ref, 0, halo, halo)

    xn = xn_ref[...]

    def conv_branch(w_ref, cw_ref, cb_ref):
        u = jnp.dot(xn, w_ref[...].astype(BF16), preferred_element_type=F32)
        cw = cw_ref[...]
        return (cw[0:1] * u[halo - 2:halo - 2 + tm] + cw[1:2] * u[halo - 1:halo - 1 + tm]
                + cw[2:3] * u[halo:halo + tm]) + cb_ref[...]

    gate = conv_branch(wg_ref, cwg_ref, cbg_ref)
    val = conv_branch(wv_ref, cwv_ref, cbv_ref)
    o_ref[...] = (gate / (1.0 + jnp.exp(-gate)) * val).astype(BF16)


def _ffn_up(h2d, ln_g, w_up, layer, conv_w, conv_b, seq):
    t = _tiles()
    tm, tn = t["tm"], t["tn_up"]
    m, d = h2d.shape
    dff = w_up.shape[2] // 2
    nj = dff // tn
    halo_blocks = tm // CONV_HALO
    kern = functools.partial(_ffn_up_kernel, tm=tm, tiles_per_seq=seq // tm, chunk=t["norm_chunk"])
    return pl.pallas_call(
        kern,
        out_shape=jax.ShapeDtypeStruct((m, dff), BF16),
        grid=(m // tm, nj),
        in_specs=[
            pl.BlockSpec((tm, d), lambda i, j: (i, 0)),
            pl.BlockSpec((CONV_HALO, d), lambda i, j: (jnp.maximum(i * halo_blocks - 1, 0), 0)),
            pl.BlockSpec((1, d), lambda i, j: (0, 0)),
            pl.BlockSpec((None, d, tn), lambda i, j: (layer, 0, j)),
            pl.BlockSpec((None, d, tn), lambda i, j: (layer, 0, j + nj)),
            pl.BlockSpec((3, tn), lambda i, j: (0, j)),
            pl.BlockSpec((3, tn), lambda i, j: (0, j + nj)),
            pl.BlockSpec((1, tn), lambda i, j: (0, j)),
            pl.BlockSpec((1, tn), lambda i, j: (0, j + nj)),
        ],
        out_specs=pl.BlockSpec((tm, tn), lambda i, j: (i, j)),
        scratch_shapes=[pltpu.VMEM((tm + CONV_HALO, d), BF16)],
        compiler_params=_params("parallel", "arbitrary"),
        name="ffn_up",
    )(h2d, h2d, ln_g.reshape(1, d), w_up, w_up, conv_w, conv_w, conv_b.reshape(1, -1), conv_b.reshape(1, -1))


def _conv_ffn(h2d, ln_g, w_up, conv_w, conv_b, w_down, layer, seq):
    act = _ffn_up(h2d, ln_g, w_up, layer, conv_w, conv_b, seq)
    return _proj_residual(act, w_down, layer, h2d, _tiles()["tn_down"])


def _diff_col_gain(q_gain, k_gain):
    scale = HEAD_DIM ** -0.5
    nqk = 2 * N_HEADS
    gain = jnp.concatenate([jnp.tile(q_gain * scale, nqk), jnp.tile(k_gain, nqk),
                            jnp.ones((N_HEADS * V_DIM,), F32)])
    return gain


def _dil_col_gain(q_gain, k_gain):
    scale = HEAD_DIM ** -0.5
    parts = []
    for g in range(len(SW_DILATIONS)):
        parts += [jnp.tile(q_gain[g] * scale, N_HEADS), jnp.tile(k_gain[g], N_HEADS),
                  jnp.ones((N_HEADS * V_DIM,), F32)]
    return jnp.concatenate(parts)


def _tile_flags(n_cols, tn, normed_ranges):
    flags = []
    for j in range(n_cols // tn):
        lo = j * tn
        flags.append(int(any(a <= lo < b for a, b in normed_ranges)))
    return jnp.asarray(flags, jnp.int32)


def kernel(x, rel_bias, ln_mix, ln_ffn, a_w_qkv, a_q_norm, a_k_norm, a_lambda, a_subln, a_w_o,
           b_w_qkv, b_q_norm, b_k_norm, b_w_o, f_w_up, f_conv_w, f_conv_b, f_w_down):
    batch, seq, d = x.shape
    m = batch * seq
    t = _tiles()
    h = x.reshape(m, d)

    nqk = 2 * N_HEADS * HEAD_DIM
    flags0 = _tile_flags(a_w_qkv.shape[2], t["tn_qkv"], [(0, 2 * nqk)])
    (qkv0,) = _qkv_proj(h, ln_mix[0], a_w_qkv, _diff_col_gain(a_q_norm[0], a_k_norm[0]), flags0, (1,))
    lam_init = 0.8 - 0.6 * math.exp(-0.3 * 0)
    attn0 = _diff_attention(qkv0, _diff_bias_table(rel_bias, seq), a_lambda[0], a_subln[0], batch, seq, lam_init)
    h = _proj_residual(attn0, a_w_o, 0, h, t["tn_proj"])
    h = _conv_ffn(h, ln_ffn[0], f_w_up, f_conv_w[0], f_conv_b[0], f_w_down, 0, seq)

    gcols = 2 * N_HEADS * HEAD_DIM + N_HEADS * V_DIM
    nq1 = N_HEADS * HEAD_DIM
    flags1 = _tile_flags(b_w_qkv.shape[2], t["tn_qkv"],
                         [(g * gcols, g * gcols + 2 * nq1) for g in range(len(SW_DILATIONS))])
    qkv1 = _qkv_proj(h, ln_mix[1], b_w_qkv, _dil_col_gain(b_q_norm[0], b_k_norm[0]), flags1, SW_DILATIONS)
    dil_bias = _dil_bias_table(rel_bias)
    outs, lses = [], []
    for g in range(len(SW_DILATIONS)):
        o, lse = _dilated_group_attention(qkv1[g], dil_bias, g, batch, seq)
        outs.append(o)
        lses.append(lse)
    h = _combine_proj_residual(outs, lses, b_w_o, h, t["tn_proj"])
    h = _conv_ffn(h, ln_ffn[1], f_w_up, f_conv_w[1], f_conv_b[1], f_w_down, 1, seq)
    return h.reshape(batch, seq, d)
```

```python
import functools
import math

import jax
import jax.numpy as jnp
from jax import lax
from jax.experimental import pallas as pl
from jax.experimental.pallas import tpu as pltpu

F32 = jnp.float32
BF16 = jnp.bfloat16

N_HEADS = 8
HEAD_DIM = 128
V_DIM = 2 * HEAD_DIM
SW_DILATIONS = (1, 4, 16)
SW_STEPS = 128
NUM_BUCKETS = 32
MAX_DISTANCE = 2048
EPS = 1e-6
NEG = -1e30

LANES = 128
COL_TILE = 256
ATTN_TILE = 256
CONV_HALO = 16
VMEM_LIMIT = 60 * 1024 * 1024


def _tiles():
    return dict(tm=1024, tn_qkv=1024, tn_proj=512, tn_up=512, tn_down=512, norm_chunk=256)


def _params(*sem):
    return pltpu.CompilerParams(dimension_semantics=sem, vmem_limit_bytes=VMEM_LIMIT)


def _rmsnorm_rows(x_ref, g_ref, out_ref, row0, rows, chunk):
    g = g_ref[...]
    for r in range(0, rows, chunk):
        x = x_ref[r:r + chunk, :]
        rs = lax.rsqrt(jnp.mean(x * x, axis=-1, keepdims=True) + EPS)
        out_ref[row0 + r:row0 + r + chunk, :] = (x * rs * g).astype(BF16)


def _t5_bucket(dist):
    max_exact = NUM_BUCKETS // 2
    n = jnp.maximum(dist, 0)
    nf = jnp.maximum(n, 1).astype(F32)
    large = max_exact + (jnp.log(nf / max_exact) / math.log(MAX_DISTANCE / max_exact)
                         * (NUM_BUCKETS - max_exact)).astype(jnp.int32)
    large = jnp.minimum(large, NUM_BUCKETS - 1)
    return jnp.where(n < max_exact, n, large)


def _bias_lookup(rb_ref, bucket, h):
    val = jnp.zeros(bucket.shape, F32)
    for b in range(NUM_BUCKETS):
        val = jnp.where(bucket == b, rb_ref[b, h], val)
    return val


def _diff_bias_kernel(rb_ref, o_ref):
    h = pl.program_id(0)
    d = pl.program_id(1)
    t = ATTN_TILE
    dist = (d * t + lax.broadcasted_iota(jnp.int32, (t, t), 0)
            - lax.broadcasted_iota(jnp.int32, (t, t), 1))
    val = _bias_lookup(rb_ref, _t5_bucket(dist), h)
    o_ref[0, 0] = jnp.where(dist >= 0, val, NEG)


def _diff_bias_table(rel_bias, seq):
    nd = seq // ATTN_TILE
    return pl.pallas_call(
        _diff_bias_kernel,
        out_shape=jax.ShapeDtypeStruct((N_HEADS, nd, ATTN_TILE, ATTN_TILE), F32),
        grid=(N_HEADS, nd),
        in_specs=[pl.BlockSpec(memory_space=pltpu.SMEM)],
        out_specs=pl.BlockSpec((1, 1, ATTN_TILE, ATTN_TILE), lambda h, d: (h, d, 0, 0)),
        compiler_params=_params("parallel", "parallel"),
        name="diff_bias_table",
    )(rel_bias)


def _dil_bias_kernel(rb_ref, o_ref):
    g = pl.program_id(0)
    h = pl.program_id(1)
    v = pl.program_id(2)
    dil = jnp.left_shift(1, 2 * g)
    a = lax.broadcasted_iota(jnp.int32, (SW_STEPS, 2 * SW_STEPS), 0)
    b = lax.broadcasted_iota(jnp.int32, (SW_STEPS, 2 * SW_STEPS), 1)
    dist_sub = a + SW_STEPS * (1 - v) - b
    val = _bias_lookup(rb_ref, _t5_bucket(dist_sub * dil), h)
    o_ref[0, 0, 0] = jnp.where((dist_sub >= 0) & (dist_sub <= SW_STEPS), val, NEG)


def _dil_bias_table(rel_bias):
    ng = len(SW_DILATIONS)
    return pl.pallas_call(
        _dil_bias_kernel,
        out_shape=jax.ShapeDtypeStruct((ng, N_HEADS, 2, SW_STEPS, 2 * SW_STEPS), F32),
        grid=(ng, N_HEADS, 2),
        in_specs=[pl.BlockSpec(memory_space=pltpu.SMEM)],
        out_specs=pl.BlockSpec((1, 1, 1, SW_STEPS, 2 * SW_STEPS), lambda g, h, v: (g, h, v, 0, 0)),
        compiler_params=_params("parallel", "parallel", "parallel"),
        name="dil_bias_table",
    )(rel_bias)


def _qkv_kernel(flag_ref, x_ref, g_ref, w_ref, cg_ref, *refs, tm, tn, chunk, dils, steps_per_group):
    o_refs = refs[:len(dils)]
    xn_ref, acc_ref = refs[len(dils):]
    j = pl.program_id(1)

    @pl.when(j == 0)
    def _():
        _rmsnorm_rows(x_ref, g_ref, xn_ref, 0, tm, chunk)

    acc = jnp.dot(xn_ref[...], w_ref[...].astype(BF16), preferred_element_type=F32)
    for c in range(tn // LANES):
        acc_ref[c] = acc[:, c * LANES:(c + 1) * LANES]
    normed = flag_ref[j] != 0
    per_tile = COL_TILE // LANES

    def write_group(o_ref, dil):
        n_rows = tm // dil
        for r in range(dil):
            rows = slice(None) if dil == 1 else pl.ds(r, n_rows, stride=dil)
            for c in range(tn // LANES):
                y = acc_ref[c, rows, :]
                rs = lax.rsqrt(jnp.mean(y * y, axis=-1, keepdims=True) + EPS)
                scale = jnp.where(normed, rs, 1.0)
                lo = r * COL_TILE + (c % per_tile) * LANES
                o_ref[c // per_tile, :, lo:lo + LANES] = (
                    y * scale * cg_ref[:, c * LANES:(c + 1) * LANES]).astype(BF16)

    if len(dils) == 1:
        write_group(o_refs[0], dils[0])
    else:
        for g, dil in enumerate(dils):
            pl.when(j // steps_per_group == g)(functools.partial(write_group, o_refs[g], dil))


def _qkv_proj(x2d, ln_g, w, col_gain, tile_flags, dils):
    t = _tiles()
    tm, tn = t["tm"], t["tn_qkv"]
    m, d = x2d.shape
    n = w.shape[2]
    ng = len(dils)
    steps_per_group = n // ng // tn
    group_tiles = n // ng // COL_TILE
    kern = functools.partial(_qkv_kernel, tm=tm, tn=tn, chunk=t["norm_chunk"], dils=dils,
                             steps_per_group=steps_per_group)

    def out_spec(g, dil):
        def index(i, j, f):
            return (jnp.clip(j - g * steps_per_group, 0, steps_per_group - 1), i, 0)
        return pl.BlockSpec((tn // COL_TILE, tm // dil, dil * COL_TILE), index)

    return pl.pallas_call(
        kern,
        out_shape=[jax.ShapeDtypeStruct((group_tiles, m // dil, dil * COL_TILE), BF16) for dil in dils],
        grid_spec=pltpu.PrefetchScalarGridSpec(
            num_scalar_prefetch=1,
            grid=(m // tm, n // tn),
            in_specs=[
                pl.BlockSpec((tm, d), lambda i, j, f: (i, 0)),
                pl.BlockSpec((1, d), lambda i, j, f: (0, 0)),
                pl.BlockSpec((None, d, tn), lambda i, j, f: (0, 0, j)),
                pl.BlockSpec((1, tn), lambda i, j, f: (0, j)),
            ],
            out_specs=[out_spec(g, dil) for g, dil in enumerate(dils)],
            scratch_shapes=[pltpu.VMEM((tm, d), BF16), pltpu.VMEM((tn // LANES, tm, LANES), F32)],
        ),
        compiler_params=_params("parallel", "arbitrary"),
        name="qkv_proj",
    )(tile_flags, x2d, ln_g.reshape(1, d), w, col_gain.reshape(1, n))


def _diff_attn_kernel(q_ref, k_ref, v_ref, bias_ref, lam_ref, sg_ref, o_ref, *, lam_init, seq):
    t = ATTN_TILE
    lam = lam_ref[...]
    lam_full = (jnp.exp(jnp.sum(lam[0:1] * lam[1:2], axis=-1, keepdims=True))
                - jnp.exp(jnp.sum(lam[2:3] * lam[3:4], axis=-1, keepdims=True)) + lam_init)
    gain = sg_ref[...] * (1.0 - lam_init)
    for c in range(seq // t):
        width = (c + 1) * t
        bias = jnp.concatenate([bias_ref[0, c - ki] for ki in range(c + 1)], axis=1)
        maps = []
        for mp in range(2):
            lo = mp * HEAD_DIM
            s = lax.dot_general(q_ref[0, c * t:(c + 1) * t, lo:lo + HEAD_DIM], k_ref[0, 0:width, lo:lo + HEAD_DIM],
                                (((1,), (1,)), ((), ())), preferred_element_type=F32) + bias
            p = jnp.exp(s - jnp.max(s, axis=-1, keepdims=True))
            den = jnp.sum(p, axis=-1, keepdims=True)
            maps.append(jnp.dot(p.astype(BF16), v_ref[0, 0:width, :], preferred_element_type=F32) / den)
        o = maps[0] - lam_full * maps[1]
        rs = lax.rsqrt(jnp.mean(o * o, axis=-1, keepdims=True) + EPS)
        o_ref[c * t:(c + 1) * t, :] = (o * rs * gain).astype(BF16)


def _diff_attention(qkv, bias_tab, lam, subln_g, batch, seq, lam_init):
    t = ATTN_TILE
    nq = seq // t
    m = batch * seq
    kern = functools.partial(_diff_attn_kernel, lam_init=lam_init, seq=seq)
    return pl.pallas_call(
        kern,
        out_shape=jax.ShapeDtypeStruct((m, N_HEADS * V_DIM), BF16),
        grid=(N_HEADS, batch),
        in_specs=[
            pl.BlockSpec((1, seq, COL_TILE), lambda h, b: (h, b, 0)),
            pl.BlockSpec((1, seq, COL_TILE), lambda h, b: (N_HEADS + h, b, 0)),
            pl.BlockSpec((1, seq, COL_TILE), lambda h, b: (2 * N_HEADS + h, b, 0)),
            pl.BlockSpec((1, nq, t, t), lambda h, b: (h, 0, 0, 0)),
            pl.BlockSpec((4, HEAD_DIM), lambda h, b: (0, 0)),
            pl.BlockSpec((1, V_DIM), lambda h, b: (0, 0)),
        ],
        out_specs=pl.BlockSpec((seq, V_DIM), lambda h, b: (b, h)),
        compiler_params=_params("parallel", "parallel"),
        name="diff_attention",
    )(qkv, qkv, qkv, bias_tab, lam, subln_g.reshape(1, V_DIM))


def _dil_attn_kernel(q_ref, k_ref, v_ref, bias_ref, o_ref, lse_ref, *scratch, sub_len, rows, dil):
    blk = SW_STEPS
    r = pl.program_id(1)
    c = pl.program_id(2)
    nblk = rows // blk
    single = sub_len == blk
    lane = lax.broadcasted_iota(jnp.int32, (blk, LANES), 1)
    for n in range(nblk):
        ng = c * nblk + n
        if single:
            start, variant, width = 0, 1, blk
        else:
            start = pl.multiple_of(jnp.maximum(ng - 1, 0) * blk, blk)
            variant = (ng == 0).astype(jnp.int32)
            width = 2 * blk
        if dil == 1:
            out_rows = slice(n * blk, (n + 1) * blk)
        else:
            out_rows = pl.ds(r + ng * (blk * dil), blk, stride=dil)
        lse_tile = jnp.zeros((blk, LANES), F32)
        for h in range(N_HEADS):
            tile, lo = h // 2, (h % 2) * HEAD_DIM
            qt = q_ref[tile, n * blk:(n + 1) * blk, lo:lo + HEAD_DIM]
            kw = k_ref[tile, pl.ds(start, width), lo:lo + HEAD_DIM]
            vw = v_ref[h, pl.ds(start, width), :]
            bias = bias_ref[0, h, variant][:, :width]
            s = lax.dot_general(qt, kw, (((1,), (1,)), ((), ())), preferred_element_type=F32) + bias
            mx = jnp.max(s, axis=-1, keepdims=True)
            p = jnp.exp(s - mx)
            den = jnp.sum(p, axis=-1, keepdims=True)
            o = jnp.dot(p.astype(BF16), vw, preferred_element_type=F32) / den
            if dil == 1:
                o_ref[out_rows, h * V_DIM:(h + 1) * V_DIM] = o.astype(BF16)
            else:
                for part in range(V_DIM // LANES):
                    scratch[0][h * (V_DIM // LANES) + part, out_rows, :] = o[:, part * LANES:(part + 1) * LANES]
            lse_tile = jnp.where(lane == h, mx + jnp.log(den), lse_tile)
        if dil == 1:
            lse_ref[out_rows, :] = lse_tile
        else:
            scratch[1][out_rows, :] = lse_tile

    if dil > 1:
        @pl.when((r == dil - 1) & (c == pl.num_programs(2) - 1))
        def _():
            for slab in range(scratch[0].shape[0]):
                o_ref[:, slab * LANES:(slab + 1) * LANES] = scratch[0][slab].astype(BF16)
            lse_ref[...] = scratch[1][...]


def _dilated_group_attention(qkv, bias_tab, g, batch, seq):
    dil = SW_DILATIONS[g]
    sub_len = seq // dil
    rows = min(sub_len, 512)
    nc = sub_len // rows
    m = batch * seq
    width = N_HEADS * V_DIM
    kern = functools.partial(_dil_attn_kernel, sub_len=sub_len, rows=rows, dil=dil)
    if dil == 1:
        out_specs = (pl.BlockSpec((rows, width), lambda b, r, c: (b * nc + c, 0)),
                     pl.BlockSpec((rows, LANES), lambda b, r, c: (b * nc + c, 0)))
        scratch, sem = [], ("parallel", "parallel", "parallel")
    else:
        out_specs = (pl.BlockSpec((seq, width), lambda b, r, c: (b, 0)),
                     pl.BlockSpec((seq, LANES), lambda b, r, c: (b, 0)))
        scratch = [pltpu.VMEM((width // LANES, seq, LANES), F32), pltpu.VMEM((seq, LANES), F32)]
        sem = ("parallel", "arbitrary", "arbitrary")
    return pl.pallas_call(
        kern,
        out_shape=(jax.ShapeDtypeStruct((m, width), BF16), jax.ShapeDtypeStruct((m, LANES), F32)),
        grid=(batch, dil, nc),
        in_specs=[
            pl.BlockSpec((4, rows, COL_TILE), lambda b, r, c: (0, b * nc + c, r)),
            pl.BlockSpec((4, sub_len, COL_TILE), lambda b, r, c: (1, b, r)),
            pl.BlockSpec((8, sub_len, COL_TILE), lambda b, r, c: (1, b, r)),
            pl.BlockSpec((1, N_HEADS, 2, SW_STEPS, 2 * SW_STEPS), lambda b, r, c: (g, 0, 0, 0, 0)),
        ],
        out_specs=out_specs,
        scratch_shapes=scratch,
        compiler_params=_params(*sem),
        name=f"dilated_attention_g{g}",
    )(qkv, qkv, qkv, bias_tab)


def _proj_res_kernel(a_ref, w_ref, r_ref, o_ref):
    o_ref[...] = r_ref[...] + jnp.dot(a_ref[...], w_ref[...].astype(BF16), preferred_element_type=F32)


def _proj_residual(a, w, layer, res, tn):
    tm = _tiles()["tm"]
    m, k = a.shape
    n = w.shape[2]
    return pl.pallas_call(
        _proj_res_kernel,
        out_shape=jax.ShapeDtypeStruct((m, n), F32),
        grid=(m // tm, n // tn),
        in_specs=[
            pl.BlockSpec((tm, k), lambda i, j: (i, 0)),
            pl.BlockSpec((None, k, tn), lambda i, j: (layer, 0, j)),
            pl.BlockSpec((tm, tn), lambda i, j: (i, j)),
        ],
        out_specs=pl.BlockSpec((tm, tn), lambda i, j: (i, j)),
        compiler_params=_params("parallel", "arbitrary"),
        name="proj_residual",
    )(a, w, res)


def _combine_proj_kernel(o0_ref, o1_ref, o2_ref, l0_ref, l1_ref, l2_ref, w_ref, r_ref, out_ref, a_ref,
                         *, tm, chunk):
    j = pl.program_id(1)

    @pl.when(j == 0)
    def _():
        for r in range(0, tm, chunk):
            ls = [l_ref[r:r + chunk, :] for l_ref in (l0_ref, l1_ref, l2_ref)]
            mx = jnp.maximum(jnp.maximum(ls[0], ls[1]), ls[2])
            es = [jnp.exp(l - mx) for l in ls]
            den = es[0] + es[1] + es[2]
            alphas = [e / den for e in es]
            for h in range(N_HEADS):
                cols = slice(h * V_DIM, (h + 1) * V_DIM)
                acc = None
                for al, o_ref in zip(alphas, (o0_ref, o1_ref, o2_ref)):
                    term = al[:, h:h + 1] * o_ref[r:r + chunk, cols].astype(F32)
                    acc = term if acc is None else acc + term
                a_ref[r:r + chunk, cols] = acc.astype(BF16)

    out_ref[...] = r_ref[...] + jnp.dot(a_ref[...], w_ref[...].astype(BF16), preferred_element_type=F32)


def _combine_proj_residual(outs, lses, w, res, tn):
    t = _tiles()
    tm = t["tm"]
    m, k = outs[0].shape
    n = w.shape[2]
    kern = functools.partial(_combine_proj_kernel, tm=tm, chunk=t["norm_chunk"])
    o_spec = pl.BlockSpec((tm, k), lambda i, j: (i, 0))
    l_spec = pl.BlockSpec((tm, LANES), lambda i, j: (i, 0))
    return pl.pallas_call(
        kern,
        out_shape=jax.ShapeDtypeStruct((m, n), F32),
        grid=(m // tm, n // tn),
        in_specs=[o_spec, o_spec, o_spec, l_spec, l_spec, l_spec,
                  pl.BlockSpec((None, k, tn), lambda i, j: (0, 0, j)),
                  pl.BlockSpec((tm, tn), lambda i, j: (i, j))],
        out_specs=pl.BlockSpec((tm, tn), lambda i, j: (i, j)),
        scratch_shapes=[pltpu.VMEM((tm, k), BF16)],
        compiler_params=_params("parallel", "arbitrary"),
        name="combine_proj_residual",
    )(*outs, *lses, w, res)


def _ffn_up_kernel(x_ref, halo_ref, g_ref, wg_ref, wv_ref, cwg_ref, cwv_ref, cbg_ref, cbv_ref, o_ref, xn_ref,
                   *, tm, tiles_per_seq, chunk):
    i = pl.program_id(0)
    j = pl.program_id(1)
    halo = CONV_HALO

    @pl.when(j == 0)
    def _():
        _rmsnorm_rows(x_ref, g_ref, xn_ref, halo, tm, chunk)
        first = (i % tiles_per_seq) == 0

        @pl.when(first)
        def _():
            xn_ref[0:halo, :] = jnp.zeros((halo, xn_ref.shape[1]), BF16)

        @pl.when(jnp.logical_not(first))
        def _():
            _rmsnorm_rows(halo_ref, g_ref, xn_ref, 0, halo, halo)

    xn = xn_ref[...]

    def conv_branch(w_ref, cw_ref, cb_ref):
        u = jnp.dot(xn, w_ref[...].astype(BF16), preferred_element_type=F32)
        cw = cw_ref[...]
        return (cw[0:1] * u[halo - 2:halo - 2 + tm] + cw[1:2] * u[halo - 1:halo - 1 + tm]
                + cw[2:3] * u[halo:halo + tm]) + cb_ref[...]

    gate = conv_branch(wg_ref, cwg_ref, cbg_ref)
    val = conv_branch(wv_ref, cwv_ref, cbv_ref)
    o_ref[...] = (gate / (1.0 + jnp.exp(-gate)) * val).astype(BF16)


def _ffn_up(h2d, ln_g, w_up, layer, conv_w, conv_b, seq):
    t = _tiles()
    tm, tn = t["tm"], t["tn_up"]
    m, d = h2d.shape
    dff = w_up.shape[2] // 2
    nj = dff // tn
    halo_blocks = tm // CONV_HALO
    kern = functools.partial(_ffn_up_kernel, tm=tm, tiles_per_seq=seq // tm, chunk=t["norm_chunk"])
    return pl.pallas_call(
        kern,
        out_shape=jax.ShapeDtypeStruct((m, dff), BF16),
        grid=(m // tm, nj),
        in_specs=[
            pl.BlockSpec((tm, d), lambda i, j: (i, 0)),
            pl.BlockSpec((CONV_HALO, d), lambda i, j: (jnp.maximum(i * halo_blocks - 1, 0), 0)),
            pl.BlockSpec((1, d), lambda i, j: (0, 0)),
            pl.BlockSpec((None, d, tn), lambda i, j: (layer, 0, j)),
            pl.BlockSpec((None, d, tn), lambda i, j: (layer, 0, j + nj)),
            pl.BlockSpec((3, tn), lambda i, j: (0, j)),
            pl.BlockSpec((3, tn), lambda i, j: (0, j + nj)),
            pl.BlockSpec((1, tn), lambda i, j: (0, j)),
            pl.BlockSpec((1, tn), lambda i, j: (0, j + nj)),
        ],
        out_specs=pl.BlockSpec((tm, tn), lambda i, j: (i, j)),
        scratch_shapes=[pltpu.VMEM((tm + CONV_HALO, d), BF16)],
        compiler_params=_params("parallel", "arbitrary"),
        name="ffn_up",
    )(h2d, h2d, ln_g.reshape(1, d), w_up, w_up, conv_w, conv_w, conv_b.reshape(1, -1), conv_b.reshape(1, -1))


def _conv_ffn(h2d, ln_g, w_up, conv_w, conv_b, w_down, layer, seq):
    act = _ffn_up(h2d, ln_g, w_up, layer, conv_w, conv_b, seq)
    return _proj_residual(act, w_down, layer, h2d, _tiles()["tn_down"])


def _diff_col_gain(q_gain, k_gain):
    scale = HEAD_DIM ** -0.5
    nqk = 2 * N_HEADS
    gain = jnp.concatenate([jnp.tile(q_gain * scale, nqk), jnp.tile(k_gain, nqk),
                            jnp.ones((N_HEADS * V_DIM,), F32)])
    return gain


def _dil_col_gain(q_gain, k_gain):
    scale = HEAD_DIM ** -0.5
    parts = []
    for g in range(len(SW_DILATIONS)):
        parts += [jnp.tile(q_gain[g] * scale, N_HEADS), jnp.tile(k_gain[g], N_HEADS),
                  jnp.ones((N_HEADS * V_DIM,), F32)]
    return jnp.concatenate(parts)


def _tile_flags(n_cols, tn, normed_ranges):
    flags = []
    for j in range(n_cols // tn):
        lo = j * tn
        flags.append(int(any(a <= lo < b for a, b in normed_ranges)))
    return jnp.asarray(flags, jnp.int32)


def kernel(x, rel_bias, ln_mix, ln_ffn, a_w_qkv, a_q_norm, a_k_norm, a_lambda, a_subln, a_w_o,
           b_w_qkv, b_q_norm, b_k_norm, b_w_o, f_w_up, f_conv_w, f_conv_b, f_w_down):
    batch, seq, d = x.shape
    m = batch * seq
    t = _tiles()
    h = x.reshape(m, d)

    nqk = 2 * N_HEADS * HEAD_DIM
    flags0 = _tile_flags(a_w_qkv.shape[2], t["tn_qkv"], [(0, 2 * nqk)])
    (qkv0,) = _qkv_proj(h, ln_mix[0], a_w_qkv, _diff_col_gain(a_q_norm[0], a_k_norm[0]), flags0, (1,))
    lam_init = 0.8 - 0.6 * math.exp(-0.3 * 0)
    attn0 = _diff_attention(qkv0, _diff_bias_table(rel_bias, seq), a_lambda[0], a_subln[0], batch, seq, lam_init)
    h = _proj_residual(attn0, a_w_o, 0, h, t["tn_proj"])
    h = _conv_ffn(h, ln_ffn[0], f_w_up, f_conv_w[0], f_conv_b[0], f_w_down, 0, seq)

    gcols = 2 * N_HEADS * HEAD_DIM + N_HEADS * V_DIM
    nq1 = N_HEADS * HEAD_DIM
    flags1 = _tile_flags(b_w_qkv.shape[2], t["tn_qkv"],
                         [(g * gcols, g * gcols + 2 * nq1) for g in range(len(SW_DILATIONS))])
    qkv1 = _qkv_proj(h, ln_mix[1], b_w_qkv, _dil_col_gain(b_q_norm[0], b_k_norm[0]), flags1, SW_DILATIONS)
    dil_bias = _dil_bias_table(rel_bias)
    outs, lses = [], []
    for g in range(len(SW_DILATIONS)):
        o, lse = _dilated_group_attention(qkv1[g], dil_bias, g, batch, seq)
        outs.append(o)
        lses.append(lse)
    h = _combine_proj_residual(outs, lses, b_w_o, h, t["tn_proj"])
    h = _conv_ffn(h, ln_ffn[1], f_w_up, f_conv_w[1], f_conv_b[1], f_w_down, 1, seq)
    return h.reshape(batch, seq, d)
```

```python
import functools
import math

import jax
import jax.numpy as jnp
from jax import lax
from jax.experimental import pallas as pl
from jax.experimental.pallas import tpu as pltpu

F32 = jnp.float32
BF16 = jnp.bfloat16

N_HEADS = 8
HEAD_DIM = 128
V_DIM = 2 * HEAD_DIM
SW_DILATIONS = (1, 4, 16)
SW_STEPS = 128
NUM_BUCKETS = 32
MAX_DISTANCE = 2048
EPS = 1e-6
NEG = -1e30

LANES = 128
COL_TILE = 256
ATTN_TILE = 256
CONV_HALO = 16
VMEM_LIMIT = 60 * 1024 * 1024


def _tiles():
    return dict(tm=1024, tn_qkv=1024, tn_proj=512, tn_up=512, tn_down=512, norm_chunk=256)


def _params(*sem):
    return pltpu.CompilerParams(dimension_semantics=sem, vmem_limit_bytes=VMEM_LIMIT)


def _rmsnorm_rows(x_ref, g_ref, out_ref, row0, rows, chunk):
    g = g_ref[...]
    for r in range(0, rows, chunk):
        x = x_ref[r:r + chunk, :]
        rs = lax.rsqrt(jnp.mean(x * x, axis=-1, keepdims=True) + EPS)
        out_ref[row0 + r:row0 + r + chunk, :] = (x * rs * g).astype(BF16)


def _t5_bucket(dist):
    max_exact = NUM_BUCKETS // 2
    n = jnp.maximum(dist, 0)
    nf = jnp.maximum(n, 1).astype(F32)
    large = max_exact + (jnp.log(nf / max_exact) / math.log(MAX_DISTANCE / max_exact)
                         * (NUM_BUCKETS - max_exact)).astype(jnp.int32)
    large = jnp.minimum(large, NUM_BUCKETS - 1)
    return jnp.where(n < max_exact, n, large)


def _bias_lookup(rb_ref, bucket, h):
    val = jnp.zeros(bucket.shape, F32)
    for b in range(NUM_BUCKETS):
        val = jnp.where(bucket == b, rb_ref[b, h], val)
    return val


def _diff_bias_kernel(rb_ref, o_ref):
    h = pl.program_id(0)
    d = pl.program_id(1)
    t = ATTN_TILE
    dist = (d * t + lax.broadcasted_iota(jnp.int32, (t, t), 0)
            - lax.broadcasted_iota(jnp.int32, (t, t), 1))
    val = _bias_lookup(rb_ref, _t5_bucket(dist), h)
    o_ref[0, 0] = jnp.where(dist >= 0, val, NEG)


def _diff_bias_table(rel_bias, seq):
    nd = seq // ATTN_TILE
    return pl.pallas_call(
        _diff_bias_kernel,
        out_shape=jax.ShapeDtypeStruct((N_HEADS, nd, ATTN_TILE, ATTN_TILE), F32),
        grid=(N_HEADS, nd),
        in_specs=[pl.BlockSpec(memory_space=pltpu.SMEM)],
        out_specs=pl.BlockSpec((1, 1, ATTN_TILE, ATTN_TILE), lambda h, d: (h, d, 0, 0)),
        compiler_params=_params("parallel", "parallel"),
        name="diff_bias_table",
    )(rel_bias)


def _dil_bias_kernel(rb_ref, o_ref):
    g = pl.program_id(0)
    h = pl.program_id(1)
    v = pl.program_id(2)
    dil = jnp.left_shift(1, 2 * g)
    a = lax.broadcasted_iota(jnp.int32, (SW_STEPS, 2 * SW_STEPS), 0)
    b = lax.broadcasted_iota(jnp.int32, (SW_STEPS, 2 * SW_STEPS), 1)
    dist_sub = a + SW_STEPS * (1 - v) - b
    val = _bias_lookup(rb_ref, _t5_bucket(dist_sub * dil), h)
    o_ref[0, 0, 0] = jnp.where((dist_sub >= 0) & (dist_sub <= SW_STEPS), val, NEG)


def _dil_bias_table(rel_bias):
    ng = len(SW_DILATIONS)
    return pl.pallas_call(
        _dil_bias_kernel,
        out_shape=jax.ShapeDtypeStruct((ng, N_HEADS, 2, SW_STEPS, 2 * SW_STEPS), F32),
        grid=(ng, N_HEADS, 2),
        in_specs=[pl.BlockSpec(memory_space=pltpu.SMEM)],
        out_specs=pl.BlockSpec((1, 1, 1, SW_STEPS, 2 * SW_STEPS), lambda g, h, v: (g, h, v, 0, 0)),
        compiler_params=_params("parallel", "parallel", "parallel"),
        name="dil_bias_table",
    )(rel_bias)


def _qkv_kernel(flag_ref, x_ref, g_ref, w_ref, cg_ref, *refs, tm, tn, chunk, dils, steps_per_group):
    o_refs = refs[:len(dils)]
    xn_ref, acc_ref = refs[len(dils):]
    j = pl.program_id(1)

    @pl.when(j == 0)
    def _():
        _rmsnorm_rows(x_ref, g_ref, xn_ref, 0, tm, chunk)

    acc = jnp.dot(xn_ref[...], w_ref[...].astype(BF16), preferred_element_type=F32)
    for c in range(tn // LANES):
        acc_ref[c] = acc[:, c * LANES:(c + 1) * LANES]
    normed = flag_ref[j] != 0
    per_tile = COL_TILE // LANES

    def write_group(o_ref, dil):
        n_rows = tm // dil
        for r in range(dil):
            rows = slice(None) if dil == 1 else pl.ds(r, n_rows, stride=dil)
            for c in range(tn // LANES):
                y = acc_ref[c, rows, :]
                rs = lax.rsqrt(jnp.mean(y * y, axis=-1, keepdims=True) + EPS)
                scale = jnp.where(normed, rs, 1.0)
                lo = r * COL_TILE + (c % per_tile) * LANES
                o_ref[c // per_tile, :, lo:lo + LANES] = (
                    y * scale * cg_ref[:, c * LANES:(c + 1) * LANES]).astype(BF16)

    if len(dils) == 1:
        write_group(o_refs[0], dils[0])
    else:
        for g, dil in enumerate(dils):
            pl.when(j // steps_per_group == g)(functools.partial(write_group, o_refs[g], dil))


def _qkv_proj(x2d, ln_g, w, col_gain, tile_flags, dils):
    t = _tiles()
    tm, tn = t["tm"], t["tn_qkv"]
    m, d = x2d.shape
    n = w.shape[2]
    ng = len(dils)
    steps_per_group = n // ng // tn
    group_tiles = n // ng // COL_TILE
    kern = functools.partial(_qkv_kernel, tm=tm, tn=tn, chunk=t["norm_chunk"], dils=dils,
                             steps_per_group=steps_per_group)

    def out_spec(g, dil):
        def index(i, j, f):
            return (jnp.clip(j - g * steps_per_group, 0, steps_per_group - 1), i, 0)
        return pl.BlockSpec((tn // COL_TILE, tm // dil, dil * COL_TILE), index)

    return pl.pallas_call(
        kern,
        out_shape=[jax.ShapeDtypeStruct((group_tiles, m // dil, dil * COL_TILE), BF16) for dil in dils],
        grid_spec=pltpu.PrefetchScalarGridSpec(
            num_scalar_prefetch=1,
            grid=(m // tm, n // tn),
            in_specs=[
                pl.BlockSpec((tm, d), lambda i, j, f: (i, 0)),
                pl.BlockSpec((1, d), lambda i, j, f: (0, 0)),
                pl.BlockSpec((None, d, tn), lambda i, j, f: (0, 0, j)),
                pl.BlockSpec((1, tn), lambda i, j, f: (0, j)),
            ],
            out_specs=[out_spec(g, dil) for g, dil in enumerate(dils)],
            scratch_shapes=[pltpu.VMEM((tm, d), BF16), pltpu.VMEM((tn // LANES, tm, LANES), F32)],
        ),
        compiler_params=_params("parallel", "arbitrary"),
        name="qkv_proj",
    )(tile_flags, x2d, ln_g.reshape(1, d), w, col_gain.reshape(1, n))


def _diff_attn_kernel(q_ref, k_ref, v_ref, bias_ref, lam_ref, sg_ref, o_ref, *, lam_init, seq):
    t = ATTN_TILE
    lam = lam_ref[...]
    lam_full = (jnp.exp(jnp.sum(lam[0:1] * lam[1:2], axis=-1, keepdims=True))
                - jnp.exp(jnp.sum(lam[2:3] * lam[3:4], axis=-1, keepdims=True)) + lam_init)
    gain = sg_ref[...] * (1.0 - lam_init)

    def logits(c, mp):
        width = (c + 1) * t
        lo = mp * HEAD_DIM
        bias = jnp.concatenate([bias_ref[0, c - ki] for ki in range(c + 1)], axis=1)
        return lax.dot_general(q_ref[0, c * t:(c + 1) * t, lo:lo + HEAD_DIM], k_ref[0, 0:width, lo:lo + HEAD_DIM],
                               (((1,), (1,)), ((), ())), preferred_element_type=F32) + bias

    def attend(c, s):
        p = jnp.exp(s - jnp.max(s, axis=-1, keepdims=True))
        den = jnp.sum(p, axis=-1, keepdims=True)
        return jnp.dot(p.astype(BF16), v_ref[0, 0:(c + 1) * t, :], preferred_element_type=F32) / den

    units = [(c, mp) for c in range(seq // t) for mp in range(2)]
    s_next = logits(*units[0])
    maps = {}
    for i, (c, mp) in enumerate(units):
        s_cur = s_next
        if i + 1 < len(units):
            s_next = logits(*units[i + 1])
        maps[mp] = attend(c, s_cur)
        if mp == 1:
            o = maps[0] - lam_full * maps[1]
            rs = lax.rsqrt(jnp.mean(o * o, axis=-1, keepdims=True) + EPS)
            o_ref[c * t:(c + 1) * t, :] = (o * rs * gain).astype(BF16)


def _diff_attention(qkv, bias_tab, lam, subln_g, batch, seq, lam_init):
    t = ATTN_TILE
    nq = seq // t
    m = batch * seq
    kern = functools.partial(_diff_attn_kernel, lam_init=lam_init, seq=seq)
    return pl.pallas_call(
        kern,
        out_shape=jax.ShapeDtypeStruct((m, N_HEADS * V_DIM), BF16),
        grid=(N_HEADS, batch),
        in_specs=[
            pl.BlockSpec((1, seq, COL_TILE), lambda h, b: (h, b, 0)),
            pl.BlockSpec((1, seq, COL_TILE), lambda h, b: (N_HEADS + h, b, 0)),
            pl.BlockSpec((1, seq, COL_TILE), lambda h, b: (2 * N_HEADS + h, b, 0)),
            pl.BlockSpec((1, nq, t, t), lambda h, b: (h, 0, 0, 0)),
            pl.BlockSpec((4, HEAD_DIM), lambda h, b: (0, 0)),
            pl.BlockSpec((1, V_DIM), lambda h, b: (0, 0)),
        ],
        out_specs=pl.BlockSpec((seq, V_DIM), lambda h, b: (b, h)),
        compiler_params=_params("parallel", "parallel"),
        name="diff_attention",
    )(qkv, qkv, qkv, bias_tab, lam, subln_g.reshape(1, V_DIM))


def _dil_attn_kernel(q_ref, k_ref, v_ref, bias_ref, o_ref, lse_ref, *scratch, sub_len, rows, dil):
    blk = SW_STEPS
    r = pl.program_id(1)
    c = pl.program_id(2)
    nblk = rows // blk
    single = sub_len == blk
    lane = lax.broadcasted_iota(jnp.int32, (blk, LANES), 1)

    def window(n):
        ng = c * nblk + n
        if single:
            return 0, 1, blk
        return pl.multiple_of(jnp.maximum(ng - 1, 0) * blk, blk), (ng == 0).astype(jnp.int32), 2 * blk

    def logits(n, h):
        start, variant, width = window(n)
        tile, lo = h // 2, (h % 2) * HEAD_DIM
        qt = q_ref[tile, n * blk:(n + 1) * blk, lo:lo + HEAD_DIM]
        kw = k_ref[tile, pl.ds(start, width), lo:lo + HEAD_DIM]
        bias = bias_ref[0, h, variant][:, :width]
        return lax.dot_general(qt, kw, (((1,), (1,)), ((), ())), preferred_element_type=F32) + bias

    def attend(n, h, s):
        start, _, width = window(n)
        mx = jnp.max(s, axis=-1, keepdims=True)
        p = jnp.exp(s - mx)
        den = jnp.sum(p, axis=-1, keepdims=True)
        o = jnp.dot(p.astype(BF16), v_ref[h, pl.ds(start, width), :], preferred_element_type=F32) / den
        return o, mx + jnp.log(den)

    units = [(n, h) for n in range(nblk) for h in range(N_HEADS)]
    ahead = 2 if nblk == 1 else 1
    queue = [logits(*u) for u in units[:ahead]]
    lse_tile = None
    for i, (n, h) in enumerate(units):
        s = queue.pop(0)
        if i + ahead < len(units):
            queue.append(logits(*units[i + ahead]))
        o, lse = attend(n, h, s)
        if dil == 1:
            out_rows = slice(n * blk, (n + 1) * blk)
            o_ref[out_rows, h * V_DIM:(h + 1) * V_DIM] = o.astype(BF16)
        else:
            out_rows = pl.ds(r + (c * nblk + n) * (blk * dil), blk, stride=dil)
            for part in range(V_DIM // LANES):
                scratch[0][h * (V_DIM // LANES) + part, out_rows, :] = o[:, part * LANES:(part + 1) * LANES]
        lse_tile = jnp.where(lane == h, lse, jnp.zeros((blk, LANES), F32) if h == 0 else lse_tile)
        if h == N_HEADS - 1:
            (lse_ref if dil == 1 else scratch[1])[out_rows, :] = lse_tile

    if dil > 1:
        @pl.when((r == dil - 1) & (c == pl.num_programs(2) - 1))
        def _():
            for slab in range(scratch[0].shape[0]):
                o_ref[:, slab * LANES:(slab + 1) * LANES] = scratch[0][slab].astype(BF16)
            lse_ref[...] = scratch[1][...]


def _dilated_group_attention(qkv, bias_tab, g, batch, seq):
    dil = SW_DILATIONS[g]
    sub_len = seq // dil
    rows = min(sub_len, 512)
    nc = sub_len // rows
    m = batch * seq
    width = N_HEADS * V_DIM
    kern = functools.partial(_dil_attn_kernel, sub_len=sub_len, rows=rows, dil=dil)
    if dil == 1:
        out_specs = (pl.BlockSpec((rows, width), lambda b, r, c: (b * nc + c, 0)),
                     pl.BlockSpec((rows, LANES), lambda b, r, c: (b * nc + c, 0)))
        scratch, sem = [], ("parallel", "parallel", "parallel")
    else:
        out_specs = (pl.BlockSpec((seq, width), lambda b, r, c: (b, 0)),
                     pl.BlockSpec((seq, LANES), lambda b, r, c: (b, 0)))
        scratch = [pltpu.VMEM((width // LANES, seq, LANES), F32), pltpu.VMEM((seq, LANES), F32)]
        sem = ("parallel", "arbitrary", "arbitrary")
    return pl.pallas_call(
        kern,
        out_shape=(jax.ShapeDtypeStruct((m, width), BF16), jax.ShapeDtypeStruct((m, LANES), F32)),
        grid=(batch, dil, nc),
        in_specs=[
            pl.BlockSpec((4, rows, COL_TILE), lambda b, r, c: (0, b * nc + c, r)),
            pl.BlockSpec((4, sub_len, COL_TILE), lambda b, r, c: (1, b, r)),
            pl.BlockSpec((8, sub_len, COL_TILE), lambda b, r, c: (1, b, r)),
            pl.BlockSpec((1, N_HEADS, 2, SW_STEPS, 2 * SW_STEPS), lambda b, r, c: (g, 0, 0, 0, 0)),
        ],
        out_specs=out_specs,
        scratch_shapes=scratch,
        compiler_params=_params(*sem),
        name=f"dilated_attention_g{g}",
    )(qkv, qkv, qkv, bias_tab)


def _proj_res_kernel(a_ref, w_ref, r_ref, o_ref):
    o_ref[...] = r_ref[...] + jnp.dot(a_ref[...], w_ref[...].astype(BF16), preferred_element_type=F32)


def _proj_residual(a, w, layer, res, tn):
    tm = _tiles()["tm"]
    m, k = a.shape
    n = w.shape[2]
    return pl.pallas_call(
        _proj_res_kernel,
        out_shape=jax.ShapeDtypeStruct((m, n), F32),
        grid=(m // tm, n // tn),
        in_specs=[
            pl.BlockSpec((tm, k), lambda i, j: (i, 0)),
            pl.BlockSpec((None, k, tn), lambda i, j: (layer, 0, j)),
            pl.BlockSpec((tm, tn), lambda i, j: (i, j)),
        ],
        out_specs=pl.BlockSpec((tm, tn), lambda i, j: (i, j)),
        compiler_params=_params("parallel", "arbitrary"),
        name="proj_residual",
    )(a, w, res)


def _combine_proj_kernel(o0_ref, o1_ref, o2_ref, l0_ref, l1_ref, l2_ref, w_ref, r_ref, out_ref, a_ref,
                         *, tm, chunk):
    j = pl.program_id(1)

    @pl.when(j == 0)
    def _():
        for r in range(0, tm, chunk):
            ls = [l_ref[r:r + chunk, :] for l_ref in (l0_ref, l1_ref, l2_ref)]
            mx = jnp.maximum(jnp.maximum(ls[0], ls[1]), ls[2])
            es = [jnp.exp(l - mx) for l in ls]
            den = es[0] + es[1] + es[2]
            alphas = [e / den for e in es]
            for h in range(N_HEADS):
                cols = slice(h * V_DIM, (h + 1) * V_DIM)
                acc = None
                for al, o_ref in zip(alphas, (o0_ref, o1_ref, o2_ref)):
                    term = al[:, h:h + 1] * o_ref[r:r + chunk, cols].astype(F32)
                    acc = term if acc is None else acc + term
                a_ref[r:r + chunk, cols] = acc.astype(BF16)

    out_ref[...] = r_ref[...] + jnp.dot(a_ref[...], w_ref[...].astype(BF16), preferred_element_type=F32)


def _combine_proj_residual(outs, lses, w, res, tn):
    t = _tiles()
    tm = t["tm"]
    m, k = outs[0].shape
    n = w.shape[2]
    kern = functools.partial(_combine_proj_kernel, tm=tm, chunk=t["norm_chunk"])
    o_spec = pl.BlockSpec((tm, k), lambda i, j: (i, 0))
    l_spec = pl.BlockSpec((tm, LANES), lambda i, j: (i, 0))
    return pl.pallas_call(
        kern,
        out_shape=jax.ShapeDtypeStruct((m, n), F32),
        grid=(m // tm, n // tn),
        in_specs=[o_spec, o_spec, o_spec, l_spec, l_spec, l_spec,
                  pl.BlockSpec((None, k, tn), lambda i, j: (0, 0, j)),
                  pl.BlockSpec((tm, tn), lambda i, j: (i, j))],
        out_specs=pl.BlockSpec((tm, tn), lambda i, j: (i, j)),
        scratch_shapes=[pltpu.VMEM((tm, k), BF16)],
        compiler_params=_params("parallel", "arbitrary"),
        name="combine_proj_residual",
    )(*outs, *lses, w, res)


def _ffn_up_kernel(x_ref, halo_ref, g_ref, wg_ref, wv_ref, cwg_ref, cwv_ref, cbg_ref, cbv_ref, o_ref, xn_ref,
                   *, tm, tiles_per_seq, chunk):
    i = pl.program_id(0)
    j = pl.program_id(1)
    halo = CONV_HALO

    @pl.when(j == 0)
    def _():
        _rmsnorm_rows(x_ref, g_ref, xn_ref, halo, tm, chunk)
        first = (i % tiles_per_seq) == 0

        @pl.when(first)
        def _():
            xn_ref[0:halo, :] = jnp.zeros((halo, xn_ref.shape[1]), BF16)

        @pl.when(jnp.logical_not(first))
        def _():
            _rmsnorm_rows(halo_ref, g_ref, xn_ref, 0, halo, halo)

    xn = xn_ref[...]

    def conv_branch(w_ref, cw_ref, cb_ref):
        u = jnp.dot(xn, w_ref[...].astype(BF16), preferred_element_type=F32)
        cw = cw_ref[...]
        return (cw[0:1] * u[halo - 2:halo - 2 + tm] + cw[1:2] * u[halo - 1:halo - 1 + tm]
                + cw[2:3] * u[halo:halo + tm]) + cb_ref[...]

    gate = conv_branch(wg_ref, cwg_ref, cbg_ref)
    val = conv_branch(wv_ref, cwv_ref, cbv_ref)
    o_ref[...] = (gate / (1.0 + jnp.exp(-gate)) * val).astype(BF16)


def _ffn_up(h2d, ln_g, w_up, layer, conv_w, conv_b, seq):
    t = _tiles()
    tm, tn = t["tm"], t["tn_up"]
    m, d = h2d.shape
    dff = w_up.shape[2] // 2
    nj = dff // tn
    halo_blocks = tm // CONV_HALO
    kern = functools.partial(_ffn_up_kernel, tm=tm, tiles_per_seq=seq // tm, chunk=t["norm_chunk"])
    return pl.pallas_call(
        kern,
        out_shape=jax.ShapeDtypeStruct((m, dff), BF16),
        grid=(m // tm, nj),
        in_specs=[
            pl.BlockSpec((tm, d), lambda i, j: (i, 0)),
            pl.BlockSpec((CONV_HALO, d), lambda i, j: (jnp.maximum(i * halo_blocks - 1, 0), 0)),
            pl.BlockSpec((1, d), lambda i, j: (0, 0)),
            pl.BlockSpec((None, d, tn), lambda i, j: (layer, 0, j)),
            pl.BlockSpec((None, d, tn), lambda i, j: (layer, 0, j + nj)),
            pl.BlockSpec((3, tn), lambda i, j: (0, j)),
            pl.BlockSpec((3, tn), lambda i, j: (0, j + nj)),
            pl.BlockSpec((1, tn), lambda i, j: (0, j)),
            pl.BlockSpec((1, tn), lambda i, j: (0, j + nj)),
        ],
        out_specs=pl.BlockSpec((tm, tn), lambda i, j: (i, j)),
        scratch_shapes=[pltpu.VMEM((tm + CONV_HALO, d), BF16)],
        compiler_params=_params("parallel", "arbitrary"),
        name="ffn_up",
    )(h2d, h2d, ln_g.reshape(1, d), w_up, w_up, conv_w, conv_w, conv_b.reshape(1, -1), conv_b.reshape(1, -1))


def _conv_ffn(h2d, ln_g, w_up, conv_w, conv_b, w_down, layer, seq):
    act = _ffn_up(h2d, ln_g, w_up, layer, conv_w, conv_b, seq)
    return _proj_residual(act, w_down, layer, h2d, _tiles()["tn_down"])


def _diff_col_gain(q_gain, k_gain):
    scale = HEAD_DIM ** -0.5
    nqk = 2 * N_HEADS
    gain = jnp.concatenate([jnp.tile(q_gain * scale, nqk), jnp.tile(k_gain, nqk),
                            jnp.ones((N_HEADS * V_DIM,), F32)])
    return gain


def _dil_col_gain(q_gain, k_gain):
    scale = HEAD_DIM ** -0.5
    parts = []
    for g in range(len(SW_DILATIONS)):
        parts += [jnp.tile(q_gain[g] * scale, N_HEADS), jnp.tile(k_gain[g], N_HEADS),
                  jnp.ones((N_HEADS * V_DIM,), F32)]
    return jnp.concatenate(parts)


def _tile_flags(n_cols, tn, normed_ranges):
    flags = []
    for j in range(n_cols // tn):
        lo = j * tn
        flags.append(int(any(a <= lo < b for a, b in normed_ranges)))
    return jnp.asarray(flags, jnp.int32)


def kernel(x, rel_bias, ln_mix, ln_ffn, a_w_qkv, a_q_norm, a_k_norm, a_lambda, a_subln, a_w_o,
           b_w_qkv, b_q_norm, b_k_norm, b_w_o, f_w_up, f_conv_w, f_conv_b, f_w_down):
    batch, seq, d = x.shape
    m = batch * seq
    t = _tiles()
    h = x.reshape(m, d)

    nqk = 2 * N_HEADS * HEAD_DIM
    flags0 = _tile_flags(a_w_qkv.shape[2], t["tn_qkv"], [(0, 2 * nqk)])
    (qkv0,) = _qkv_proj(h, ln_mix[0], a_w_qkv, _diff_col_gain(a_q_norm[0], a_k_norm[0]), flags0, (1,))
    lam_init = 0.8 - 0.6 * math.exp(-0.3 * 0)
    attn0 = _diff_attention(qkv0, _diff_bias_table(rel_bias, seq), a_lambda[0], a_subln[0], batch, seq, lam_init)
    h = _proj_residual(attn0, a_w_o, 0, h, t["tn_proj"])
    h = _conv_ffn(h, ln_ffn[0], f_w_up, f_conv_w[0], f_conv_b[0], f_w_down, 0, seq)

    gcols = 2 * N_HEADS * HEAD_DIM + N_HEADS * V_DIM
    nq1 = N_HEADS * HEAD_DIM
    flags1 = _tile_flags(b_w_qkv.shape[2], t["tn_qkv"],
                         [(g * gcols, g * gcols + 2 * nq1) for g in range(len(SW_DILATIONS))])
    qkv1 = _qkv_proj(h, ln_mix[1], b_w_qkv, _dil_col_gain(b_q_norm[0], b_k_norm[0]), flags1, SW_DILATIONS)
    dil_bias = _dil_bias_table(rel_bias)
    outs, lses = [], []
    for g in range(len(SW_DILATIONS)):
        o, lse = _dilated_group_attention(qkv1[g], dil_bias, g, batch, seq)
        outs.append(o)
        lses.append(lse)
    h = _combine_proj_residual(outs, lses, b_w_o, h, t["tn_proj"])
    h = _conv_ffn(h, ln_ffn[1], f_w_up, f_conv_w[1], f_conv_b[1], f_w_down, 1, seq)
    return h.reshape(batch, seq, d)
```

```python
import functools
import math

import jax
import jax.numpy as jnp
from jax import lax
from jax.experimental import pallas as pl
from jax.experimental.pallas import tpu as pltpu

F32 = jnp.float32
BF16 = jnp.bfloat16

N_HEADS = 8
HEAD_DIM = 128
V_DIM = 2 * HEAD_DIM
SW_DILATIONS = (1, 4, 16)
SW_STEPS = 128
NUM_BUCKETS = 32
MAX_DISTANCE = 2048
EPS = 1e-6
NEG = -1e30

LANES = 128
COL_TILE = 256
ATTN_TILE = 256
CONV_HALO = 16
VMEM_LIMIT = 60 * 1024 * 1024


def _tiles():
    return dict(tm=1024, tn_qkv=1024, tn_up=512, tn_down=512, norm_chunk=256,
                tm_attn_out=512, attn_out_chunk=256)


def _params(*sem):
    return pltpu.CompilerParams(dimension_semantics=sem, vmem_limit_bytes=VMEM_LIMIT)


def _rmsnorm_rows(x_ref, g_ref, out_ref, row0, rows, chunk):
    g = g_ref[...]
    for r in range(0, rows, chunk):
        x = x_ref[r:r + chunk, :]
        rs = lax.rsqrt(jnp.mean(x * x, axis=-1, keepdims=True) + EPS)
        out_ref[row0 + r:row0 + r + chunk, :] = (x * rs * g).astype(BF16)


def _t5_bucket(dist):
    max_exact = NUM_BUCKETS // 2
    n = jnp.maximum(dist, 0)
    nf = jnp.maximum(n, 1).astype(F32)
    large = max_exact + (jnp.log(nf / max_exact) / math.log(MAX_DISTANCE / max_exact)
                         * (NUM_BUCKETS - max_exact)).astype(jnp.int32)
    large = jnp.minimum(large, NUM_BUCKETS - 1)
    return jnp.where(n < max_exact, n, large)


def _bucket_range(dmin, dmax):
    def bucket(n):
        n = max(n, 0)
        if n < NUM_BUCKETS // 2:
            return n
        half = NUM_BUCKETS // 2
        return min(half + int(math.log(n / half) / math.log(MAX_DISTANCE / half) * half), NUM_BUCKETS - 1)
    return max(bucket(dmin) - 1, 0), min(bucket(dmax) + 1, NUM_BUCKETS - 1)


def _bias_lookup(rb_ref, bucket, h, bmin, bmax):
    val = jnp.full(bucket.shape, rb_ref[bmin, h], F32)
    for b in range(bmin + 1, bmax + 1):
        val = jnp.where(bucket == b, rb_ref[b, h], val)
    return val


def _diff_bias_kernel(rb_ref, o_ref):
    h = pl.program_id(0)
    t = ATTN_TILE
    delta = (lax.broadcasted_iota(jnp.int32, (t, t), 0) - lax.broadcasted_iota(jnp.int32, (t, t), 1))
    for d in range(o_ref.shape[1]):
        dist = d * t + delta
        val = _bias_lookup(rb_ref, _t5_bucket(dist), h, *_bucket_range(d * t - (t - 1), d * t + (t - 1)))
        o_ref[0, d] = jnp.where(dist >= 0, val, NEG) if d == 0 else val


def _diff_bias_table(rel_bias, seq):
    nd = seq // ATTN_TILE
    return pl.pallas_call(
        _diff_bias_kernel,
        out_shape=jax.ShapeDtypeStruct((N_HEADS, nd, ATTN_TILE, ATTN_TILE), F32),
        grid=(N_HEADS,),
        in_specs=[pl.BlockSpec(memory_space=pltpu.SMEM)],
        out_specs=pl.BlockSpec((1, nd, ATTN_TILE, ATTN_TILE), lambda h: (h, 0, 0, 0)),
        compiler_params=_params("parallel"),
        name="diff_bias_table",
    )(rel_bias)


def _dil_bias_kernel(rb_ref, o_ref):
    h = pl.program_id(0)
    v = pl.program_id(1)
    a = lax.broadcasted_iota(jnp.int32, (SW_STEPS, 2 * SW_STEPS), 0)
    b = lax.broadcasted_iota(jnp.int32, (SW_STEPS, 2 * SW_STEPS), 1)
    dist_sub = a + SW_STEPS * (1 - v) - b
    in_window = (dist_sub >= 0) & (dist_sub <= SW_STEPS)
    for g, dil in enumerate(SW_DILATIONS):
        dist = jnp.clip(dist_sub, 0, SW_STEPS) * dil
        val = _bias_lookup(rb_ref, _t5_bucket(dist), h, *_bucket_range(0, SW_STEPS * dil))
        o_ref[g, 0, 0] = jnp.where(in_window, val, NEG)


def _dil_bias_table(rel_bias):
    ng = len(SW_DILATIONS)
    return pl.pallas_call(
        _dil_bias_kernel,
        out_shape=jax.ShapeDtypeStruct((ng, N_HEADS, 2, SW_STEPS, 2 * SW_STEPS), F32),
        grid=(N_HEADS, 2),
        in_specs=[pl.BlockSpec(memory_space=pltpu.SMEM)],
        out_specs=pl.BlockSpec((ng, 1, 1, SW_STEPS, 2 * SW_STEPS), lambda h, v: (0, h, v, 0, 0)),
        compiler_params=_params("parallel", "parallel"),
        name="dil_bias_table",
    )(rel_bias)


def _qkv_kernel(flag_ref, x_ref, g_ref, w_ref, cg_ref, *refs, tm, tn, chunk, dils, steps_per_group):
    o_refs = refs[:len(dils)]
    xn_ref, acc_ref = refs[len(dils):]
    j = pl.program_id(1)

    @pl.when(j == 0)
    def _():
        _rmsnorm_rows(x_ref, g_ref, xn_ref, 0, tm, chunk)

    acc = jnp.dot(xn_ref[...], w_ref[...].astype(BF16), preferred_element_type=F32)
    for c in range(tn // LANES):
        acc_ref[c] = acc[:, c * LANES:(c + 1) * LANES]
    normed = flag_ref[j] != 0
    per_tile = COL_TILE // LANES

    def write_group(o_ref, dil):
        n_rows = tm // dil
        for r in range(dil):
            rows = slice(None) if dil == 1 else pl.ds(r, n_rows, stride=dil)
            for c in range(tn // LANES):
                y = acc_ref[c, rows, :]
                rs = lax.rsqrt(jnp.mean(y * y, axis=-1, keepdims=True) + EPS)
                scale = jnp.where(normed, rs, 1.0)
                lo = r * COL_TILE + (c % per_tile) * LANES
                o_ref[c // per_tile, :, lo:lo + LANES] = (
                    y * scale * cg_ref[:, c * LANES:(c + 1) * LANES]).astype(BF16)

    if len(dils) == 1:
        write_group(o_refs[0], dils[0])
    else:
        for g, dil in enumerate(dils):
            pl.when(j // steps_per_group == g)(functools.partial(write_group, o_refs[g], dil))


def _qkv_proj(x2d, ln_g, w, col_gain, tile_flags, dils):
    t = _tiles()
    tm, tn = t["tm"], t["tn_qkv"]
    m, d = x2d.shape
    n = w.shape[2]
    ng = len(dils)
    steps_per_group = n // ng // tn
    group_tiles = n // ng // COL_TILE
    kern = functools.partial(_qkv_kernel, tm=tm, tn=tn, chunk=t["norm_chunk"], dils=dils,
                             steps_per_group=steps_per_group)

    def out_spec(g, dil):
        def index(i, j, f):
            return (jnp.clip(j - g * steps_per_group, 0, steps_per_group - 1), i, 0)
        return pl.BlockSpec((tn // COL_TILE, tm // dil, dil * COL_TILE), index)

    return pl.pallas_call(
        kern,
        out_shape=[jax.ShapeDtypeStruct((group_tiles, m // dil, dil * COL_TILE), BF16) for dil in dils],
        grid_spec=pltpu.PrefetchScalarGridSpec(
            num_scalar_prefetch=1,
            grid=(m // tm, n // tn),
            in_specs=[
                pl.BlockSpec((tm, d), lambda i, j, f: (i, 0)),
                pl.BlockSpec((1, d), lambda i, j, f: (0, 0)),
                pl.BlockSpec((None, d, tn), lambda i, j, f: (0, 0, j)),
                pl.BlockSpec((1, tn), lambda i, j, f: (0, j)),
            ],
            out_specs=[out_spec(g, dil) for g, dil in enumerate(dils)],
            scratch_shapes=[pltpu.VMEM((tm, d), BF16), pltpu.VMEM((tn // LANES, tm, LANES), F32)],
        ),
        compiler_params=_params("parallel", "arbitrary"),
        name="qkv_proj",
    )(tile_flags, x2d, ln_g.reshape(1, d), w, col_gain.reshape(1, n))


def _diff_attn_kernel(q_ref, k_ref, v_ref, bias_ref, lam_ref, sg_ref, o_ref, *, lam_init, seq):
    t = ATTN_TILE
    lam = lam_ref[...]
    lam_full = (jnp.exp(jnp.sum(lam[0:1] * lam[1:2], axis=-1, keepdims=True))
                - jnp.exp(jnp.sum(lam[2:3] * lam[3:4], axis=-1, keepdims=True)) + lam_init)
    gain = sg_ref[...] * (1.0 - lam_init)

    def logits(c, mp):
        width = (c + 1) * t
        lo = mp * HEAD_DIM
        bias = jnp.concatenate([bias_ref[0, c - ki] for ki in range(c + 1)], axis=1)
        return lax.dot_general(q_ref[0, c * t:(c + 1) * t, lo:lo + HEAD_DIM], k_ref[0, 0:width, lo:lo + HEAD_DIM],
                               (((1,), (1,)), ((), ())), preferred_element_type=F32) + bias

    def attend(c, s):
        p = jnp.exp(s - jnp.max(s, axis=-1, keepdims=True))
        den = jnp.sum(p, axis=-1, keepdims=True)
        return jnp.dot(p.astype(BF16), v_ref[0, 0:(c + 1) * t, :], preferred_element_type=F32) / den

    units = [(c, mp) for c in range(seq // t) for mp in range(2)]
    s_next = logits(*units[0])
    maps = {}
    for i, (c, mp) in enumerate(units):
        s_cur = s_next
        if i + 1 < len(units):
            s_next = logits(*units[i + 1])
        maps[mp] = attend(c, s_cur)
        if mp == 1:
            o = maps[0] - lam_full * maps[1]
            rs = lax.rsqrt(jnp.mean(o * o, axis=-1, keepdims=True) + EPS)
            o_ref[c * t:(c + 1) * t, :] = (o * rs * gain).astype(BF16)


def _diff_attention(qkv, bias_tab, lam, subln_g, batch, seq, lam_init):
    t = ATTN_TILE
    nq = seq // t
    m = batch * seq
    kern = functools.partial(_diff_attn_kernel, lam_init=lam_init, seq=seq)
    return pl.pallas_call(
        kern,
        out_shape=jax.ShapeDtypeStruct((m, N_HEADS * V_DIM), BF16),
        grid=(N_HEADS, batch),
        in_specs=[
            pl.BlockSpec((1, seq, COL_TILE), lambda h, b: (h, b, 0)),
            pl.BlockSpec((1, seq, COL_TILE), lambda h, b: (N_HEADS + h, b, 0)),
            pl.BlockSpec((1, seq, COL_TILE), lambda h, b: (2 * N_HEADS + h, b, 0)),
            pl.BlockSpec((1, nq, t, t), lambda h, b: (h, 0, 0, 0)),
            pl.BlockSpec((4, HEAD_DIM), lambda h, b: (0, 0)),
            pl.BlockSpec((1, V_DIM), lambda h, b: (0, 0)),
        ],
        out_specs=pl.BlockSpec((seq, V_DIM), lambda h, b: (b, h)),
        compiler_params=_params("parallel", "parallel"),
        name="diff_attention",
    )(qkv, qkv, qkv, bias_tab, lam, subln_g.reshape(1, V_DIM))


def _dil_attn_kernel(q_ref, k_ref, v_ref, bias_ref, o_ref, lse_ref, *scratch, sub_len, rows, dil):
    blk = SW_STEPS
    r = pl.program_id(1)
    c = pl.program_id(2)
    nblk = rows // blk
    single = sub_len == blk
    lane = lax.broadcasted_iota(jnp.int32, (blk, LANES), 1)

    def window(n):
        ng = c * nblk + n
        if single:
            return 0, 1, blk
        return pl.multiple_of(jnp.maximum(ng - 1, 0) * blk, blk), (ng == 0).astype(jnp.int32), 2 * blk

    def logits(n, h):
        start, variant, width = window(n)
        tile, lo = h // 2, (h % 2) * HEAD_DIM
        qt = q_ref[tile, n * blk:(n + 1) * blk, lo:lo + HEAD_DIM]
        kw = k_ref[tile, pl.ds(start, width), lo:lo + HEAD_DIM]
        bias = bias_ref[0, h, variant][:, :width]
        return lax.dot_general(qt, kw, (((1,), (1,)), ((), ())), preferred_element_type=F32) + bias

    def attend(n, h, s):
        start, _, width = window(n)
        mx = jnp.max(s, axis=-1, keepdims=True)
        p = jnp.exp(s - mx)
        den = jnp.sum(p, axis=-1, keepdims=True)
        o = jnp.dot(p.astype(BF16), v_ref[h, pl.ds(start, width), :], preferred_element_type=F32) / den
        return o, mx + jnp.log(den)

    units = [(n, h) for n in range(nblk) for h in range(N_HEADS)]
    ahead = 2 if nblk == 1 else 1
    queue = [logits(*u) for u in units[:ahead]]
    lse_tile = None
    for i, (n, h) in enumerate(units):
        s = queue.pop(0)
        if i + ahead < len(units):
            queue.append(logits(*units[i + ahead]))
        o, lse = attend(n, h, s)
        if dil == 1:
            out_rows = slice(n * blk, (n + 1) * blk)
            o_ref[out_rows, h * V_DIM:(h + 1) * V_DIM] = o.astype(BF16)
        else:
            out_rows = pl.ds(r + (c * nblk + n) * (blk * dil), blk, stride=dil)
            for part in range(V_DIM // LANES):
                scratch[0][h * (V_DIM // LANES) + part, out_rows, :] = o[:, part * LANES:(part + 1) * LANES]
        lse_tile = jnp.where(lane == h, lse, jnp.zeros((blk, LANES), F32) if h == 0 else lse_tile)
        if h == N_HEADS - 1:
            (lse_ref if dil == 1 else scratch[1])[out_rows, :] = lse_tile

    if dil > 1:
        @pl.when((r == dil - 1) & (c == pl.num_programs(2) - 1))
        def _():
            for slab in range(scratch[0].shape[0]):
                o_ref[:, slab * LANES:(slab + 1) * LANES] = scratch[0][slab].astype(BF16)
            lse_ref[...] = scratch[1][...]


def _dilated_group_attention(qkv, bias_tab, g, batch, seq):
    dil = SW_DILATIONS[g]
    sub_len = seq // dil
    rows = min(sub_len, 512)
    nc = sub_len // rows
    m = batch * seq
    width = N_HEADS * V_DIM
    kern = functools.partial(_dil_attn_kernel, sub_len=sub_len, rows=rows, dil=dil)
    if dil == 1:
        out_specs = (pl.BlockSpec((rows, width), lambda b, r, c: (b * nc + c, 0)),
                     pl.BlockSpec((rows, LANES), lambda b, r, c: (b * nc + c, 0)))
        scratch, sem = [], ("parallel", "parallel", "parallel")
    else:
        out_specs = (pl.BlockSpec((seq, width), lambda b, r, c: (b, 0)),
                     pl.BlockSpec((seq, LANES), lambda b, r, c: (b, 0)))
        scratch = [pltpu.VMEM((width // LANES, seq, LANES), F32), pltpu.VMEM((seq, LANES), F32)]
        sem = ("parallel", "arbitrary", "arbitrary")
    return pl.pallas_call(
        kern,
        out_shape=(jax.ShapeDtypeStruct((m, width), BF16), jax.ShapeDtypeStruct((m, LANES), F32)),
        grid=(batch, dil, nc),
        in_specs=[
            pl.BlockSpec((4, rows, COL_TILE), lambda b, r, c: (0, b * nc + c, r)),
            pl.BlockSpec((4, sub_len, COL_TILE), lambda b, r, c: (1, b, r)),
            pl.BlockSpec((8, sub_len, COL_TILE), lambda b, r, c: (1, b, r)),
            pl.BlockSpec((1, N_HEADS, 2, SW_STEPS, 2 * SW_STEPS), lambda b, r, c: (g, 0, 0, 0, 0)),
        ],
        out_specs=out_specs,
        scratch_shapes=scratch,
        compiler_params=_params(*sem),
        name=f"dilated_attention_g{g}",
    )(qkv, qkv, qkv, bias_tab)


def _proj_res_kernel(a_ref, w_ref, r_ref, o_ref):
    o_ref[...] = r_ref[...] + jnp.dot(a_ref[...], w_ref[...].astype(BF16), preferred_element_type=F32)


def _proj_residual(a, w, layer, res, tn):
    tm = _tiles()["tm"]
    m, k = a.shape
    n = w.shape[2]
    return pl.pallas_call(
        _proj_res_kernel,
        out_shape=jax.ShapeDtypeStruct((m, n), F32),
        grid=(m // tm, n // tn),
        in_specs=[
            pl.BlockSpec((tm, k), lambda i, j: (i, 0)),
            pl.BlockSpec((None, k, tn), lambda i, j: (layer, 0, j)),
            pl.BlockSpec((tm, tn), lambda i, j: (i, j)),
        ],
        out_specs=pl.BlockSpec((tm, tn), lambda i, j: (i, j)),
        compiler_params=_params("parallel", "arbitrary"),
        name="proj_residual",
    )(a, w, res)


def _attn_out_kernel(*refs, n_groups, tm, chunk):
    o_refs = refs[:n_groups]
    l_refs = refs[n_groups:2 * n_groups] if n_groups > 1 else ()
    w_ref, r_ref, out_ref, wb_ref = refs[-4:]

    @pl.when(pl.program_id(0) == 0)
    def _():
        wb_ref[...] = w_ref[...].astype(BF16)

    def lhs(r):
        if n_groups == 1:
            return o_refs[0][r:r + chunk, :]
        ls = [l_ref[r:r + chunk, :] for l_ref in l_refs]
        mx = functools.reduce(jnp.maximum, ls)
        es = [jnp.exp(l - mx) for l in ls]
        den = functools.reduce(jnp.add, es)
        alphas = [e / den for e in es]
        heads = []
        for h in range(N_HEADS):
            cols = slice(h * V_DIM, (h + 1) * V_DIM)
            terms = [al[:, h:h + 1] * o_ref[r:r + chunk, cols].astype(F32) for al, o_ref in zip(alphas, o_refs)]
            heads.append(functools.reduce(jnp.add, terms).astype(BF16))
        return jnp.concatenate(heads, axis=1)

    starts = list(range(0, tm, chunk))
    a_next = lhs(starts[0])
    for i, r in enumerate(starts):
        a_cur = a_next
        if i + 1 < len(starts):
            a_next = lhs(starts[i + 1])
        out_ref[r:r + chunk, :] = r_ref[r:r + chunk, :] + jnp.dot(a_cur, wb_ref[...], preferred_element_type=F32)


def _attn_out_proj(outs, lses, w, res):
    t = _tiles()
    tm, chunk = t["tm_attn_out"], t["attn_out_chunk"]
    m, k = outs[0].shape
    n = w.shape[2]
    kern = functools.partial(_attn_out_kernel, n_groups=len(outs), tm=tm, chunk=chunk)
    o_spec = pl.BlockSpec((tm, k), lambda i: (i, 0))
    l_spec = pl.BlockSpec((tm, LANES), lambda i: (i, 0))
    return pl.pallas_call(
        kern,
        out_shape=jax.ShapeDtypeStruct((m, n), F32),
        grid=(m // tm,),
        in_specs=[o_spec] * len(outs) + [l_spec] * len(lses) + [
            pl.BlockSpec((None, k, n), lambda i: (0, 0, 0), pipeline_mode=pl.Buffered(1)),
            pl.BlockSpec((tm, n), lambda i: (i, 0))],
        out_specs=pl.BlockSpec((tm, n), lambda i: (i, 0)),
        scratch_shapes=[pltpu.VMEM((k, n), BF16)],
        compiler_params=_params("arbitrary"),
        name="attn_out_proj",
    )(*outs, *lses, w, res)


def _ffn_up_kernel(x_ref, halo_ref, g_ref, wg_ref, wv_ref, cwg_ref, cwv_ref, cbg_ref, cbv_ref, o_ref, xn_ref,
                   *, tm, tiles_per_seq, chunk):
    i = pl.program_id(0)
    j = pl.program_id(1)
    halo = CONV_HALO

    @pl.when(j == 0)
    def _():
        _rmsnorm_rows(x_ref, g_ref, xn_ref, halo, tm, chunk)
        first = (i % tiles_per_seq) == 0

        @pl.when(first)
        def _():
            xn_ref[0:halo, :] = jnp.zeros((halo, xn_ref.shape[1]), BF16)

        @pl.when(jnp.logical_not(first))
        def _():
            _rmsnorm_rows(halo_ref, g_ref, xn_ref, 0, halo, halo)

    xn = xn_ref[...]

    def conv_branch(w_ref, cw_ref, cb_ref):
        u = jnp.dot(xn, w_ref[...].astype(BF16), preferred_element_type=F32)
        cw = cw_ref[...]
        return (cw[0:1] * u[halo - 2:halo - 2 + tm] + cw[1:2] * u[halo - 1:halo - 1 + tm]
                + cw[2:3] * u[halo:halo + tm]) + cb_ref[...]

    gate = conv_branch(wg_ref, cwg_ref, cbg_ref)
    val = conv_branch(wv_ref, cwv_ref, cbv_ref)
    o_ref[...] = (gate / (1.0 + jnp.exp(-gate)) * val).astype(BF16)


def _ffn_up(h2d, ln_g, w_up, layer, conv_w, conv_b, seq):
    t = _tiles()
    tm, tn = t["tm"], t["tn_up"]
    m, d = h2d.shape
    dff = w_up.shape[2] // 2
    nj = dff // tn
    halo_blocks = tm // CONV_HALO
    kern = functools.partial(_ffn_up_kernel, tm=tm, tiles_per_seq=seq // tm, chunk=t["norm_chunk"])
    return pl.pallas_call(
        kern,
        out_shape=jax.ShapeDtypeStruct((m, dff), BF16),
        grid=(m // tm, nj),
        in_specs=[
            pl.BlockSpec((tm, d), lambda i, j: (i, 0)),
            pl.BlockSpec((CONV_HALO, d), lambda i, j: (jnp.maximum(i * halo_blocks - 1, 0), 0)),
            pl.BlockSpec((1, d), lambda i, j: (0, 0)),
            pl.BlockSpec((None, d, tn), lambda i, j: (layer, 0, j)),
            pl.BlockSpec((None, d, tn), lambda i, j: (layer, 0, j + nj)),
            pl.BlockSpec((3, tn), lambda i, j: (0, j)),
            pl.BlockSpec((3, tn), lambda i, j: (0, j + nj)),
            pl.BlockSpec((1, tn), lambda i, j: (0, j)),
            pl.BlockSpec((1, tn), lambda i, j: (0, j + nj)),
        ],
        out_specs=pl.BlockSpec((tm, tn), lambda i, j: (i, j)),
        scratch_shapes=[pltpu.VMEM((tm + CONV_HALO, d), BF16)],
        compiler_params=_params("parallel", "arbitrary"),
        name="ffn_up",
    )(h2d, h2d, ln_g.reshape(1, d), w_up, w_up, conv_w, conv_w, conv_b.reshape(1, -1), conv_b.reshape(1, -1))


def _conv_ffn(h2d, ln_g, w_up, conv_w, conv_b, w_down, layer, seq):
    act = _ffn_up(h2d, ln_g, w_up, layer, conv_w, conv_b, seq)
    return _proj_residual(act, w_down, layer, h2d, _tiles()["tn_down"])


def _diff_col_gain(q_gain, k_gain):
    scale = HEAD_DIM ** -0.5
    nqk = 2 * N_HEADS
    gain = jnp.concatenate([jnp.tile(q_gain * scale, nqk), jnp.tile(k_gain, nqk),
                            jnp.ones((N_HEADS * V_DIM,), F32)])
    return gain


def _dil_col_gain(q_gain, k_gain):
    scale = HEAD_DIM ** -0.5
    parts = []
    for g in range(len(SW_DILATIONS)):
        parts += [jnp.tile(q_gain[g] * scale, N_HEADS), jnp.tile(k_gain[g], N_HEADS),
                  jnp.ones((N_HEADS * V_DIM,), F32)]
    return jnp.concatenate(parts)


def _tile_flags(n_cols, tn, normed_ranges):
    flags = []
    for j in range(n_cols // tn):
        lo = j * tn
        flags.append(int(any(a <= lo < b for a, b in normed_ranges)))
    return jnp.asarray(flags, jnp.int32)


def kernel(x, rel_bias, ln_mix, ln_ffn, a_w_qkv, a_q_norm, a_k_norm, a_lambda, a_subln, a_w_o,
           b_w_qkv, b_q_norm, b_k_norm, b_w_o, f_w_up, f_conv_w, f_conv_b, f_w_down):
    batch, seq, d = x.shape
    m = batch * seq
    t = _tiles()
    h = x.reshape(m, d)

    nqk = 2 * N_HEADS * HEAD_DIM
    flags0 = _tile_flags(a_w_qkv.shape[2], t["tn_qkv"], [(0, 2 * nqk)])
    (qkv0,) = _qkv_proj(h, ln_mix[0], a_w_qkv, _diff_col_gain(a_q_norm[0], a_k_norm[0]), flags0, (1,))
    lam_init = 0.8 - 0.6 * math.exp(-0.3 * 0)
    attn0 = _diff_attention(qkv0, _diff_bias_table(rel_bias, seq), a_lambda[0], a_subln[0], batch, seq, lam_init)
    h = _attn_out_proj([attn0], [], a_w_o, h)
    h = _conv_ffn(h, ln_ffn[0], f_w_up, f_conv_w[0], f_conv_b[0], f_w_down, 0, seq)

    gcols = 2 * N_HEADS * HEAD_DIM + N_HEADS * V_DIM
    nq1 = N_HEADS * HEAD_DIM
    flags1 = _tile_flags(b_w_qkv.shape[2], t["tn_qkv"],
                         [(g * gcols, g * gcols + 2 * nq1) for g in range(len(SW_DILATIONS))])
    qkv1 = _qkv_proj(h, ln_mix[1], b_w_qkv, _dil_col_gain(b_q_norm[0], b_k_norm[0]), flags1, SW_DILATIONS)
    dil_bias = _dil_bias_table(rel_bias)
    outs, lses = [], []
    for g in range(len(SW_DILATIONS)):
        o, lse = _dilated_group_attention(qkv1[g], dil_bias, g, batch, seq)
        outs.append(o)
        lses.append(lse)
    h = _attn_out_proj(outs, lses, b_w_o, h)
    h = _conv_ffn(h, ln_ffn[1], f_w_up, f_conv_w[1], f_conv_b[1], f_w_down, 1, seq)
    return h.reshape(batch, seq, d)
```

```python
import functools
import math

import jax
import jax.numpy as jnp
from jax import lax
from jax.experimental import pallas as pl
from jax.experimental.pallas import tpu as pltpu

F32 = jnp.float32
BF16 = jnp.bfloat16

N_HEADS = 8
HEAD_DIM = 128
V_DIM = 2 * HEAD_DIM
SW_DILATIONS = (1, 4, 16)
SW_STEPS = 128
NUM_BUCKETS = 32
MAX_DISTANCE = 2048
EPS = 1e-6
NEG = -1e30

LANES = 128
COL_TILE = 256
ATTN_TILE = 256
CONV_HALO = 16
VMEM_LIMIT = 60 * 1024 * 1024


def _tiles():
    return dict(tm=1024, tn_qkv=1024, qkv_rows_chunk=256, tn_up=512, tn_down=512, norm_chunk=256,
                tm_attn_out=512, attn_out_chunk=256)


def _params(*sem):
    return pltpu.CompilerParams(dimension_semantics=sem, vmem_limit_bytes=VMEM_LIMIT)


def _rmsnorm_rows(x_ref, g_ref, out_ref, row0, rows, chunk):
    g = g_ref[...]
    for r in range(0, rows, chunk):
        x = x_ref[r:r + chunk, :]
        rs = lax.rsqrt(jnp.mean(x * x, axis=-1, keepdims=True) + EPS)
        out_ref[row0 + r:row0 + r + chunk, :] = (x * rs * g).astype(BF16)


def _t5_bucket(dist):
    max_exact = NUM_BUCKETS // 2
    n = jnp.maximum(dist, 0)
    nf = jnp.maximum(n, 1).astype(F32)
    large = max_exact + (jnp.log(nf / max_exact) / math.log(MAX_DISTANCE / max_exact)
                         * (NUM_BUCKETS - max_exact)).astype(jnp.int32)
    large = jnp.minimum(large, NUM_BUCKETS - 1)
    return jnp.where(n < max_exact, n, large)


def _bucket_range(dmin, dmax):
    def bucket(n):
        n = max(n, 0)
        if n < NUM_BUCKETS // 2:
            return n
        half = NUM_BUCKETS // 2
        return min(half + int(math.log(n / half) / math.log(MAX_DISTANCE / half) * half), NUM_BUCKETS - 1)
    return max(bucket(dmin) - 1, 0), min(bucket(dmax) + 1, NUM_BUCKETS - 1)


def _bias_lookup(rb_ref, bucket, h, bmin, bmax):
    val = jnp.full(bucket.shape, rb_ref[bmin, h], F32)
    for b in range(bmin + 1, bmax + 1):
        val = jnp.where(bucket == b, rb_ref[b, h], val)
    return val


def _diff_bias_kernel(rb_ref, o_ref):
    h = pl.program_id(0)
    t = ATTN_TILE
    delta = (lax.broadcasted_iota(jnp.int32, (t, t), 0) - lax.broadcasted_iota(jnp.int32, (t, t), 1))
    for d in range(o_ref.shape[1]):
        dist = d * t + delta
        val = _bias_lookup(rb_ref, _t5_bucket(dist), h, *_bucket_range(d * t - (t - 1), d * t + (t - 1)))
        o_ref[0, d] = jnp.where(dist >= 0, val, NEG) if d == 0 else val


def _diff_bias_table(rel_bias, seq):
    nd = seq // ATTN_TILE
    return pl.pallas_call(
        _diff_bias_kernel,
        out_shape=jax.ShapeDtypeStruct((N_HEADS, nd, ATTN_TILE, ATTN_TILE), F32),
        grid=(N_HEADS,),
        in_specs=[pl.BlockSpec(memory_space=pltpu.SMEM)],
        out_specs=pl.BlockSpec((1, nd, ATTN_TILE, ATTN_TILE), lambda h: (h, 0, 0, 0)),
        compiler_params=_params("parallel"),
        name="diff_bias_table",
    )(rel_bias)


def _dil_bias_kernel(rb_ref, o_ref):
    h = pl.program_id(0)
    v = pl.program_id(1)
    a = lax.broadcasted_iota(jnp.int32, (SW_STEPS, 2 * SW_STEPS), 0)
    b = lax.broadcasted_iota(jnp.int32, (SW_STEPS, 2 * SW_STEPS), 1)
    dist_sub = a + SW_STEPS * (1 - v) - b
    in_window = (dist_sub >= 0) & (dist_sub <= SW_STEPS)
    for g, dil in enumerate(SW_DILATIONS):
        dist = jnp.clip(dist_sub, 0, SW_STEPS) * dil
        val = _bias_lookup(rb_ref, _t5_bucket(dist), h, *_bucket_range(0, SW_STEPS * dil))
        o_ref[g, 0, 0] = jnp.where(in_window, val, NEG)


def _dil_bias_table(rel_bias):
    ng = len(SW_DILATIONS)
    return pl.pallas_call(
        _dil_bias_kernel,
        out_shape=jax.ShapeDtypeStruct((ng, N_HEADS, 2, SW_STEPS, 2 * SW_STEPS), F32),
        grid=(N_HEADS, 2),
        in_specs=[pl.BlockSpec(memory_space=pltpu.SMEM)],
        out_specs=pl.BlockSpec((ng, 1, 1, SW_STEPS, 2 * SW_STEPS), lambda h, v: (0, h, v, 0, 0)),
        compiler_params=_params("parallel", "parallel"),
        name="dil_bias_table",
    )(rel_bias)


def _qkv_kernel(flag_ref, x_ref, g_ref, w_ref, cg_ref, *refs, tm, tn, chunk, dils, steps_per_group):
    o_refs = refs[:len(dils)]
    xn_ref, acc_ref = refs[len(dils):]
    j = pl.program_id(1)

    @pl.when(j == 0)
    def _():
        _rmsnorm_rows(x_ref, g_ref, xn_ref, 0, tm, chunk)

    normed = flag_ref[j] != 0
    per_tile = COL_TILE // LANES
    rows_chunk = acc_ref.shape[1]

    def finish(o_ref, dil, acc, r0):
        n_rows = rows_chunk // dil
        out_rows = slice(r0 // dil, r0 // dil + n_rows)
        for c in range(tn // LANES):
            y_all = acc[:, c * LANES:(c + 1) * LANES]
            if dil > 1:
                acc_ref[c] = y_all
            for r in range(dil):
                y = y_all if dil == 1 else acc_ref[c, pl.ds(r, n_rows, stride=dil), :]
                rs = lax.rsqrt(jnp.mean(y * y, axis=-1, keepdims=True) + EPS)
                scale = jnp.where(normed, rs, 1.0)
                lo = r * COL_TILE + (c % per_tile) * LANES
                o_ref[c // per_tile, out_rows, lo:lo + LANES] = (
                    y * scale * cg_ref[:, c * LANES:(c + 1) * LANES]).astype(BF16)

    def project_group(o_ref, dil):
        wb = w_ref[...].astype(BF16)
        starts = list(range(0, tm, rows_chunk))

        def project(r0):
            return jnp.dot(xn_ref[r0:r0 + rows_chunk, :], wb, preferred_element_type=F32)

        acc_next = project(starts[0])
        for i, r0 in enumerate(starts):
            acc = acc_next
            if i + 1 < len(starts):
                acc_next = project(starts[i + 1])
            finish(o_ref, dil, acc, r0)

    if len(dils) == 1:
        project_group(o_refs[0], dils[0])
    else:
        for g, dil in enumerate(dils):
            pl.when(j // steps_per_group == g)(functools.partial(project_group, o_refs[g], dil))


def _qkv_proj(x2d, ln_g, w, col_gain, tile_flags, dils):
    t = _tiles()
    tm, tn = t["tm"], t["tn_qkv"]
    m, d = x2d.shape
    n = w.shape[2]
    ng = len(dils)
    steps_per_group = n // ng // tn
    group_tiles = n // ng // COL_TILE
    kern = functools.partial(_qkv_kernel, tm=tm, tn=tn, chunk=t["norm_chunk"], dils=dils,
                             steps_per_group=steps_per_group)

    def out_spec(g, dil):
        def index(i, j, f):
            return (jnp.clip(j - g * steps_per_group, 0, steps_per_group - 1), i, 0)
        return pl.BlockSpec((tn // COL_TILE, tm // dil, dil * COL_TILE), index)

    return pl.pallas_call(
        kern,
        out_shape=[jax.ShapeDtypeStruct((group_tiles, m // dil, dil * COL_TILE), BF16) for dil in dils],
        grid_spec=pltpu.PrefetchScalarGridSpec(
            num_scalar_prefetch=1,
            grid=(m // tm, n // tn),
            in_specs=[
                pl.BlockSpec((tm, d), lambda i, j, f: (i, 0)),
                pl.BlockSpec((1, d), lambda i, j, f: (0, 0)),
                pl.BlockSpec((None, d, tn), lambda i, j, f: (0, 0, j)),
                pl.BlockSpec((1, tn), lambda i, j, f: (0, j)),
            ],
            out_specs=[out_spec(g, dil) for g, dil in enumerate(dils)],
            scratch_shapes=[pltpu.VMEM((tm, d), BF16),
                            pltpu.VMEM((tn // LANES, t["qkv_rows_chunk"], LANES), F32)],
        ),
        compiler_params=_params("parallel", "arbitrary"),
        name="qkv_proj",
    )(tile_flags, x2d, ln_g.reshape(1, d), w, col_gain.reshape(1, n))


def _diff_attn_kernel(q_ref, k_ref, v_ref, bias_ref, lam_ref, sg_ref, o_ref, *, lam_init, seq):
    t = ATTN_TILE
    lam = lam_ref[...]
    lam_full = (jnp.exp(jnp.sum(lam[0:1] * lam[1:2], axis=-1, keepdims=True))
                - jnp.exp(jnp.sum(lam[2:3] * lam[3:4], axis=-1, keepdims=True)) + lam_init)
    gain = sg_ref[...] * (1.0 - lam_init)

    def logits(c, mp):
        width = (c + 1) * t
        lo = mp * HEAD_DIM
        bias = jnp.concatenate([bias_ref[0, c - ki] for ki in range(c + 1)], axis=1)
        return lax.dot_general(q_ref[0, c * t:(c + 1) * t, lo:lo + HEAD_DIM], k_ref[0, 0:width, lo:lo + HEAD_DIM],
                               (((1,), (1,)), ((), ())), preferred_element_type=F32) + bias

    def attend(c, s):
        p = jnp.exp(s - jnp.max(s, axis=-1, keepdims=True))
        den = jnp.sum(p, axis=-1, keepdims=True)
        return jnp.dot(p.astype(BF16), v_ref[0, 0:(c + 1) * t, :], preferred_element_type=F32) / den

    units = [(c, mp) for c in range(seq // t) for mp in range(2)]
    s_next = logits(*units[0])
    maps = {}
    for i, (c, mp) in enumerate(units):
        s_cur = s_next
        if i + 1 < len(units):
            s_next = logits(*units[i + 1])
        maps[mp] = attend(c, s_cur)
        if mp == 1:
            o = maps[0] - lam_full * maps[1]
            rs = lax.rsqrt(jnp.mean(o * o, axis=-1, keepdims=True) + EPS)
            o_ref[c * t:(c + 1) * t, :] = (o * rs * gain).astype(BF16)


def _diff_attention(qkv, bias_tab, lam, subln_g, batch, seq, lam_init):
    t = ATTN_TILE
    nq = seq // t
    m = batch * seq
    kern = functools.partial(_diff_attn_kernel, lam_init=lam_init, seq=seq)
    return pl.pallas_call(
        kern,
        out_shape=jax.ShapeDtypeStruct((m, N_HEADS * V_DIM), BF16),
        grid=(N_HEADS, batch),
        in_specs=[
            pl.BlockSpec((1, seq, COL_TILE), lambda h, b: (h, b, 0)),
            pl.BlockSpec((1, seq, COL_TILE), lambda h, b: (N_HEADS + h, b, 0)),
            pl.BlockSpec((1, seq, COL_TILE), lambda h, b: (2 * N_HEADS + h, b, 0)),
            pl.BlockSpec((1, nq, t, t), lambda h, b: (h, 0, 0, 0)),
            pl.BlockSpec((4, HEAD_DIM), lambda h, b: (0, 0)),
            pl.BlockSpec((1, V_DIM), lambda h, b: (0, 0)),
        ],
        out_specs=pl.BlockSpec((seq, V_DIM), lambda h, b: (b, h)),
        compiler_params=_params("parallel", "parallel"),
        name="diff_attention",
    )(qkv, qkv, qkv, bias_tab, lam, subln_g.reshape(1, V_DIM))


def _dil_attn_kernel(q_ref, k_ref, v_ref, bias_ref, o_ref, lse_ref, *scratch, sub_len, rows, dil):
    blk = SW_STEPS
    r = pl.program_id(1)
    c = pl.program_id(2)
    nblk = rows // blk
    single = sub_len == blk
    lane = lax.broadcasted_iota(jnp.int32, (blk, LANES), 1)

    def window(n):
        ng = c * nblk + n
        if single:
            return 0, 1, blk
        return pl.multiple_of(jnp.maximum(ng - 1, 0) * blk, blk), (ng == 0).astype(jnp.int32), 2 * blk

    def logits(n, h):
        start, variant, width = window(n)
        tile, lo = h // 2, (h % 2) * HEAD_DIM
        qt = q_ref[tile, n * blk:(n + 1) * blk, lo:lo + HEAD_DIM]
        kw = k_ref[tile, pl.ds(start, width), lo:lo + HEAD_DIM]
        bias = bias_ref[0, h, variant][:, :width]
        return lax.dot_general(qt, kw, (((1,), (1,)), ((), ())), preferred_element_type=F32) + bias

    def attend(n, h, s):
        start, _, width = window(n)
        mx = jnp.max(s, axis=-1, keepdims=True)
        p = jnp.exp(s - mx)
        den = jnp.sum(p, axis=-1, keepdims=True)
        o = jnp.dot(p.astype(BF16), v_ref[h, pl.ds(start, width), :], preferred_element_type=F32) / den
        return o, mx + jnp.log(den)

    units = [(n, h) for n in range(nblk) for h in range(N_HEADS)]
    ahead = 2 if nblk == 1 else 1
    queue = [logits(*u) for u in units[:ahead]]
    lse_tile = None
    for i, (n, h) in enumerate(units):
        s = queue.pop(0)
        if i + ahead < len(units):
            queue.append(logits(*units[i + ahead]))
        o, lse = attend(n, h, s)
        if dil == 1:
            out_rows = slice(n * blk, (n + 1) * blk)
            o_ref[out_rows, h * V_DIM:(h + 1) * V_DIM] = o.astype(BF16)
        else:
            out_rows = pl.ds(r + (c * nblk + n) * (blk * dil), blk, stride=dil)
            for part in range(V_DIM // LANES):
                scratch[0][h * (V_DIM // LANES) + part, out_rows, :] = o[:, part * LANES:(part + 1) * LANES]
        lse_tile = jnp.where(lane == h, lse, jnp.zeros((blk, LANES), F32) if h == 0 else lse_tile)
        if h == N_HEADS - 1:
            (lse_ref if dil == 1 else scratch[1])[out_rows, :] = lse_tile

    if dil > 1:
        @pl.when((r == dil - 1) & (c == pl.num_programs(2) - 1))
        def _():
            for slab in range(scratch[0].shape[0]):
                o_ref[:, slab * LANES:(slab + 1) * LANES] = scratch[0][slab].astype(BF16)
            lse_ref[...] = scratch[1][...]


def _dilated_group_attention(qkv, bias_tab, g, batch, seq):
    dil = SW_DILATIONS[g]
    sub_len = seq // dil
    rows = min(sub_len, 512)
    nc = sub_len // rows
    m = batch * seq
    width = N_HEADS * V_DIM
    kern = functools.partial(_dil_attn_kernel, sub_len=sub_len, rows=rows, dil=dil)
    if dil == 1:
        out_specs = (pl.BlockSpec((rows, width), lambda b, r, c: (b * nc + c, 0)),
                     pl.BlockSpec((rows, LANES), lambda b, r, c: (b * nc + c, 0)))
        scratch, sem = [], ("parallel", "parallel", "parallel")
    else:
        out_specs = (pl.BlockSpec((seq, width), lambda b, r, c: (b, 0)),
                     pl.BlockSpec((seq, LANES), lambda b, r, c: (b, 0)))
        scratch = [pltpu.VMEM((width // LANES, seq, LANES), F32), pltpu.VMEM((seq, LANES), F32)]
        sem = ("parallel", "arbitrary", "arbitrary")
    return pl.pallas_call(
        kern,
        out_shape=(jax.ShapeDtypeStruct((m, width), BF16), jax.ShapeDtypeStruct((m, LANES), F32)),
        grid=(batch, dil, nc),
        in_specs=[
            pl.BlockSpec((4, rows, COL_TILE), lambda b, r, c: (0, b * nc + c, r)),
            pl.BlockSpec((4, sub_len, COL_TILE), lambda b, r, c: (1, b, r)),
            pl.BlockSpec((8, sub_len, COL_TILE), lambda b, r, c: (1, b, r)),
            pl.BlockSpec((1, N_HEADS, 2, SW_STEPS, 2 * SW_STEPS), lambda b, r, c: (g, 0, 0, 0, 0)),
        ],
        out_specs=out_specs,
        scratch_shapes=scratch,
        compiler_params=_params(*sem),
        name=f"dilated_attention_g{g}",
    )(qkv, qkv, qkv, bias_tab)


def _proj_res_kernel(a_ref, w_ref, r_ref, o_ref):
    o_ref[...] = r_ref[...] + jnp.dot(a_ref[...], w_ref[...].astype(BF16), preferred_element_type=F32)


def _proj_residual(a, w, layer, res, tn):
    tm = _tiles()["tm"]
    m, k = a.shape
    n = w.shape[2]
    return pl.pallas_call(
        _proj_res_kernel,
        out_shape=jax.ShapeDtypeStruct((m, n), F32),
        grid=(m // tm, n // tn),
        in_specs=[
            pl.BlockSpec((tm, k), lambda i, j: (i, 0)),
            pl.BlockSpec((None, k, tn), lambda i, j: (layer, 0, j)),
            pl.BlockSpec((tm, tn), lambda i, j: (i, j)),
        ],
        out_specs=pl.BlockSpec((tm, tn), lambda i, j: (i, j)),
        compiler_params=_params("parallel", "arbitrary"),
        name="proj_residual",
    )(a, w, res)


def _attn_out_kernel(*refs, n_groups, tm, chunk):
    o_refs = refs[:n_groups]
    l_refs = refs[n_groups:2 * n_groups] if n_groups > 1 else ()
    w_ref, r_ref, out_ref, wb_ref = refs[-4:]

    @pl.when(pl.program_id(0) == 0)
    def _():
        wb_ref[...] = w_ref[...].astype(BF16)

    def lhs(r):
        if n_groups == 1:
            return o_refs[0][r:r + chunk, :]
        ls = [l_ref[r:r + chunk, :] for l_ref in l_refs]
        mx = functools.reduce(jnp.maximum, ls)
        es = [jnp.exp(l - mx) for l in ls]
        den = functools.reduce(jnp.add, es)
        alphas = [e / den for e in es]
        heads = []
        for h in range(N_HEADS):
            cols = slice(h * V_DIM, (h + 1) * V_DIM)
            terms = [al[:, h:h + 1] * o_ref[r:r + chunk, cols].astype(F32) for al, o_ref in zip(alphas, o_refs)]
            heads.append(functools.reduce(jnp.add, terms).astype(BF16))
        return jnp.concatenate(heads, axis=1)

    starts = list(range(0, tm, chunk))
    a_next = lhs(starts[0])
    for i, r in enumerate(starts):
        a_cur = a_next
        if i + 1 < len(starts):
            a_next = lhs(starts[i + 1])
        out_ref[r:r + chunk, :] = r_ref[r:r + chunk, :] + jnp.dot(a_cur, wb_ref[...], preferred_element_type=F32)


def _attn_out_proj(outs, lses, w, res):
    t = _tiles()
    tm, chunk = t["tm_attn_out"], t["attn_out_chunk"]
    m, k = outs[0].shape
    n = w.shape[2]
    kern = functools.partial(_attn_out_kernel, n_groups=len(outs), tm=tm, chunk=chunk)
    o_spec = pl.BlockSpec((tm, k), lambda i: (i, 0))
    l_spec = pl.BlockSpec((tm, LANES), lambda i: (i, 0))
    return pl.pallas_call(
        kern,
        out_shape=jax.ShapeDtypeStruct((m, n), F32),
        grid=(m // tm,),
        in_specs=[o_spec] * len(outs) + [l_spec] * len(lses) + [
            pl.BlockSpec((None, k, n), lambda i: (0, 0, 0), pipeline_mode=pl.Buffered(1)),
            pl.BlockSpec((tm, n), lambda i: (i, 0))],
        out_specs=pl.BlockSpec((tm, n), lambda i: (i, 0)),
        scratch_shapes=[pltpu.VMEM((k, n), BF16)],
        compiler_params=_params("arbitrary"),
        name="attn_out_proj",
    )(*outs, *lses, w, res)


def _ffn_up_kernel(x_ref, halo_ref, g_ref, wg_ref, wv_ref, cwg_ref, cwv_ref, cbg_ref, cbv_ref, o_ref, xn_ref,
                   *, tm, tiles_per_seq, chunk):
    i = pl.program_id(0)
    j = pl.program_id(1)
    halo = CONV_HALO

    @pl.when(j == 0)
    def _():
        _rmsnorm_rows(x_ref, g_ref, xn_ref, halo, tm, chunk)
        first = (i % tiles_per_seq) == 0

        @pl.when(first)
        def _():
            xn_ref[0:halo, :] = jnp.zeros((halo, xn_ref.shape[1]), BF16)

        @pl.when(jnp.logical_not(first))
        def _():
            _rmsnorm_rows(halo_ref, g_ref, xn_ref, 0, halo, halo)

    xn = xn_ref[...]

    def conv_branch(w_ref, cw_ref, cb_ref):
        u = jnp.dot(xn, w_ref[...].astype(BF16), preferred_element_type=F32)
        cw = cw_ref[...]
        return (cw[0:1] * u[halo - 2:halo - 2 + tm] + cw[1:2] * u[halo - 1:halo - 1 + tm]
                + cw[2:3] * u[halo:halo + tm]) + cb_ref[...]

    gate = conv_branch(wg_ref, cwg_ref, cbg_ref)
    val = conv_branch(wv_ref, cwv_ref, cbv_ref)
    o_ref[...] = (gate / (1.0 + jnp.exp(-gate)) * val).astype(BF16)


def _ffn_up(h2d, ln_g, w_up, layer, conv_w, conv_b, seq):
    t = _tiles()
    tm, tn = t["tm"], t["tn_up"]
    m, d = h2d.shape
    dff = w_up.shape[2] // 2
    nj = dff // tn
    halo_blocks = tm // CONV_HALO
    kern = functools.partial(_ffn_up_kernel, tm=tm, tiles_per_seq=seq // tm, chunk=t["norm_chunk"])
    return pl.pallas_call(
        kern,
        out_shape=jax.ShapeDtypeStruct((m, dff), BF16),
        grid=(m // tm, nj),
        in_specs=[
            pl.BlockSpec((tm, d), lambda i, j: (i, 0)),
            pl.BlockSpec((CONV_HALO, d), lambda i, j: (jnp.maximum(i * halo_blocks - 1, 0), 0)),
            pl.BlockSpec((1, d), lambda i, j: (0, 0)),
            pl.BlockSpec((None, d, tn), lambda i, j: (layer, 0, j)),
            pl.BlockSpec((None, d, tn), lambda i, j: (layer, 0, j + nj)),
            pl.BlockSpec((3, tn), lambda i, j: (0, j)),
            pl.BlockSpec((3, tn), lambda i, j: (0, j + nj)),
            pl.BlockSpec((1, tn), lambda i, j: (0, j)),
            pl.BlockSpec((1, tn), lambda i, j: (0, j + nj)),
        ],
        out_specs=pl.BlockSpec((tm, tn), lambda i, j: (i, j)),
        scratch_shapes=[pltpu.VMEM((tm + CONV_HALO, d), BF16)],
        compiler_params=_params("parallel", "arbitrary"),
        name="ffn_up",
    )(h2d, h2d, ln_g.reshape(1, d), w_up, w_up, conv_w, conv_w, conv_b.reshape(1, -1), conv_b.reshape(1, -1))


def _conv_ffn(h2d, ln_g, w_up, conv_w, conv_b, w_down, layer, seq):
    act = _ffn_up(h2d, ln_g, w_up, layer, conv_w, conv_b, seq)
    return _proj_residual(act, w_down, layer, h2d, _tiles()["tn_down"])


def _diff_col_gain(q_gain, k_gain):
    scale = HEAD_DIM ** -0.5
    nqk = 2 * N_HEADS
    gain = jnp.concatenate([jnp.tile(q_gain * scale, nqk), jnp.tile(k_gain, nqk),
                            jnp.ones((N_HEADS * V_DIM,), F32)])
    return gain


def _dil_col_gain(q_gain, k_gain):
    scale = HEAD_DIM ** -0.5
    parts = []
    for g in range(len(SW_DILATIONS)):
        parts += [jnp.tile(q_gain[g] * scale, N_HEADS), jnp.tile(k_gain[g], N_HEADS),
                  jnp.ones((N_HEADS * V_DIM,), F32)]
    return jnp.concatenate(parts)


def _tile_flags(n_cols, tn, normed_ranges):
    flags = []
    for j in range(n_cols // tn):
        lo = j * tn
        flags.append(int(any(a <= lo < b for a, b in normed_ranges)))
    return jnp.asarray(flags, jnp.int32)


def kernel(x, rel_bias, ln_mix, ln_ffn, a_w_qkv, a_q_norm, a_k_norm, a_lambda, a_subln, a_w_o,
           b_w_qkv, b_q_norm, b_k_norm, b_w_o, f_w_up, f_conv_w, f_conv_b, f_w_down):
    batch, seq, d = x.shape
    m = batch * seq
    t = _tiles()
    h = x.reshape(m, d)

    nqk = 2 * N_HEADS * HEAD_DIM
    flags0 = _tile_flags(a_w_qkv.shape[2], t["tn_qkv"], [(0, 2 * nqk)])
    (qkv0,) = _qkv_proj(h, ln_mix[0], a_w_qkv, _diff_col_gain(a_q_norm[0], a_k_norm[0]), flags0, (1,))
    lam_init = 0.8 - 0.6 * math.exp(-0.3 * 0)
    attn0 = _diff_attention(qkv0, _diff_bias_table(rel_bias, seq), a_lambda[0], a_subln[0], batch, seq, lam_init)
    h = _attn_out_proj([attn0], [], a_w_o, h)
    h = _conv_ffn(h, ln_ffn[0], f_w_up, f_conv_w[0], f_conv_b[0], f_w_down, 0, seq)

    gcols = 2 * N_HEADS * HEAD_DIM + N_HEADS * V_DIM
    nq1 = N_HEADS * HEAD_DIM
    flags1 = _tile_flags(b_w_qkv.shape[2], t["tn_qkv"],
                         [(g * gcols, g * gcols + 2 * nq1) for g in range(len(SW_DILATIONS))])
    qkv1 = _qkv_proj(h, ln_mix[1], b_w_qkv, _dil_col_gain(b_q_norm[0], b_k_norm[0]), flags1, SW_DILATIONS)
    dil_bias = _dil_bias_table(rel_bias)
    outs, lses = [], []
    for g in range(len(SW_DILATIONS)):
        o, lse = _dilated_group_attention(qkv1[g], dil_bias, g, batch, seq)
        outs.append(o)
        lses.append(lse)
    h = _attn_out_proj(outs, lses, b_w_o, h)
    h = _conv_ffn(h, ln_ffn[1], f_w_up, f_conv_w[1], f_conv_b[1], f_w_down, 1, seq)
    return h.reshape(batch, seq, d)
```

```python
import functools
import math

import jax
import jax.numpy as jnp
from jax import lax
from jax.experimental import pallas as pl
from jax.experimental.pallas import tpu as pltpu

F32 = jnp.float32
BF16 = jnp.bfloat16

N_HEADS = 8
HEAD_DIM = 128
V_DIM = 2 * HEAD_DIM
SW_DILATIONS = (1, 4, 16)
SW_STEPS = 128
NUM_BUCKETS = 32
MAX_DISTANCE = 2048
EPS = 1e-6
NEG = -1e30

LANES = 128
COL_TILE = 256
ATTN_TILE = 256
CONV_HALO = 16
VMEM_LIMIT = 60 * 1024 * 1024


def _tiles():
    return dict(tm=1024, tn_qkv=1024, qkv_rows_chunk=256, tn_up=512, tn_down=512,
                norm_chunk=256, tm_attn_out=512, attn_out_chunk=256)


def _params(*sem):
    return pltpu.CompilerParams(dimension_semantics=sem, vmem_limit_bytes=VMEM_LIMIT)


def _rmsnorm_rows(x_ref, g_ref, out_ref, row0, rows, chunk):
    g = g_ref[...]
    for r in range(0, rows, chunk):
        x = x_ref[r:r + chunk, :]
        rs = lax.rsqrt(jnp.mean(x * x, axis=-1, keepdims=True) + EPS)
        out_ref[row0 + r:row0 + r + chunk, :] = (x * rs * g).astype(BF16)


def _t5_bucket(dist):
    max_exact = NUM_BUCKETS // 2
    n = jnp.maximum(dist, 0)
    nf = jnp.maximum(n, 1).astype(F32)
    large = max_exact + (jnp.log(nf / max_exact) / math.log(MAX_DISTANCE / max_exact)
                         * (NUM_BUCKETS - max_exact)).astype(jnp.int32)
    large = jnp.minimum(large, NUM_BUCKETS - 1)
    return jnp.where(n < max_exact, n, large)


def _bucket_range(dmin, dmax):
    def bucket(n):
        n = max(n, 0)
        if n < NUM_BUCKETS // 2:
            return n
        half = NUM_BUCKETS // 2
        return min(half + int(math.log(n / half) / math.log(MAX_DISTANCE / half) * half), NUM_BUCKETS - 1)
    return max(bucket(dmin) - 1, 0), min(bucket(dmax) + 1, NUM_BUCKETS - 1)


def _bias_lookup(rb_ref, bucket, h, bmin, bmax):
    val = jnp.full(bucket.shape, rb_ref[bmin, h], F32)
    for b in range(bmin + 1, bmax + 1):
        val = jnp.where(bucket == b, rb_ref[b, h], val)
    return val


def _diff_bias_kernel(rb_ref, o_ref):
    h = pl.program_id(0)
    t = ATTN_TILE
    delta = (lax.broadcasted_iota(jnp.int32, (t, t), 0) - lax.broadcasted_iota(jnp.int32, (t, t), 1))
    for d in range(o_ref.shape[1]):
        dist = d * t + delta
        val = _bias_lookup(rb_ref, _t5_bucket(dist), h, *_bucket_range(d * t - (t - 1), d * t + (t - 1)))
        o_ref[0, d] = jnp.where(dist >= 0, val, NEG) if d == 0 else val


def _diff_bias_table(rel_bias, seq):
    nd = seq // ATTN_TILE
    return pl.pallas_call(
        _diff_bias_kernel,
        out_shape=jax.ShapeDtypeStruct((N_HEADS, nd, ATTN_TILE, ATTN_TILE), F32),
        grid=(N_HEADS,),
        in_specs=[pl.BlockSpec(memory_space=pltpu.SMEM)],
        out_specs=pl.BlockSpec((1, nd, ATTN_TILE, ATTN_TILE), lambda h: (h, 0, 0, 0)),
        compiler_params=_params("parallel"),
        name="diff_bias_table",
    )(rel_bias)


def _dil_bias_kernel(rb_ref, o_ref):
    h = pl.program_id(0)
    v = pl.program_id(1)
    a = lax.broadcasted_iota(jnp.int32, (SW_STEPS, 2 * SW_STEPS), 0)
    b = lax.broadcasted_iota(jnp.int32, (SW_STEPS, 2 * SW_STEPS), 1)
    dist_sub = a + SW_STEPS * (1 - v) - b
    in_window = (dist_sub >= 0) & (dist_sub <= SW_STEPS)
    for g, dil in enumerate(SW_DILATIONS):
        dist = jnp.clip(dist_sub, 0, SW_STEPS) * dil
        val = _bias_lookup(rb_ref, _t5_bucket(dist), h, *_bucket_range(0, SW_STEPS * dil))
        o_ref[g, 0, 0] = jnp.where(in_window, val, NEG)


def _dil_bias_table(rel_bias):
    ng = len(SW_DILATIONS)
    return pl.pallas_call(
        _dil_bias_kernel,
        out_shape=jax.ShapeDtypeStruct((ng, N_HEADS, 2, SW_STEPS, 2 * SW_STEPS), F32),
        grid=(N_HEADS, 2),
        in_specs=[pl.BlockSpec(memory_space=pltpu.SMEM)],
        out_specs=pl.BlockSpec((ng, 1, 1, SW_STEPS, 2 * SW_STEPS), lambda h, v: (0, h, v, 0, 0)),
        compiler_params=_params("parallel", "parallel"),
        name="dil_bias_table",
    )(rel_bias)


def _qkv_kernel(flag_ref, x_ref, g_ref, w_ref, cg_ref, *refs, tm, tn, chunk, dils, steps_per_group):
    o_refs = refs[:len(dils)]
    xn_ref, acc_ref = refs[len(dils):]
    j = pl.program_id(1)

    @pl.when(j == 0)
    def _():
        _rmsnorm_rows(x_ref, g_ref, xn_ref, 0, tm, chunk)

    normed = flag_ref[j] != 0
    per_tile = COL_TILE // LANES
    rows_chunk = acc_ref.shape[1]

    def finish(o_ref, dil, acc, r0):
        n_rows = rows_chunk // dil
        out_rows = slice(r0 // dil, r0 // dil + n_rows)
        for c in range(tn // LANES):
            y_all = acc[:, c * LANES:(c + 1) * LANES]
            if dil > 1:
                acc_ref[c] = y_all
            for r in range(dil):
                y = y_all if dil == 1 else acc_ref[c, pl.ds(r, n_rows, stride=dil), :]
                rs = lax.rsqrt(jnp.mean(y * y, axis=-1, keepdims=True) + EPS)
                scale = jnp.where(normed, rs, 1.0)
                lo = r * COL_TILE + (c % per_tile) * LANES
                o_ref[c // per_tile, out_rows, lo:lo + LANES] = (
                    y * scale * cg_ref[:, c * LANES:(c + 1) * LANES]).astype(BF16)

    def project_group(o_ref, dil):
        wb = w_ref[...].astype(BF16)
        starts = list(range(0, tm, rows_chunk))

        def project(r0):
            return jnp.dot(xn_ref[r0:r0 + rows_chunk, :], wb, preferred_element_type=F32)

        acc_next = project(starts[0])
        for i, r0 in enumerate(starts):
            acc = acc_next
            if i + 1 < len(starts):
                acc_next = project(starts[i + 1])
            finish(o_ref, dil, acc, r0)

    if len(dils) == 1:
        project_group(o_refs[0], dils[0])
    else:
        for g, dil in enumerate(dils):
            pl.when(j // steps_per_group == g)(functools.partial(project_group, o_refs[g], dil))


def _qkv_proj(x2d, ln_g, w, col_gain, tile_flags, dils):
    t = _tiles()
    tm, tn = t["tm"], t["tn_qkv"]
    m, d = x2d.shape
    n = w.shape[2]
    ng = len(dils)
    steps_per_group = n // ng // tn
    group_tiles = n // ng // COL_TILE
    kern = functools.partial(_qkv_kernel, tm=tm, tn=tn, chunk=t["norm_chunk"], dils=dils,
                             steps_per_group=steps_per_group)

    def out_spec(g, dil):
        def index(i, j, f):
            return (jnp.clip(j - g * steps_per_group, 0, steps_per_group - 1), i, 0)
        return pl.BlockSpec((tn // COL_TILE, tm // dil, dil * COL_TILE), index)

    return pl.pallas_call(
        kern,
        out_shape=[jax.ShapeDtypeStruct((group_tiles, m // dil, dil * COL_TILE), BF16) for dil in dils],
        grid_spec=pltpu.PrefetchScalarGridSpec(
            num_scalar_prefetch=1,
            grid=(m // tm, n // tn),
            in_specs=[
                pl.BlockSpec((tm, d), lambda i, j, f: (i, 0)),
                pl.BlockSpec((1, d), lambda i, j, f: (0, 0)),
                pl.BlockSpec((None, d, tn), lambda i, j, f: (0, 0, j)),
                pl.BlockSpec((1, tn), lambda i, j, f: (0, j)),
            ],
            out_specs=[out_spec(g, dil) for g, dil in enumerate(dils)],
            scratch_shapes=[pltpu.VMEM((tm, d), BF16),
                            pltpu.VMEM((tn // LANES, t["qkv_rows_chunk"], LANES), F32)],
        ),
        compiler_params=_params("parallel", "arbitrary"),
        name="qkv_proj",
    )(tile_flags, x2d, ln_g.reshape(1, d), w, col_gain.reshape(1, n))


def _diff_attn_kernel(q_ref, k_ref, v_ref, bias_ref, lam_ref, sg_ref, o_ref, *, lam_init, seq):
    t = ATTN_TILE
    lam = lam_ref[...]
    lam_full = (jnp.exp(jnp.sum(lam[0:1] * lam[1:2], axis=-1, keepdims=True))
                - jnp.exp(jnp.sum(lam[2:3] * lam[3:4], axis=-1, keepdims=True)) + lam_init)
    gain = sg_ref[...] * (1.0 - lam_init)

    def logits(c, mp):
        width = (c + 1) * t
        lo = mp * HEAD_DIM
        bias = jnp.concatenate([bias_ref[0, c - ki] for ki in range(c + 1)], axis=1)
        return lax.dot_general(q_ref[0, c * t:(c + 1) * t, lo:lo + HEAD_DIM], k_ref[0, 0:width, lo:lo + HEAD_DIM],
                               (((1,), (1,)), ((), ())), preferred_element_type=F32) + bias

    def attend(c, s):
        p = jnp.exp(s - jnp.max(s, axis=-1, keepdims=True))
        den = jnp.sum(p, axis=-1, keepdims=True)
        return jnp.dot(p.astype(BF16), v_ref[0, 0:(c + 1) * t, :], preferred_element_type=F32) / den

    units = [(c, mp) for c in reversed(range(seq // t)) for mp in range(2)]
    s_next = logits(*units[0])
    maps = {}
    for i, (c, mp) in enumerate(units):
        s_cur = s_next
        if i + 1 < len(units):
            s_next = logits(*units[i + 1])
        maps[mp] = attend(c, s_cur)
        if mp == 1:
            o = maps[0] - lam_full * maps[1]
            rs = lax.rsqrt(jnp.mean(o * o, axis=-1, keepdims=True) + EPS)
            o_ref[c * t:(c + 1) * t, :] = (o * rs * gain).astype(BF16)


def _diff_attention(qkv, bias_tab, lam, subln_g, batch, seq, lam_init):
    t = ATTN_TILE
    nq = seq // t
    m = batch * seq
    kern = functools.partial(_diff_attn_kernel, lam_init=lam_init, seq=seq)
    return pl.pallas_call(
        kern,
        out_shape=jax.ShapeDtypeStruct((m, N_HEADS * V_DIM), BF16),
        grid=(N_HEADS, batch),
        in_specs=[
            pl.BlockSpec((1, seq, COL_TILE), lambda h, b: (h, b, 0)),
            pl.BlockSpec((1, seq, COL_TILE), lambda h, b: (N_HEADS + h, b, 0)),
            pl.BlockSpec((1, seq, COL_TILE), lambda h, b: (2 * N_HEADS + h, b, 0)),
            pl.BlockSpec((1, nq, t, t), lambda h, b: (h, 0, 0, 0)),
            pl.BlockSpec((4, HEAD_DIM), lambda h, b: (0, 0)),
            pl.BlockSpec((1, V_DIM), lambda h, b: (0, 0)),
        ],
        out_specs=pl.BlockSpec((seq, V_DIM), lambda h, b: (b, h)),
        compiler_params=_params("parallel", "parallel"),
        name="diff_attention",
    )(qkv, qkv, qkv, bias_tab, lam, subln_g.reshape(1, V_DIM))


def _dil_attn_kernel(q_ref, k_ref, v_ref, bias_ref, o_ref, lse_ref, *scratch, sub_len, rows, dil, rps):
    blk = SW_STEPS
    c = pl.program_id(2)
    nblk = rows // blk
    single = sub_len == blk
    lane = lax.broadcasted_iota(jnp.int32, (blk, LANES), 1)

    def window(n):
        ng = c * nblk + n
        if single:
            return 0, 1, blk
        return pl.multiple_of(jnp.maximum(ng - 1, 0) * blk, blk), (ng == 0).astype(jnp.int32), 2 * blk

    def logits(rr, n, h):
        start, variant, width = window(n)
        tile, lo = h // 2, rr * COL_TILE + (h % 2) * HEAD_DIM
        qt = q_ref[tile, n * blk:(n + 1) * blk, lo:lo + HEAD_DIM]
        kw = k_ref[tile, pl.ds(start, width), lo:lo + HEAD_DIM]
        bias = bias_ref[0, h, variant][:, :width]
        return lax.dot_general(qt, kw, (((1,), (1,)), ((), ())), preferred_element_type=F32) + bias

    def attend(rr, n, h, s):
        start, _, width = window(n)
        mx = jnp.max(s, axis=-1, keepdims=True)
        p = jnp.exp(s - mx)
        den = jnp.sum(p, axis=-1, keepdims=True)
        vw = v_ref[h, pl.ds(start, width), rr * COL_TILE:(rr + 1) * COL_TILE]
        o = jnp.dot(p.astype(BF16), vw, preferred_element_type=F32) / den
        return o, mx + jnp.log(den)

    units = [(rr, n, h) for rr in range(rps) for n in range(nblk) for h in range(N_HEADS)]
    ahead = 2 if nblk == 1 else 1
    queue = [logits(*u) for u in units[:ahead]]
    lse_tile = None
    for i, (rr, n, h) in enumerate(units):
        s = queue.pop(0)
        if i + ahead < len(units):
            queue.append(logits(*units[i + ahead]))
        o, lse = attend(rr, n, h, s)
        if dil == 1:
            out_rows = slice(n * blk, (n + 1) * blk)
            o_ref[out_rows, h * V_DIM:(h + 1) * V_DIM] = o.astype(BF16)
        else:
            r = pl.program_id(1) * rps + rr
            out_rows = pl.ds(r + (c * nblk + n) * (blk * dil), blk, stride=dil)
            for part in range(V_DIM // LANES):
                scratch[0][h * (V_DIM // LANES) + part, out_rows, :] = o[:, part * LANES:(part + 1) * LANES]
        lse_tile = jnp.where(lane == h, lse, jnp.zeros((blk, LANES), F32) if h == 0 else lse_tile)
        if h == N_HEADS - 1:
            (lse_ref if dil == 1 else scratch[1])[out_rows, :] = lse_tile

    if dil > 1:
        @pl.when((pl.program_id(1) == pl.num_programs(1) - 1) & (c == pl.num_programs(2) - 1))
        def _():
            for slab in range(scratch[0].shape[0]):
                o_ref[:, slab * LANES:(slab + 1) * LANES] = scratch[0][slab].astype(BF16)
            lse_ref[...] = scratch[1][...]


def _dilated_group_attention(qkv, bias_tab, g, batch, seq):
    dil = SW_DILATIONS[g]
    sub_len = seq // dil
    rows = min(sub_len, 512)
    nc = sub_len // rows
    rps = max(1, 512 // sub_len)
    m = batch * seq
    width = N_HEADS * V_DIM
    kern = functools.partial(_dil_attn_kernel, sub_len=sub_len, rows=rows, dil=dil, rps=rps)
    if dil == 1:
        out_specs = (pl.BlockSpec((rows, width), lambda b, r, c: (b * nc + c, 0)),
                     pl.BlockSpec((rows, LANES), lambda b, r, c: (b * nc + c, 0)))
        scratch, sem = [], ("parallel", "parallel", "parallel")
    else:
        out_specs = (pl.BlockSpec((seq, width), lambda b, r, c: (b, 0)),
                     pl.BlockSpec((seq, LANES), lambda b, r, c: (b, 0)))
        scratch = [pltpu.VMEM((width // LANES, seq, LANES), F32), pltpu.VMEM((seq, LANES), F32)]
        sem = ("parallel", "arbitrary", "arbitrary")
    return pl.pallas_call(
        kern,
        out_shape=(jax.ShapeDtypeStruct((m, width), BF16), jax.ShapeDtypeStruct((m, LANES), F32)),
        grid=(batch, dil // rps, nc),
        in_specs=[
            pl.BlockSpec((4, rows, rps * COL_TILE), lambda b, r, c: (0, b * nc + c, r)),
            pl.BlockSpec((4, sub_len, rps * COL_TILE), lambda b, r, c: (1, b, r)),
            pl.BlockSpec((8, sub_len, rps * COL_TILE), lambda b, r, c: (1, b, r)),
            pl.BlockSpec((1, N_HEADS, 2, SW_STEPS, 2 * SW_STEPS), lambda b, r, c: (g, 0, 0, 0, 0)),
        ],
        out_specs=out_specs,
        scratch_shapes=scratch,
        compiler_params=_params(*sem),
        name=f"dilated_attention_g{g}",
    )(qkv, qkv, qkv, bias_tab)


def _proj_res_kernel(a_ref, w_ref, r_ref, o_ref):
    o_ref[...] = r_ref[...] + jnp.dot(a_ref[...], w_ref[...].astype(BF16), preferred_element_type=F32)


def _proj_residual(a, w, layer, res, tn):
    tm = _tiles()["tm"]
    m, k = a.shape
    n = w.shape[2]
    return pl.pallas_call(
        _proj_res_kernel,
        out_shape=jax.ShapeDtypeStruct((m, n), F32),
        grid=(m // tm, n // tn),
        in_specs=[
            pl.BlockSpec((tm, k), lambda i, j: (i, 0)),
            pl.BlockSpec((None, k, tn), lambda i, j: (layer, 0, j)),
            pl.BlockSpec((tm, tn), lambda i, j: (i, j)),
        ],
        out_specs=pl.BlockSpec((tm, tn), lambda i, j: (i, j)),
        compiler_params=_params("parallel", "arbitrary"),
        name="proj_residual",
    )(a, w, res)


def _attn_out_kernel(*refs, n_groups, tm, chunk):
    o_refs = refs[:n_groups]
    l_refs = refs[n_groups:2 * n_groups] if n_groups > 1 else ()
    w_ref, r_ref, out_ref, wb_ref = refs[-4:]

    @pl.when(pl.program_id(0) == 0)
    def _():
        wb_ref[...] = w_ref[...].astype(BF16)

    def lhs(r):
        if n_groups == 1:
            return o_refs[0][r:r + chunk, :]
        ls = [l_ref[r:r + chunk, :] for l_ref in l_refs]
        mx = functools.reduce(jnp.maximum, ls)
        es = [jnp.exp(l - mx) for l in ls]
        den = functools.reduce(jnp.add, es)
        alphas = [e / den for e in es]
        heads = []
        for h in range(N_HEADS):
            cols = slice(h * V_DIM, (h + 1) * V_DIM)
            terms = [al[:, h:h + 1] * o_ref[r:r + chunk, cols].astype(F32) for al, o_ref in zip(alphas, o_refs)]
            heads.append(functools.reduce(jnp.add, terms).astype(BF16))
        return jnp.concatenate(heads, axis=1)

    starts = list(range(0, tm, chunk))
    a_next = lhs(starts[0])
    for i, r in enumerate(starts):
        a_cur = a_next
        if i + 1 < len(starts):
            a_next = lhs(starts[i + 1])
        out_ref[r:r + chunk, :] = r_ref[r:r + chunk, :] + jnp.dot(a_cur, wb_ref[...], preferred_element_type=F32)


def _attn_out_proj(outs, lses, w, res):
    t = _tiles()
    tm, chunk = t["tm_attn_out"], t["attn_out_chunk"]
    m, k = outs[0].shape
    n = w.shape[2]
    kern = functools.partial(_attn_out_kernel, n_groups=len(outs), tm=tm, chunk=chunk)
    o_spec = pl.BlockSpec((tm, k), lambda i: (i, 0))
    l_spec = pl.BlockSpec((tm, LANES), lambda i: (i, 0))
    return pl.pallas_call(
        kern,
        out_shape=jax.ShapeDtypeStruct((m, n), F32),
        grid=(m // tm,),
        in_specs=[o_spec] * len(outs) + [l_spec] * len(lses) + [
            pl.BlockSpec((None, k, n), lambda i: (0, 0, 0), pipeline_mode=pl.Buffered(1)),
            pl.BlockSpec((tm, n), lambda i: (i, 0))],
        out_specs=pl.BlockSpec((tm, n), lambda i: (i, 0)),
        scratch_shapes=[pltpu.VMEM((k, n), BF16)],
        compiler_params=_params("arbitrary"),
        name="attn_out_proj",
    )(*outs, *lses, w, res)


def _ffn_up_kernel(x_ref, halo_ref, g_ref, wg_ref, wv_ref, cwg_ref, cwv_ref, cbg_ref, cbv_ref, o_ref, xn_ref,
                   *, tm, tiles_per_seq, chunk):
    i = pl.program_id(0)
    j = pl.program_id(1)
    halo = CONV_HALO

    @pl.when(j == 0)
    def _():
        _rmsnorm_rows(x_ref, g_ref, xn_ref, halo, tm, chunk)
        first = (i % tiles_per_seq) == 0

        @pl.when(first)
        def _():
            xn_ref[0:halo, :] = jnp.zeros((halo, xn_ref.shape[1]), BF16)

        @pl.when(jnp.logical_not(first))
        def _():
            _rmsnorm_rows(halo_ref, g_ref, xn_ref, 0, halo, halo)

    xn = xn_ref[...]

    def conv_branch(w_ref, cw_ref, cb_ref):
        u = jnp.dot(xn, w_ref[...].astype(BF16), preferred_element_type=F32)
        cw = cw_ref[...]
        return (cw[0:1] * u[halo - 2:halo - 2 + tm] + cw[1:2] * u[halo - 1:halo - 1 + tm]
                + cw[2:3] * u[halo:halo + tm]) + cb_ref[...]

    gate = conv_branch(wg_ref, cwg_ref, cbg_ref)
    val = conv_branch(wv_ref, cwv_ref, cbv_ref)
    o_ref[...] = (gate / (1.0 + jnp.exp(-gate)) * val).astype(BF16)


def _ffn_up(h2d, ln_g, w_up, layer, conv_w, conv_b, seq):
    t = _tiles()
    tm, tn = t["tm"], t["tn_up"]
    m, d = h2d.shape
    dff = w_up.shape[2] // 2
    nj = dff // tn
    halo_blocks = tm // CONV_HALO
    kern = functools.partial(_ffn_up_kernel, tm=tm, tiles_per_seq=seq // tm, chunk=t["norm_chunk"])
    return pl.pallas_call(
        kern,
        out_shape=jax.ShapeDtypeStruct((m, dff), BF16),
        grid=(m // tm, nj),
        in_specs=[
            pl.BlockSpec((tm, d), lambda i, j: (i, 0)),
            pl.BlockSpec((CONV_HALO, d), lambda i, j: (jnp.maximum(i * halo_blocks - 1, 0), 0)),
            pl.BlockSpec((1, d), lambda i, j: (0, 0)),
            pl.BlockSpec((None, d, tn), lambda i, j: (layer, 0, j)),
            pl.BlockSpec((None, d, tn), lambda i, j: (layer, 0, j + nj)),
            pl.BlockSpec((3, tn), lambda i, j: (0, j)),
            pl.BlockSpec((3, tn), lambda i, j: (0, j + nj)),
            pl.BlockSpec((1, tn), lambda i, j: (0, j)),
            pl.BlockSpec((1, tn), lambda i, j: (0, j + nj)),
        ],
        out_specs=pl.BlockSpec((tm, tn), lambda i, j: (i, j)),
        scratch_shapes=[pltpu.VMEM((tm + CONV_HALO, d), BF16)],
        compiler_params=_params("parallel", "arbitrary"),
        name="ffn_up",
    )(h2d, h2d, ln_g.reshape(1, d), w_up, w_up, conv_w, conv_w, conv_b.reshape(1, -1), conv_b.reshape(1, -1))


def _conv_ffn(h2d, ln_g, w_up, conv_w, conv_b, w_down, layer, seq):
    act = _ffn_up(h2d, ln_g, w_up, layer, conv_w, conv_b, seq)
    return _proj_residual(act, w_down, layer, h2d, _tiles()["tn_down"])


def _diff_col_gain(q_gain, k_gain):
    scale = HEAD_DIM ** -0.5
    nqk = 2 * N_HEADS
    gain = jnp.concatenate([jnp.tile(q_gain * scale, nqk), jnp.tile(k_gain, nqk),
                            jnp.ones((N_HEADS * V_DIM,), F32)])
    return gain


def _dil_col_gain(q_gain, k_gain):
    scale = HEAD_DIM ** -0.5
    parts = []
    for g in range(len(SW_DILATIONS)):
        parts += [jnp.tile(q_gain[g] * scale, N_HEADS), jnp.tile(k_gain[g], N_HEADS),
                  jnp.ones((N_HEADS * V_DIM,), F32)]
    return jnp.concatenate(parts)


def _tile_flags(n_cols, tn, normed_ranges):
    flags = []
    for j in range(n_cols // tn):
        lo = j * tn
        flags.append(int(any(a <= lo < b for a, b in normed_ranges)))
    return jnp.asarray(flags, jnp.int32)


def kernel(x, rel_bias, ln_mix, ln_ffn, a_w_qkv, a_q_norm, a_k_norm, a_lambda, a_subln, a_w_o,
           b_w_qkv, b_q_norm, b_k_norm, b_w_o, f_w_up, f_conv_w, f_conv_b, f_w_down):
    batch, seq, d = x.shape
    m = batch * seq
    t = _tiles()
    h = x.reshape(m, d)

    nqk = 2 * N_HEADS * HEAD_DIM
    flags0 = _tile_flags(a_w_qkv.shape[2], t["tn_qkv"], [(0, 2 * nqk)])
    (qkv0,) = _qkv_proj(h, ln_mix[0], a_w_qkv, _diff_col_gain(a_q_norm[0], a_k_norm[0]), flags0, (1,))
    lam_init = 0.8 - 0.6 * math.exp(-0.3 * 0)
    attn0 = _diff_attention(qkv0, _diff_bias_table(rel_bias, seq), a_lambda[0], a_subln[0], batch, seq, lam_init)
    h = _attn_out_proj([attn0], [], a_w_o, h)
    h = _conv_ffn(h, ln_ffn[0], f_w_up, f_conv_w[0], f_conv_b[0], f_w_down, 0, seq)

    gcols = 2 * N_HEADS * HEAD_DIM + N_HEADS * V_DIM
    nq1 = N_HEADS * HEAD_DIM
    flags1 = _tile_flags(b_w_qkv.shape[2], t["tn_qkv"],
                         [(g * gcols, g * gcols + 2 * nq1) for g in range(len(SW_DILATIONS))])
    qkv1 = _qkv_proj(h, ln_mix[1], b_w_qkv, _dil_col_gain(b_q_norm[0], b_k_norm[0]), flags1, SW_DILATIONS)
    dil_bias = _dil_bias_table(rel_bias)
    outs, lses = [], []
    for g in range(len(SW_DILATIONS)):
        o, lse = _dilated_group_attention(qkv1[g], dil_bias, g, batch, seq)
        outs.append(o)
        lses.append(lse)
    h = _attn_out_proj(outs, lses, b_w_o, h)
    h = _conv_ffn(h, ln_ffn[1], f_w_up, f_conv_w[1], f_conv_b[1], f_w_down, 1, seq)
    return h.reshape(batch, seq, d)
```

```python
import functools
import math

import jax
import jax.numpy as jnp
from jax import lax
from jax.experimental import pallas as pl
from jax.experimental.pallas import tpu as pltpu

F32 = jnp.float32
BF16 = jnp.bfloat16

N_HEADS = 8
HEAD_DIM = 128
V_DIM = 2 * HEAD_DIM
SW_DILATIONS = (1, 4, 16)
SW_STEPS = 128
NUM_BUCKETS = 32
MAX_DISTANCE = 2048
EPS = 1e-6
NEG = -1e30

LANES = 128
COL_TILE = 256
ATTN_TILE = 256
CONV_HALO = 16
VMEM_LIMIT = 60 * 1024 * 1024


def _tiles():
    return dict(tm=1024, tn_qkv=1024, qkv_rows_chunk=256, tn_up=512, tn_down=512,
                norm_chunk=256, tm_attn_out=512, attn_out_chunk=256)


def _params(*sem):
    return pltpu.CompilerParams(dimension_semantics=sem, vmem_limit_bytes=VMEM_LIMIT)


def _rmsnorm_rows(x_ref, g_ref, out_ref, row0, rows, chunk):
    g = g_ref[...]
    for r in range(0, rows, chunk):
        x = x_ref[r:r + chunk, :]
        rs = lax.rsqrt(jnp.mean(x * x, axis=-1, keepdims=True) + EPS)
        out_ref[row0 + r:row0 + r + chunk, :] = (x * rs * g).astype(BF16)


def _t5_bucket(dist):
    max_exact = NUM_BUCKETS // 2
    n = jnp.maximum(dist, 0)
    nf = jnp.maximum(n, 1).astype(F32)
    large = max_exact + (jnp.log(nf / max_exact) / math.log(MAX_DISTANCE / max_exact)
                         * (NUM_BUCKETS - max_exact)).astype(jnp.int32)
    large = jnp.minimum(large, NUM_BUCKETS - 1)
    return jnp.where(n < max_exact, n, large)


def _bucket_range(dmin, dmax):
    def bucket(n):
        n = max(n, 0)
        if n < NUM_BUCKETS // 2:
            return n
        half = NUM_BUCKETS // 2
        return min(half + int(math.log(n / half) / math.log(MAX_DISTANCE / half) * half), NUM_BUCKETS - 1)
    return max(bucket(dmin) - 1, 0), min(bucket(dmax) + 1, NUM_BUCKETS - 1)


def _bias_lookup(rb_ref, bucket, h, bmin, bmax):
    val = jnp.full(bucket.shape, rb_ref[bmin, h], F32)
    for b in range(bmin + 1, bmax + 1):
        val = jnp.where(bucket == b, rb_ref[b, h], val)
    return val


def _diff_bias_kernel(rb_ref, o_ref):
    h = pl.program_id(0)
    t = ATTN_TILE
    delta = (lax.broadcasted_iota(jnp.int32, (t, t), 0) - lax.broadcasted_iota(jnp.int32, (t, t), 1))
    for d in range(o_ref.shape[1]):
        dist = d * t + delta
        val = _bias_lookup(rb_ref, _t5_bucket(dist), h, *_bucket_range(d * t - (t - 1), d * t + (t - 1)))
        o_ref[0, d] = jnp.where(dist >= 0, val, NEG) if d == 0 else val


def _diff_bias_table(rel_bias, seq):
    nd = seq // ATTN_TILE
    return pl.pallas_call(
        _diff_bias_kernel,
        out_shape=jax.ShapeDtypeStruct((N_HEADS, nd, ATTN_TILE, ATTN_TILE), F32),
        grid=(N_HEADS,),
        in_specs=[pl.BlockSpec(memory_space=pltpu.SMEM)],
        out_specs=pl.BlockSpec((1, nd, ATTN_TILE, ATTN_TILE), lambda h: (h, 0, 0, 0)),
        compiler_params=_params("parallel"),
        name="diff_bias_table",
    )(rel_bias)


def _dil_bias_kernel(rb_ref, o_ref):
    h = pl.program_id(0)
    v = pl.program_id(1)
    a = lax.broadcasted_iota(jnp.int32, (SW_STEPS, 2 * SW_STEPS), 0)
    b = lax.broadcasted_iota(jnp.int32, (SW_STEPS, 2 * SW_STEPS), 1)
    dist_sub = a + SW_STEPS * (1 - v) - b
    in_window = (dist_sub >= 0) & (dist_sub <= SW_STEPS)
    for g, dil in enumerate(SW_DILATIONS):
        dist = jnp.clip(dist_sub, 0, SW_STEPS) * dil
        val = _bias_lookup(rb_ref, _t5_bucket(dist), h, *_bucket_range(0, SW_STEPS * dil))
        o_ref[g, 0, 0] = jnp.where(in_window, val, NEG)


def _dil_bias_table(rel_bias):
    ng = len(SW_DILATIONS)
    return pl.pallas_call(
        _dil_bias_kernel,
        out_shape=jax.ShapeDtypeStruct((ng, N_HEADS, 2, SW_STEPS, 2 * SW_STEPS), F32),
        grid=(N_HEADS, 2),
        in_specs=[pl.BlockSpec(memory_space=pltpu.SMEM)],
        out_specs=pl.BlockSpec((ng, 1, 1, SW_STEPS, 2 * SW_STEPS), lambda h, v: (0, h, v, 0, 0)),
        compiler_params=_params("parallel", "parallel"),
        name="dil_bias_table",
    )(rel_bias)


def _qkv_kernel(flag_ref, x_ref, g_ref, w_ref, cg_ref, *refs, tm, tn, chunk, dils, steps_per_group):
    o_refs = refs[:len(dils)]
    xn_ref, acc_ref = refs[len(dils):]
    j = pl.program_id(1)

    @pl.when(j == 0)
    def _():
        _rmsnorm_rows(x_ref, g_ref, xn_ref, 0, tm, chunk)

    normed = flag_ref[j] != 0
    per_tile = COL_TILE // LANES
    rows_chunk = acc_ref.shape[1]

    def finish(o_ref, dil, acc, r0):
        n_rows = rows_chunk // dil
        out_rows = slice(r0 // dil, r0 // dil + n_rows)
        for c in range(tn // LANES):
            y_all = acc[:, c * LANES:(c + 1) * LANES]
            if dil > 1:
                acc_ref[c] = y_all
            for r in range(dil):
                y = y_all if dil == 1 else acc_ref[c, pl.ds(r, n_rows, stride=dil), :]
                rs = lax.rsqrt(jnp.mean(y * y, axis=-1, keepdims=True) + EPS)
                scale = jnp.where(normed, rs, 1.0)
                lo = r * COL_TILE + (c % per_tile) * LANES
                o_ref[c // per_tile, out_rows, lo:lo + LANES] = (
                    y * scale * cg_ref[:, c * LANES:(c + 1) * LANES]).astype(BF16)

    def project_group(o_ref, dil):
        wb = w_ref[...].astype(BF16)
        starts = list(range(0, tm, rows_chunk))

        def project(r0):
            return jnp.dot(xn_ref[r0:r0 + rows_chunk, :], wb, preferred_element_type=F32)

        acc_next = project(starts[0])
        for i, r0 in enumerate(starts):
            acc = acc_next
            if i + 1 < len(starts):
                acc_next = project(starts[i + 1])
            finish(o_ref, dil, acc, r0)

    if len(dils) == 1:
        project_group(o_refs[0], dils[0])
    else:
        for g, dil in enumerate(dils):
            pl.when(j // steps_per_group == g)(functools.partial(project_group, o_refs[g], dil))


def _qkv_proj(x2d, ln_g, w, col_gain, tile_flags, dils):
    t = _tiles()
    tm, tn = t["tm"], t["tn_qkv"]
    m, d = x2d.shape
    n = w.shape[2]
    ng = len(dils)
    steps_per_group = n // ng // tn
    group_tiles = n // ng // COL_TILE
    kern = functools.partial(_qkv_kernel, tm=tm, tn=tn, chunk=t["norm_chunk"], dils=dils,
                             steps_per_group=steps_per_group)

    def out_spec(g, dil):
        def index(i, j, f):
            return (jnp.clip(j - g * steps_per_group, 0, steps_per_group - 1), i, 0)
        return pl.BlockSpec((tn // COL_TILE, tm // dil, dil * COL_TILE), index)

    return pl.pallas_call(
        kern,
        out_shape=[jax.ShapeDtypeStruct((group_tiles, m // dil, dil * COL_TILE), BF16) for dil in dils],
        grid_spec=pltpu.PrefetchScalarGridSpec(
            num_scalar_prefetch=1,
            grid=(m // tm, n // tn),
            in_specs=[
                pl.BlockSpec((tm, d), lambda i, j, f: (i, 0)),
                pl.BlockSpec((1, d), lambda i, j, f: (0, 0)),
                pl.BlockSpec((None, d, tn), lambda i, j, f: (0, 0, j)),
                pl.BlockSpec((1, tn), lambda i, j, f: (0, j)),
            ],
            out_specs=[out_spec(g, dil) for g, dil in enumerate(dils)],
            scratch_shapes=[pltpu.VMEM((tm, d), BF16),
                            pltpu.VMEM((tn // LANES, t["qkv_rows_chunk"], LANES), F32)],
        ),
        compiler_params=_params("parallel", "arbitrary"),
        name="qkv_proj",
    )(tile_flags, x2d, ln_g.reshape(1, d), w, col_gain.reshape(1, n))


def _diff_attn_kernel(q_ref, k_ref, v_ref, bias_ref, lam_ref, sg_ref, o_ref, *, lam_init, seq):
    t = ATTN_TILE
    lam = lam_ref[...]
    lam_full = (jnp.exp(jnp.sum(lam[0:1] * lam[1:2], axis=-1, keepdims=True))
                - jnp.exp(jnp.sum(lam[2:3] * lam[3:4], axis=-1, keepdims=True)) + lam_init)
    gain = sg_ref[...] * (1.0 - lam_init)

    def logits(c, mp):
        width = (c + 1) * t
        lo = mp * HEAD_DIM
        bias = jnp.concatenate([bias_ref[0, c - ki] for ki in range(c + 1)], axis=1)
        return lax.dot_general(q_ref[0, c * t:(c + 1) * t, lo:lo + HEAD_DIM], k_ref[0, 0:width, lo:lo + HEAD_DIM],
                               (((1,), (1,)), ((), ())), preferred_element_type=F32) + bias

    def attend(c, s):
        p = jnp.exp(s - jnp.max(s, axis=-1, keepdims=True))
        den = jnp.sum(p, axis=-1, keepdims=True)
        return jnp.dot(p.astype(BF16), v_ref[0, 0:(c + 1) * t, :], preferred_element_type=F32) / den

    units = [(c, mp) for c in reversed(range(seq // t)) for mp in range(2)]
    s_next = logits(*units[0])
    maps = {}
    for i, (c, mp) in enumerate(units):
        s_cur = s_next
        if i + 1 < len(units):
            s_next = logits(*units[i + 1])
        maps[mp] = attend(c, s_cur)
        if mp == 1:
            o = maps[0] - lam_full * maps[1]
            rs = lax.rsqrt(jnp.mean(o * o, axis=-1, keepdims=True) + EPS)
            o_ref[c * t:(c + 1) * t, :] = (o * rs * gain).astype(BF16)


def _diff_attention(qkv, bias_tab, lam, subln_g, batch, seq, lam_init):
    t = ATTN_TILE
    nq = seq // t
    m = batch * seq
    kern = functools.partial(_diff_attn_kernel, lam_init=lam_init, seq=seq)
    return pl.pallas_call(
        kern,
        out_shape=jax.ShapeDtypeStruct((m, N_HEADS * V_DIM), BF16),
        grid=(N_HEADS, batch),
        in_specs=[
            pl.BlockSpec((1, seq, COL_TILE), lambda h, b: (h, b, 0)),
            pl.BlockSpec((1, seq, COL_TILE), lambda h, b: (N_HEADS + h, b, 0)),
            pl.BlockSpec((1, seq, COL_TILE), lambda h, b: (2 * N_HEADS + h, b, 0)),
            pl.BlockSpec((1, nq, t, t), lambda h, b: (h, 0, 0, 0)),
            pl.BlockSpec((4, HEAD_DIM), lambda h, b: (0, 0)),
            pl.BlockSpec((1, V_DIM), lambda h, b: (0, 0)),
        ],
        out_specs=pl.BlockSpec((seq, V_DIM), lambda h, b: (b, h)),
        compiler_params=_params("parallel", "parallel"),
        name="diff_attention",
    )(qkv, qkv, qkv, bias_tab, lam, subln_g.reshape(1, V_DIM))


def _dil_attn_kernel(q_ref, k_ref, v_ref, bias_ref, o_ref, lse_ref, *scratch, sub_len, rows, dil, rps):
    blk = SW_STEPS
    c = pl.program_id(2)
    nblk = rows // blk
    single = sub_len == blk
    lane = lax.broadcasted_iota(jnp.int32, (blk, LANES), 1)

    def window(n):
        ng = c * nblk + n
        if single:
            return 0, 1, blk
        return pl.multiple_of(jnp.maximum(ng - 1, 0) * blk, blk), (ng == 0).astype(jnp.int32), 2 * blk

    def logits(rr, n, h):
        start, variant, width = window(n)
        tile, lo = h // 2, rr * COL_TILE + (h % 2) * HEAD_DIM
        qt = q_ref[tile, n * blk:(n + 1) * blk, lo:lo + HEAD_DIM]
        kw = k_ref[tile, pl.ds(start, width), lo:lo + HEAD_DIM]
        bias = bias_ref[0, h, variant][:, :width]
        return lax.dot_general(qt, kw, (((1,), (1,)), ((), ())), preferred_element_type=F32) + bias

    def attend(rr, n, h, s):
        start, _, width = window(n)
        mx = jnp.max(s, axis=-1, keepdims=True)
        p = jnp.exp(s - mx)
        den = jnp.sum(p, axis=-1, keepdims=True)
        vw = v_ref[h, pl.ds(start, width), rr * COL_TILE:(rr + 1) * COL_TILE]
        o = jnp.dot(p.astype(BF16), vw, preferred_element_type=F32) / den
        return o, mx + jnp.log(den)

    units = [(rr, n, h) for rr in range(rps) for n in range(nblk) for h in range(N_HEADS)]
    ahead = 2 if nblk == 1 else 1
    queue = [logits(*u) for u in units[:ahead]]
    lse_tile = None
    for i, (rr, n, h) in enumerate(units):
        s = queue.pop(0)
        if i + ahead < len(units):
            queue.append(logits(*units[i + ahead]))
        o, lse = attend(rr, n, h, s)
        if dil == 1:
            out_rows = slice(n * blk, (n + 1) * blk)
            o_ref[out_rows, h * V_DIM:(h + 1) * V_DIM] = o.astype(BF16)
        else:
            r = pl.program_id(1) * rps + rr
            out_rows = pl.ds(r + (c * nblk + n) * (blk * dil), blk, stride=dil)
            for part in range(V_DIM // LANES):
                scratch[0][h * (V_DIM // LANES) + part, out_rows, :] = o[:, part * LANES:(part + 1) * LANES]
        lse_tile = jnp.where(lane == h, lse, jnp.zeros((blk, LANES), F32) if h == 0 else lse_tile)
        if h == N_HEADS - 1:
            (lse_ref if dil == 1 else scratch[1])[out_rows, :] = lse_tile

    if dil > 1:
        @pl.when((pl.program_id(1) == pl.num_programs(1) - 1) & (c == pl.num_programs(2) - 1))
        def _():
            for slab in range(scratch[0].shape[0]):
                o_ref[:, slab * LANES:(slab + 1) * LANES] = scratch[0][slab].astype(BF16)
            lse_ref[...] = scratch[1][...]


def _dilated_group_attention(qkv, bias_tab, g, batch, seq):
    dil = SW_DILATIONS[g]
    sub_len = seq // dil
    rows = min(sub_len, 512)
    nc = sub_len // rows
    rps = max(1, 512 // sub_len)
    m = batch * seq
    width = N_HEADS * V_DIM
    kern = functools.partial(_dil_attn_kernel, sub_len=sub_len, rows=rows, dil=dil, rps=rps)
    if dil == 1:
        out_specs = (pl.BlockSpec((rows, width), lambda b, r, c: (b * nc + c, 0)),
                     pl.BlockSpec((rows, LANES), lambda b, r, c: (b * nc + c, 0)))
        scratch, sem = [], ("parallel", "parallel", "parallel")
    else:
        out_specs = (pl.BlockSpec((seq, width), lambda b, r, c: (b, 0)),
                     pl.BlockSpec((seq, LANES), lambda b, r, c: (b, 0)))
        scratch = [pltpu.VMEM((width // LANES, seq, LANES), F32), pltpu.VMEM((seq, LANES), F32)]
        sem = ("parallel", "arbitrary", "arbitrary")
    return pl.pallas_call(
        kern,
        out_shape=(jax.ShapeDtypeStruct((m, width), BF16), jax.ShapeDtypeStruct((m, LANES), F32)),
        grid=(batch, dil // rps, nc),
        in_specs=[
            pl.BlockSpec((4, rows, rps * COL_TILE), lambda b, r, c: (0, b * nc + c, r)),
            pl.BlockSpec((4, sub_len, rps * COL_TILE), lambda b, r, c: (1, b, r)),
            pl.BlockSpec((8, sub_len, rps * COL_TILE), lambda b, r, c: (1, b, r)),
            pl.BlockSpec((1, N_HEADS, 2, SW_STEPS, 2 * SW_STEPS), lambda b, r, c: (g, 0, 0, 0, 0)),
        ],
        out_specs=out_specs,
        scratch_shapes=scratch,
        compiler_params=_params(*sem),
        name=f"dilated_attention_g{g}",
    )(qkv, qkv, qkv, bias_tab)


def _proj_res_kernel(a_ref, w_ref, r_ref, o_ref):
    o_ref[...] = r_ref[...] + jnp.dot(a_ref[...], w_ref[...].astype(BF16), preferred_element_type=F32)


def _proj_residual(a, w, layer, res, tn):
    tm = _tiles()["tm"]
    m, k = a.shape
    n = w.shape[2]
    return pl.pallas_call(
        _proj_res_kernel,
        out_shape=jax.ShapeDtypeStruct((m, n), F32),
        grid=(m // tm, n // tn),
        in_specs=[
            pl.BlockSpec((tm, k), lambda i, j: (i, 0)),
            pl.BlockSpec((None, k, tn), lambda i, j: (layer, 0, j)),
            pl.BlockSpec((tm, tn), lambda i, j: (i, j)),
        ],
        out_specs=pl.BlockSpec((tm, tn), lambda i, j: (i, j)),
        compiler_params=_params("parallel", "arbitrary"),
        name="proj_residual",
    )(a, w, res)


def _attn_out_kernel(*refs, n_groups, tm, chunk):
    o_refs = refs[:n_groups]
    l_refs = refs[n_groups:2 * n_groups] if n_groups > 1 else ()
    w_ref, r_ref, out_ref, wb_ref = refs[-4:]

    @pl.when(pl.program_id(0) == 0)
    def _():
        wb_ref[...] = w_ref[...].astype(BF16)

    def lhs(r):
        if n_groups == 1:
            return o_refs[0][r:r + chunk, :]
        ls = [l_ref[r:r + chunk, :] for l_ref in l_refs]
        mx = functools.reduce(jnp.maximum, ls)
        es = [jnp.exp(l - mx) for l in ls]
        den = functools.reduce(jnp.add, es)
        alphas = [e / den for e in es]
        heads = []
        for h in range(N_HEADS):
            cols = slice(h * V_DIM, (h + 1) * V_DIM)
            terms = [al[:, h:h + 1] * o_ref[r:r + chunk, cols].astype(F32) for al, o_ref in zip(alphas, o_refs)]
            heads.append(functools.reduce(jnp.add, terms).astype(BF16))
        return jnp.concatenate(heads, axis=1)

    starts = list(range(0, tm, chunk))
    a_next = lhs(starts[0])
    for i, r in enumerate(starts):
        a_cur = a_next
        if i + 1 < len(starts):
            a_next = lhs(starts[i + 1])
        out_ref[r:r + chunk, :] = r_ref[r:r + chunk, :] + jnp.dot(a_cur, wb_ref[...], preferred_element_type=F32)


def _attn_out_proj(outs, lses, w, res):
    t = _tiles()
    tm, chunk = t["tm_attn_out"], t["attn_out_chunk"]
    m, k = outs[0].shape
    n = w.shape[2]
    kern = functools.partial(_attn_out_kernel, n_groups=len(outs), tm=tm, chunk=chunk)
    o_spec = pl.BlockSpec((tm, k), lambda i: (i, 0))
    l_spec = pl.BlockSpec((tm, LANES), lambda i: (i, 0))
    return pl.pallas_call(
        kern,
        out_shape=jax.ShapeDtypeStruct((m, n), F32),
        grid=(m // tm,),
        in_specs=[o_spec] * len(outs) + [l_spec] * len(lses) + [
            pl.BlockSpec((None, k, n), lambda i: (0, 0, 0), pipeline_mode=pl.Buffered(1)),
            pl.BlockSpec((tm, n), lambda i: (i, 0))],
        out_specs=pl.BlockSpec((tm, n), lambda i: (i, 0)),
        scratch_shapes=[pltpu.VMEM((k, n), BF16)],
        compiler_params=_params("arbitrary"),
        name="attn_out_proj",
    )(*outs, *lses, w, res)


def _ffn_up_kernel(x_ref, halo_ref, g_ref, wg0_ref, wgn_ref, wv_ref, cwg_ref, cwv_ref, cbg_ref, cbv_ref, o_ref,
                   xn_ref, ug_ref, *, tm, nj, tiles_per_seq, chunk):
    i = pl.program_id(0)
    j = pl.program_id(1)
    halo = CONV_HALO

    @pl.when(j == 0)
    def _():
        _rmsnorm_rows(x_ref, g_ref, xn_ref, halo, tm, chunk)
        first = (i % tiles_per_seq) == 0

        @pl.when(first)
        def _():
            xn_ref[0:halo, :] = jnp.zeros((halo, xn_ref.shape[1]), BF16)

        @pl.when(jnp.logical_not(first))
        def _():
            _rmsnorm_rows(halo_ref, g_ref, xn_ref, 0, halo, halo)

        ug_ref[...] = jnp.dot(xn_ref[...], wg0_ref[...].astype(BF16), preferred_element_type=F32)

    def conv(u, cw_ref, cb_ref):
        cw = cw_ref[...]
        return (cw[0:1] * u[halo - 2:halo - 2 + tm] + cw[1:2] * u[halo - 1:halo - 1 + tm]
                + cw[2:3] * u[halo:halo + tm]) + cb_ref[...]

    def step(project_next_gate):
        xn = xn_ref[...]
        uv = jnp.dot(xn, wv_ref[...].astype(BF16), preferred_element_type=F32)
        gate = conv(ug_ref[...], cwg_ref, cbg_ref)
        if project_next_gate:
            ug_ref[...] = jnp.dot(xn, wgn_ref[...].astype(BF16), preferred_element_type=F32)
        val = conv(uv, cwv_ref, cbv_ref)
        o_ref[...] = (gate / (1.0 + jnp.exp(-gate)) * val).astype(BF16)

    pl.when(j < nj - 1)(functools.partial(step, True))
    pl.when(j == nj - 1)(functools.partial(step, False))


def _ffn_up(h2d, ln_g, w_up, layer, conv_w, conv_b, seq):
    t = _tiles()
    tm, tn = t["tm"], t["tn_up"]
    m, d = h2d.shape
    dff = w_up.shape[2] // 2
    nj = dff // tn
    halo_blocks = tm // CONV_HALO
    kern = functools.partial(_ffn_up_kernel, tm=tm, nj=nj, tiles_per_seq=seq // tm, chunk=t["norm_chunk"])
    return pl.pallas_call(
        kern,
        out_shape=jax.ShapeDtypeStruct((m, dff), BF16),
        grid=(m // tm, nj),
        in_specs=[
            pl.BlockSpec((tm, d), lambda i, j: (i, 0)),
            pl.BlockSpec((CONV_HALO, d), lambda i, j: (jnp.maximum(i * halo_blocks - 1, 0), 0)),
            pl.BlockSpec((1, d), lambda i, j: (0, 0)),
            pl.BlockSpec((None, d, tn), lambda i, j: (layer, 0, 0), pipeline_mode=pl.Buffered(1)),
            pl.BlockSpec((None, d, tn), lambda i, j: (layer, 0, jnp.minimum(j + 1, nj - 1))),
            pl.BlockSpec((None, d, tn), lambda i, j: (layer, 0, j + nj)),
            pl.BlockSpec((3, tn), lambda i, j: (0, j)),
            pl.BlockSpec((3, tn), lambda i, j: (0, j + nj)),
            pl.BlockSpec((1, tn), lambda i, j: (0, j)),
            pl.BlockSpec((1, tn), lambda i, j: (0, j + nj)),
        ],
        out_specs=pl.BlockSpec((tm, tn), lambda i, j: (i, j)),
        scratch_shapes=[pltpu.VMEM((tm + CONV_HALO, d), BF16), pltpu.VMEM((tm + CONV_HALO, tn), F32)],
        compiler_params=_params("parallel", "arbitrary"),
        name="ffn_up",
    )(h2d, h2d, ln_g.reshape(1, d), w_up, w_up, w_up, conv_w, conv_w, conv_b.reshape(1, -1), conv_b.reshape(1, -1))


def _conv_ffn(h2d, ln_g, w_up, conv_w, conv_b, w_down, layer, seq):
    act = _ffn_up(h2d, ln_g, w_up, layer, conv_w, conv_b, seq)
    return _proj_residual(act, w_down, layer, h2d, _tiles()["tn_down"])


def _diff_col_gain(q_gain, k_gain):
    scale = HEAD_DIM ** -0.5
    nqk = 2 * N_HEADS
    gain = jnp.concatenate([jnp.tile(q_gain * scale, nqk), jnp.tile(k_gain, nqk),
                            jnp.ones((N_HEADS * V_DIM,), F32)])
    return gain


def _dil_col_gain(q_gain, k_gain):
    scale = HEAD_DIM ** -0.5
    parts = []
    for g in range(len(SW_DILATIONS)):
        parts += [jnp.tile(q_gain[g] * scale, N_HEADS), jnp.tile(k_gain[g], N_HEADS),
                  jnp.ones((N_HEADS * V_DIM,), F32)]
    return jnp.concatenate(parts)


def _tile_flags(n_cols, tn, normed_ranges):
    flags = []
    for j in range(n_cols // tn):
        lo = j * tn
        flags.append(int(any(a <= lo < b for a, b in normed_ranges)))
    return jnp.asarray(flags, jnp.int32)


def kernel(x, rel_bias, ln_mix, ln_ffn, a_w_qkv, a_q_norm, a_k_norm, a_lambda, a_subln, a_w_o,
           b_w_qkv, b_q_norm, b_k_norm, b_w_o, f_w_up, f_conv_w, f_conv_b, f_w_down):
    batch, seq, d = x.shape
    m = batch * seq
    t = _tiles()
    h = x.reshape(m, d)

    nqk = 2 * N_HEADS * HEAD_DIM
    flags0 = _tile_flags(a_w_qkv.shape[2], t["tn_qkv"], [(0, 2 * nqk)])
    (qkv0,) = _qkv_proj(h, ln_mix[0], a_w_qkv, _diff_col_gain(a_q_norm[0], a_k_norm[0]), flags0, (1,))
    lam_init = 0.8 - 0.6 * math.exp(-0.3 * 0)
    attn0 = _diff_attention(qkv0, _diff_bias_table(rel_bias, seq), a_lambda[0], a_subln[0], batch, seq, lam_init)
    h = _attn_out_proj([attn0], [], a_w_o, h)
    h = _conv_ffn(h, ln_ffn[0], f_w_up, f_conv_w[0], f_conv_b[0], f_w_down, 0, seq)

    gcols = 2 * N_HEADS * HEAD_DIM + N_HEADS * V_DIM
    nq1 = N_HEADS * HEAD_DIM
    flags1 = _tile_flags(b_w_qkv.shape[2], t["tn_qkv"],
                         [(g * gcols, g * gcols + 2 * nq1) for g in range(len(SW_DILATIONS))])
    qkv1 = _qkv_proj(h, ln_mix[1], b_w_qkv, _dil_col_gain(b_q_norm[0], b_k_norm[0]), flags1, SW_DILATIONS)
    dil_bias = _dil_bias_table(rel_bias)
    outs, lses = [], []
    for g in range(len(SW_DILATIONS)):
        o, lse = _dilated_group_attention(qkv1[g], dil_bias, g, batch, seq)
        outs.append(o)
        lses.append(lse)
    h = _attn_out_proj(outs, lses, b_w_o, h)
    h = _conv_ffn(h, ln_ffn[1], f_w_up, f_conv_w[1], f_conv_b[1], f_w_down, 1, seq)
    return h.reshape(batch, seq, d)
```

```python
import functools
import math

import jax
import jax.numpy as jnp
from jax import lax
from jax.experimental import pallas as pl
from jax.experimental.pallas import tpu as pltpu

F32 = jnp.float32
BF16 = jnp.bfloat16

N_HEADS = 8
HEAD_DIM = 128
V_DIM = 2 * HEAD_DIM
SW_DILATIONS = (1, 4, 16)
SW_STEPS = 128
NUM_BUCKETS = 32
MAX_DISTANCE = 2048
EPS = 1e-6
NEG = -1e30

LANES = 128
COL_TILE = 256
ATTN_TILE = 256
CONV_HALO = 16
VMEM_LIMIT = 60 * 1024 * 1024


def _tiles():
    return dict(tm=1024, tn_qkv=1024, qkv_rows_chunk=256, tn_up=512, tn_down=512,
                norm_chunk=256, tm_attn_out=512, attn_out_chunk=256)


def _params(*sem):
    return pltpu.CompilerParams(dimension_semantics=sem, vmem_limit_bytes=VMEM_LIMIT)


def _rmsnorm_rows(x_ref, g_ref, out_ref, row0, rows, chunk):
    g = g_ref[...]
    for r in range(0, rows, chunk):
        x = x_ref[r:r + chunk, :]
        rs = lax.rsqrt(jnp.mean(x * x, axis=-1, keepdims=True) + EPS)
        out_ref[row0 + r:row0 + r + chunk, :] = (x * rs * g).astype(BF16)


def _t5_bucket(dist):
    max_exact = NUM_BUCKETS // 2
    n = jnp.maximum(dist, 0)
    nf = jnp.maximum(n, 1).astype(F32)
    large = max_exact + (jnp.log(nf / max_exact) / math.log(MAX_DISTANCE / max_exact)
                         * (NUM_BUCKETS - max_exact)).astype(jnp.int32)
    large = jnp.minimum(large, NUM_BUCKETS - 1)
    return jnp.where(n < max_exact, n, large)


def _bucket_range(dmin, dmax):
    def bucket(n):
        n = max(n, 0)
        if n < NUM_BUCKETS // 2:
            return n
        half = NUM_BUCKETS // 2
        return min(half + int(math.log(n / half) / math.log(MAX_DISTANCE / half) * half), NUM_BUCKETS - 1)
    return max(bucket(dmin) - 1, 0), min(bucket(dmax) + 1, NUM_BUCKETS - 1)


def _bias_lookup(rb_ref, bucket, h, bmin, bmax):
    val = jnp.full(bucket.shape, rb_ref[bmin, h], F32)
    for b in range(bmin + 1, bmax + 1):
        val = jnp.where(bucket == b, rb_ref[b, h], val)
    return val


def _diff_bias_kernel(rb_ref, o_ref):
    h = pl.program_id(0)
    t = ATTN_TILE
    delta = (lax.broadcasted_iota(jnp.int32, (t, t), 0) - lax.broadcasted_iota(jnp.int32, (t, t), 1))
    for d in range(o_ref.shape[1]):
        dist = d * t + delta
        val = _bias_lookup(rb_ref, _t5_bucket(dist), h, *_bucket_range(d * t - (t - 1), d * t + (t - 1)))
        o_ref[0, d] = jnp.where(dist >= 0, val, NEG) if d == 0 else val


def _diff_bias_table(rel_bias, seq):
    nd = seq // ATTN_TILE
    return pl.pallas_call(
        _diff_bias_kernel,
        out_shape=jax.ShapeDtypeStruct((N_HEADS, nd, ATTN_TILE, ATTN_TILE), F32),
        grid=(N_HEADS,),
        in_specs=[pl.BlockSpec(memory_space=pltpu.SMEM)],
        out_specs=pl.BlockSpec((1, nd, ATTN_TILE, ATTN_TILE), lambda h: (h, 0, 0, 0)),
        compiler_params=_params("parallel"),
        name="diff_bias_table",
    )(rel_bias)


def _dil_bias_kernel(rb_ref, o_ref):
    h = pl.program_id(0)
    v = pl.program_id(1)
    a = lax.broadcasted_iota(jnp.int32, (SW_STEPS, 2 * SW_STEPS), 0)
    b = lax.broadcasted_iota(jnp.int32, (SW_STEPS, 2 * SW_STEPS), 1)
    dist_sub = a + SW_STEPS * (1 - v) - b
    in_window = (dist_sub >= 0) & (dist_sub <= SW_STEPS)
    for g, dil in enumerate(SW_DILATIONS):
        dist = jnp.clip(dist_sub, 0, SW_STEPS) * dil
        val = _bias_lookup(rb_ref, _t5_bucket(dist), h, *_bucket_range(0, SW_STEPS * dil))
        o_ref[g, 0, 0] = jnp.where(in_window, val, NEG)


def _dil_bias_table(rel_bias):
    ng = len(SW_DILATIONS)
    return pl.pallas_call(
        _dil_bias_kernel,
        out_shape=jax.ShapeDtypeStruct((ng, N_HEADS, 2, SW_STEPS, 2 * SW_STEPS), F32),
        grid=(N_HEADS, 2),
        in_specs=[pl.BlockSpec(memory_space=pltpu.SMEM)],
        out_specs=pl.BlockSpec((ng, 1, 1, SW_STEPS, 2 * SW_STEPS), lambda h, v: (0, h, v, 0, 0)),
        compiler_params=_params("parallel", "parallel"),
        name="dil_bias_table",
    )(rel_bias)


def _qkv_kernel(flag_ref, x_ref, g_ref, w_ref, cg_ref, *refs, tm, tn, chunk, dils, steps_per_group):
    o_refs = refs[:len(dils)]
    xn_ref, acc_ref = refs[len(dils):]
    j = pl.program_id(1)

    @pl.when(j == 0)
    def _():
        _rmsnorm_rows(x_ref, g_ref, xn_ref, 0, tm, chunk)

    normed = flag_ref[j] != 0
    per_tile = COL_TILE // LANES
    rows_chunk = acc_ref.shape[1]

    def finish(o_ref, dil, acc, r0):
        n_rows = rows_chunk // dil
        out_rows = slice(r0 // dil, r0 // dil + n_rows)
        for c in range(tn // LANES):
            y_all = acc[:, c * LANES:(c + 1) * LANES]
            if dil > 1:
                acc_ref[c] = y_all
            for r in range(dil):
                y = y_all if dil == 1 else acc_ref[c, pl.ds(r, n_rows, stride=dil), :]
                rs = lax.rsqrt(jnp.mean(y * y, axis=-1, keepdims=True) + EPS)
                scale = jnp.where(normed, rs, 1.0)
                lo = r * COL_TILE + (c % per_tile) * LANES
                o_ref[c // per_tile, out_rows, lo:lo + LANES] = (
                    y * scale * cg_ref[:, c * LANES:(c + 1) * LANES]).astype(BF16)

    def project_group(o_ref, dil):
        wb = w_ref[...].astype(BF16)
        starts = list(range(0, tm, rows_chunk))

        def project(r0):
            return jnp.dot(xn_ref[r0:r0 + rows_chunk, :], wb, preferred_element_type=F32)

        acc_next = project(starts[0])
        for i, r0 in enumerate(starts):
            acc = acc_next
            if i + 1 < len(starts):
                acc_next = project(starts[i + 1])
            finish(o_ref, dil, acc, r0)

    if len(dils) == 1:
        project_group(o_refs[0], dils[0])
    else:
        for g, dil in enumerate(dils):
            pl.when(j // steps_per_group == g)(functools.partial(project_group, o_refs[g], dil))


def _qkv_proj(x2d, ln_g, w, col_gain, tile_flags, dils):
    t = _tiles()
    tm, tn = t["tm"], t["tn_qkv"]
    m, d = x2d.shape
    n = w.shape[2]
    ng = len(dils)
    steps_per_group = n // ng // tn
    group_tiles = n // ng // COL_TILE
    kern = functools.partial(_qkv_kernel, tm=tm, tn=tn, chunk=t["norm_chunk"], dils=dils,
                             steps_per_group=steps_per_group)

    def out_spec(g, dil):
        def index(i, j, f):
            return (jnp.clip(j - g * steps_per_group, 0, steps_per_group - 1), i, 0)
        return pl.BlockSpec((tn // COL_TILE, tm // dil, dil * COL_TILE), index)

    return pl.pallas_call(
        kern,
        out_shape=[jax.ShapeDtypeStruct((group_tiles, m // dil, dil * COL_TILE), BF16) for dil in dils],
        grid_spec=pltpu.PrefetchScalarGridSpec(
            num_scalar_prefetch=1,
            grid=(m // tm, n // tn),
            in_specs=[
                pl.BlockSpec((tm, d), lambda i, j, f: (i, 0)),
                pl.BlockSpec((1, d), lambda i, j, f: (0, 0)),
                pl.BlockSpec((None, d, tn), lambda i, j, f: (0, 0, j)),
                pl.BlockSpec((1, tn), lambda i, j, f: (0, j)),
            ],
            out_specs=[out_spec(g, dil) for g, dil in enumerate(dils)],
            scratch_shapes=[pltpu.VMEM((tm, d), BF16),
                            pltpu.VMEM((tn // LANES, t["qkv_rows_chunk"], LANES), F32)],
        ),
        compiler_params=_params("parallel", "arbitrary"),
        name="qkv_proj",
    )(tile_flags, x2d, ln_g.reshape(1, d), w, col_gain.reshape(1, n))


def _diff_attn_kernel(q_ref, k_ref, v_ref, bias_ref, lam_ref, sg_ref, wsrc_ref, o_ref, wdst_ref, *, lam_init, seq):
    wdst_ref[...] = wsrc_ref[...].astype(BF16)
    t = ATTN_TILE
    lam = lam_ref[...]
    lam_full = (jnp.exp(jnp.sum(lam[0:1] * lam[1:2], axis=-1, keepdims=True))
                - jnp.exp(jnp.sum(lam[2:3] * lam[3:4], axis=-1, keepdims=True)) + lam_init)
    gain = sg_ref[...] * (1.0 - lam_init)

    def logits(c, mp):
        width = (c + 1) * t
        lo = mp * HEAD_DIM
        bias = jnp.concatenate([bias_ref[0, c - ki] for ki in range(c + 1)], axis=1)
        return lax.dot_general(q_ref[0, c * t:(c + 1) * t, lo:lo + HEAD_DIM], k_ref[0, 0:width, lo:lo + HEAD_DIM],
                               (((1,), (1,)), ((), ())), preferred_element_type=F32) + bias

    def attend(c, s):
        p = jnp.exp(s - jnp.max(s, axis=-1, keepdims=True))
        den = jnp.sum(p, axis=-1, keepdims=True)
        return jnp.dot(p.astype(BF16), v_ref[0, 0:(c + 1) * t, :], preferred_element_type=F32) / den

    units = [(c, mp) for c in reversed(range(seq // t)) for mp in range(2)]
    s_next = logits(*units[0])
    maps = {}
    for i, (c, mp) in enumerate(units):
        s_cur = s_next
        if i + 1 < len(units):
            s_next = logits(*units[i + 1])
        maps[mp] = attend(c, s_cur)
        if mp == 1:
            o = maps[0] - lam_full * maps[1]
            rs = lax.rsqrt(jnp.mean(o * o, axis=-1, keepdims=True) + EPS)
            o_ref[c * t:(c + 1) * t, :] = (o * rs * gain).astype(BF16)


def _cast_side_job(w, layer, n_steps, step_of):
    _, rows, cols = w.shape
    slab = rows // n_steps
    in_spec = pl.BlockSpec((None, slab, cols), lambda *g: (layer, step_of(*g), 0))
    out_spec = pl.BlockSpec((slab, cols), lambda *g: (step_of(*g), 0))
    return in_spec, out_spec, jax.ShapeDtypeStruct((rows, cols), BF16)


def _diff_attention(qkv, bias_tab, lam, subln_g, batch, seq, lam_init, w_cast, w_cast_layer):
    t = ATTN_TILE
    nq = seq // t
    m = batch * seq
    kern = functools.partial(_diff_attn_kernel, lam_init=lam_init, seq=seq)
    w_in, w_out, w_shape = _cast_side_job(w_cast, w_cast_layer, N_HEADS * batch, lambda h, b: h * batch + b)
    return pl.pallas_call(
        kern,
        out_shape=(jax.ShapeDtypeStruct((m, N_HEADS * V_DIM), BF16), w_shape),
        grid=(N_HEADS, batch),
        in_specs=[
            pl.BlockSpec((1, seq, COL_TILE), lambda h, b: (h, b, 0)),
            pl.BlockSpec((1, seq, COL_TILE), lambda h, b: (N_HEADS + h, b, 0)),
            pl.BlockSpec((1, seq, COL_TILE), lambda h, b: (2 * N_HEADS + h, b, 0)),
            pl.BlockSpec((1, nq, t, t), lambda h, b: (h, 0, 0, 0)),
            pl.BlockSpec((4, HEAD_DIM), lambda h, b: (0, 0)),
            pl.BlockSpec((1, V_DIM), lambda h, b: (0, 0)),
            w_in,
        ],
        out_specs=(pl.BlockSpec((seq, V_DIM), lambda h, b: (b, h)), w_out),
        compiler_params=_params("parallel", "parallel"),
        name="diff_attention",
    )(qkv, qkv, qkv, bias_tab, lam, subln_g.reshape(1, V_DIM), w_cast)


def _dil_attn_kernel(q_ref, k_ref, v_ref, bias_ref, *refs, sub_len, rows, dil, rps, side_cast):
    if side_cast:
        wsrc_ref, o_ref, lse_ref, wdst_ref, *scratch = refs
        wdst_ref[...] = wsrc_ref[...].astype(BF16)
    else:
        o_ref, lse_ref, *scratch = refs
    blk = SW_STEPS
    c = pl.program_id(2)
    nblk = rows // blk
    single = sub_len == blk
    lane = lax.broadcasted_iota(jnp.int32, (blk, LANES), 1)

    def window(n):
        ng = c * nblk + n
        if single:
            return 0, 1, blk
        return pl.multiple_of(jnp.maximum(ng - 1, 0) * blk, blk), (ng == 0).astype(jnp.int32), 2 * blk

    def logits(rr, n, h):
        start, variant, width = window(n)
        tile, lo = h // 2, rr * COL_TILE + (h % 2) * HEAD_DIM
        qt = q_ref[tile, n * blk:(n + 1) * blk, lo:lo + HEAD_DIM]
        kw = k_ref[tile, pl.ds(start, width), lo:lo + HEAD_DIM]
        bias = bias_ref[0, h, variant][:, :width]
        return lax.dot_general(qt, kw, (((1,), (1,)), ((), ())), preferred_element_type=F32) + bias

    def attend(rr, n, h, s):
        start, _, width = window(n)
        mx = jnp.max(s, axis=-1, keepdims=True)
        p = jnp.exp(s - mx)
        den = jnp.sum(p, axis=-1, keepdims=True)
        vw = v_ref[h, pl.ds(start, width), rr * COL_TILE:(rr + 1) * COL_TILE]
        o = jnp.dot(p.astype(BF16), vw, preferred_element_type=F32) / den
        return o, mx + jnp.log(den)

    units = [(rr, n, h) for rr in range(rps) for n in range(nblk) for h in range(N_HEADS)]
    ahead = 2 if nblk == 1 else 1
    queue = [logits(*u) for u in units[:ahead]]
    lse_tile = None
    for i, (rr, n, h) in enumerate(units):
        s = queue.pop(0)
        if i + ahead < len(units):
            queue.append(logits(*units[i + ahead]))
        o, lse = attend(rr, n, h, s)
        if dil == 1:
            out_rows = slice(n * blk, (n + 1) * blk)
            o_ref[out_rows, h * V_DIM:(h + 1) * V_DIM] = o.astype(BF16)
        else:
            r = pl.program_id(1) * rps + rr
            out_rows = pl.ds(r + (c * nblk + n) * (blk * dil), blk, stride=dil)
            for part in range(V_DIM // LANES):
                scratch[0][h * (V_DIM // LANES) + part, out_rows, :] = o[:, part * LANES:(part + 1) * LANES]
        lse_tile = jnp.where(lane == h, lse, jnp.zeros((blk, LANES), F32) if h == 0 else lse_tile)
        if h == N_HEADS - 1:
            (lse_ref if dil == 1 else scratch[1])[out_rows, :] = lse_tile

    if dil > 1:
        @pl.when((pl.program_id(1) == pl.num_programs(1) - 1) & (c == pl.num_programs(2) - 1))
        def _():
            for slab in range(scratch[0].shape[0]):
                o_ref[:, slab * LANES:(slab + 1) * LANES] = scratch[0][slab].astype(BF16)
            lse_ref[...] = scratch[1][...]


def _dilated_group_attention(qkv, bias_tab, g, batch, seq, w_cast=None, w_cast_layer=0):
    dil = SW_DILATIONS[g]
    sub_len = seq // dil
    rows = min(sub_len, 512)
    nc = sub_len // rows
    rps = max(1, 512 // sub_len)
    m = batch * seq
    width = N_HEADS * V_DIM
    kern = functools.partial(_dil_attn_kernel, sub_len=sub_len, rows=rows, dil=dil, rps=rps,
                             side_cast=w_cast is not None)
    if dil == 1:
        out_specs = [pl.BlockSpec((rows, width), lambda b, r, c: (b * nc + c, 0)),
                     pl.BlockSpec((rows, LANES), lambda b, r, c: (b * nc + c, 0))]
        scratch, sem = [], ("parallel", "parallel", "parallel")
    else:
        out_specs = [pl.BlockSpec((seq, width), lambda b, r, c: (b, 0)),
                     pl.BlockSpec((seq, LANES), lambda b, r, c: (b, 0))]
        scratch = [pltpu.VMEM((width // LANES, seq, LANES), F32), pltpu.VMEM((seq, LANES), F32)]
        sem = ("parallel", "arbitrary", "arbitrary")
    out_shape = [jax.ShapeDtypeStruct((m, width), BF16), jax.ShapeDtypeStruct((m, LANES), F32)]
    in_specs = [
        pl.BlockSpec((4, rows, rps * COL_TILE), lambda b, r, c: (0, b * nc + c, r)),
        pl.BlockSpec((4, sub_len, rps * COL_TILE), lambda b, r, c: (1, b, r)),
        pl.BlockSpec((8, sub_len, rps * COL_TILE), lambda b, r, c: (1, b, r)),
        pl.BlockSpec((1, N_HEADS, 2, SW_STEPS, 2 * SW_STEPS), lambda b, r, c: (g, 0, 0, 0, 0)),
    ]
    operands = [qkv, qkv, qkv, bias_tab]
    if w_cast is not None:
        n_r = dil // rps
        w_in, w_out, w_shape = _cast_side_job(w_cast, w_cast_layer, batch * n_r * nc,
                                              lambda b, r, c: (b * n_r + r) * nc + c)
        in_specs.append(w_in)
        out_specs.append(w_out)
        out_shape.append(w_shape)
        operands.append(w_cast)
    return pl.pallas_call(
        kern,
        out_shape=out_shape,
        grid=(batch, dil // rps, nc),
        in_specs=in_specs,
        out_specs=out_specs,
        scratch_shapes=scratch,
        compiler_params=_params(*sem),
        name=f"dilated_attention_g{g}",
    )(*operands)


def _proj_res_kernel(a_ref, w_ref, r_ref, o_ref):
    o_ref[...] = r_ref[...] + jnp.dot(a_ref[...], w_ref[...].astype(BF16), preferred_element_type=F32)


def _proj_residual(a, w, layer, res, tn):
    tm = _tiles()["tm"]
    m, k = a.shape
    n = w.shape[2]
    return pl.pallas_call(
        _proj_res_kernel,
        out_shape=jax.ShapeDtypeStruct((m, n), F32),
        grid=(m // tm, n // tn),
        in_specs=[
            pl.BlockSpec((tm, k), lambda i, j: (i, 0)),
            pl.BlockSpec((None, k, tn), lambda i, j: (layer, 0, j)),
            pl.BlockSpec((tm, tn), lambda i, j: (i, j)),
        ],
        out_specs=pl.BlockSpec((tm, tn), lambda i, j: (i, j)),
        compiler_params=_params("parallel", "arbitrary"),
        name="proj_residual",
    )(a, w, res)


def _attn_out_kernel(*refs, n_groups, tm, chunk):
    o_refs = refs[:n_groups]
    l_refs = refs[n_groups:2 * n_groups] if n_groups > 1 else ()
    w_ref, r_ref, out_ref, wb_ref = refs[-4:]

    @pl.when(pl.program_id(0) == 0)
    def _():
        wb_ref[...] = w_ref[...].astype(BF16)

    def lhs(r):
        if n_groups == 1:
            return o_refs[0][r:r + chunk, :]
        ls = [l_ref[r:r + chunk, :] for l_ref in l_refs]
        mx = functools.reduce(jnp.maximum, ls)
        es = [jnp.exp(l - mx) for l in ls]
        den = functools.reduce(jnp.add, es)
        alphas = [e / den for e in es]
        heads = []
        for h in range(N_HEADS):
            cols = slice(h * V_DIM, (h + 1) * V_DIM)
            terms = [al[:, h:h + 1] * o_ref[r:r + chunk, cols].astype(F32) for al, o_ref in zip(alphas, o_refs)]
            heads.append(functools.reduce(jnp.add, terms).astype(BF16))
        return jnp.concatenate(heads, axis=1)

    starts = list(range(0, tm, chunk))
    a_next = lhs(starts[0])
    for i, r in enumerate(starts):
        a_cur = a_next
        if i + 1 < len(starts):
            a_next = lhs(starts[i + 1])
        out_ref[r:r + chunk, :] = r_ref[r:r + chunk, :] + jnp.dot(a_cur, wb_ref[...], preferred_element_type=F32)


def _attn_out_proj(outs, lses, w, res):
    t = _tiles()
    tm, chunk = t["tm_attn_out"], t["attn_out_chunk"]
    m, k = outs[0].shape
    n = w.shape[2]
    kern = functools.partial(_attn_out_kernel, n_groups=len(outs), tm=tm, chunk=chunk)
    o_spec = pl.BlockSpec((tm, k), lambda i: (i, 0))
    l_spec = pl.BlockSpec((tm, LANES), lambda i: (i, 0))
    return pl.pallas_call(
        kern,
        out_shape=jax.ShapeDtypeStruct((m, n), F32),
        grid=(m // tm,),
        in_specs=[o_spec] * len(outs) + [l_spec] * len(lses) + [
            pl.BlockSpec((None, k, n), lambda i: (0, 0, 0), pipeline_mode=pl.Buffered(1)),
            pl.BlockSpec((tm, n), lambda i: (i, 0))],
        out_specs=pl.BlockSpec((tm, n), lambda i: (i, 0)),
        scratch_shapes=[pltpu.VMEM((k, n), BF16)],
        compiler_params=_params("arbitrary"),
        name="attn_out_proj",
    )(*outs, *lses, w, res)


def _ffn_up_kernel(x_ref, halo_ref, g_ref, wg0_ref, wgn_ref, wv_ref, cwg_ref, cwv_ref, cbg_ref, cbv_ref, wsrc_ref,
                   o_ref, wdst_ref, xn_ref, ug_ref, *, tm, nj, tiles_per_seq, chunk):
    wdst_ref[...] = wsrc_ref[...].astype(BF16)
    i = pl.program_id(0)
    j = pl.program_id(1)
    halo = CONV_HALO

    @pl.when(j == 0)
    def _():
        _rmsnorm_rows(x_ref, g_ref, xn_ref, halo, tm, chunk)
        first = (i % tiles_per_seq) == 0

        @pl.when(first)
        def _():
            xn_ref[0:halo, :] = jnp.zeros((halo, xn_ref.shape[1]), BF16)

        @pl.when(jnp.logical_not(first))
        def _():
            _rmsnorm_rows(halo_ref, g_ref, xn_ref, 0, halo, halo)

        ug_ref[...] = jnp.dot(xn_ref[...], wg0_ref[...].astype(BF16), preferred_element_type=F32)

    def conv(u, cw_ref, cb_ref):
        cw = cw_ref[...]
        return (cw[0:1] * u[halo - 2:halo - 2 + tm] + cw[1:2] * u[halo - 1:halo - 1 + tm]
                + cw[2:3] * u[halo:halo + tm]) + cb_ref[...]

    def step(project_next_gate):
        xn = xn_ref[...]
        uv = jnp.dot(xn, wv_ref[...].astype(BF16), preferred_element_type=F32)
        gate = conv(ug_ref[...], cwg_ref, cbg_ref)
        if project_next_gate:
            ug_ref[...] = jnp.dot(xn, wgn_ref[...].astype(BF16), preferred_element_type=F32)
        val = conv(uv, cwv_ref, cbv_ref)
        o_ref[...] = (gate / (1.0 + jnp.exp(-gate)) * val).astype(BF16)

    pl.when(j < nj - 1)(functools.partial(step, True))
    pl.when(j == nj - 1)(functools.partial(step, False))


def _ffn_up(h2d, ln_g, w_up, layer, conv_w, conv_b, seq, w_cast, w_cast_layer):
    t = _tiles()
    tm, tn = t["tm"], t["tn_up"]
    m, d = h2d.shape
    dff = w_up.shape[2] // 2
    nj = dff // tn
    halo_blocks = tm // CONV_HALO
    kern = functools.partial(_ffn_up_kernel, tm=tm, nj=nj, tiles_per_seq=seq // tm, chunk=t["norm_chunk"])
    w_in, w_out, w_shape = _cast_side_job(w_cast, w_cast_layer, (m // tm) * nj, lambda i, j: i * nj + j)
    return pl.pallas_call(
        kern,
        out_shape=(jax.ShapeDtypeStruct((m, dff), BF16), w_shape),
        grid=(m // tm, nj),
        in_specs=[
            pl.BlockSpec((tm, d), lambda i, j: (i, 0)),
            pl.BlockSpec((CONV_HALO, d), lambda i, j: (jnp.maximum(i * halo_blocks - 1, 0), 0)),
            pl.BlockSpec((1, d), lambda i, j: (0, 0)),
            pl.BlockSpec((None, d, tn), lambda i, j: (layer, 0, 0), pipeline_mode=pl.Buffered(1)),
            pl.BlockSpec((None, d, tn), lambda i, j: (layer, 0, jnp.minimum(j + 1, nj - 1))),
            pl.BlockSpec((None, d, tn), lambda i, j: (layer, 0, j + nj)),
            pl.BlockSpec((3, tn), lambda i, j: (0, j)),
            pl.BlockSpec((3, tn), lambda i, j: (0, j + nj)),
            pl.BlockSpec((1, tn), lambda i, j: (0, j)),
            pl.BlockSpec((1, tn), lambda i, j: (0, j + nj)),
            w_in,
        ],
        out_specs=(pl.BlockSpec((tm, tn), lambda i, j: (i, j)), w_out),
        scratch_shapes=[pltpu.VMEM((tm + CONV_HALO, d), BF16), pltpu.VMEM((tm + CONV_HALO, tn), F32)],
        compiler_params=_params("parallel", "arbitrary"),
        name="ffn_up",
    )(h2d, h2d, ln_g.reshape(1, d), w_up, w_up, w_up, conv_w, conv_w, conv_b.reshape(1, -1), conv_b.reshape(1, -1),
      w_cast)


def _conv_ffn(h2d, ln_g, w_up_bf16, conv_w, conv_b, w_down, layer, seq):
    act, w_down_bf16 = _ffn_up(h2d, ln_g, w_up_bf16[None], 0, conv_w, conv_b, seq, w_down, layer)
    return _proj_residual(act, w_down_bf16[None], 0, h2d, _tiles()["tn_down"])


def _diff_col_gain(q_gain, k_gain):
    scale = HEAD_DIM ** -0.5
    nqk = 2 * N_HEADS
    gain = jnp.concatenate([jnp.tile(q_gain * scale, nqk), jnp.tile(k_gain, nqk),
                            jnp.ones((N_HEADS * V_DIM,), F32)])
    return gain


def _dil_col_gain(q_gain, k_gain):
    scale = HEAD_DIM ** -0.5
    parts = []
    for g in range(len(SW_DILATIONS)):
        parts += [jnp.tile(q_gain[g] * scale, N_HEADS), jnp.tile(k_gain[g], N_HEADS),
                  jnp.ones((N_HEADS * V_DIM,), F32)]
    return jnp.concatenate(parts)


def _tile_flags(n_cols, tn, normed_ranges):
    flags = []
    for j in range(n_cols // tn):
        lo = j * tn
        flags.append(int(any(a <= lo < b for a, b in normed_ranges)))
    return jnp.asarray(flags, jnp.int32)


def kernel(x, rel_bias, ln_mix, ln_ffn, a_w_qkv, a_q_norm, a_k_norm, a_lambda, a_subln, a_w_o,
           b_w_qkv, b_q_norm, b_k_norm, b_w_o, f_w_up, f_conv_w, f_conv_b, f_w_down):
    batch, seq, d = x.shape
    m = batch * seq
    t = _tiles()
    h = x.reshape(m, d)

    nqk = 2 * N_HEADS * HEAD_DIM
    flags0 = _tile_flags(a_w_qkv.shape[2], t["tn_qkv"], [(0, 2 * nqk)])
    (qkv0,) = _qkv_proj(h, ln_mix[0], a_w_qkv, _diff_col_gain(a_q_norm[0], a_k_norm[0]), flags0, (1,))
    lam_init = 0.8 - 0.6 * math.exp(-0.3 * 0)
    attn0, w_up0 = _diff_attention(qkv0, _diff_bias_table(rel_bias, seq), a_lambda[0], a_subln[0], batch, seq,
                                   lam_init, f_w_up, 0)
    h = _attn_out_proj([attn0], [], a_w_o, h)
    h = _conv_ffn(h, ln_ffn[0], w_up0, f_conv_w[0], f_conv_b[0], f_w_down, 0, seq)

    gcols = 2 * N_HEADS * HEAD_DIM + N_HEADS * V_DIM
    nq1 = N_HEADS * HEAD_DIM
    flags1 = _tile_flags(b_w_qkv.shape[2], t["tn_qkv"],
                         [(g * gcols, g * gcols + 2 * nq1) for g in range(len(SW_DILATIONS))])
    qkv1 = _qkv_proj(h, ln_mix[1], b_w_qkv, _dil_col_gain(b_q_norm[0], b_k_norm[0]), flags1, SW_DILATIONS)
    dil_bias = _dil_bias_table(rel_bias)
    o0, lse0, w_up1 = _dilated_group_attention(qkv1[0], dil_bias, 0, batch, seq, f_w_up, 1)
    outs, lses = [o0], [lse0]
    for g in range(1, len(SW_DILATIONS)):
        o, lse = _dilated_group_attention(qkv1[g], dil_bias, g, batch, seq)
        outs.append(o)
        lses.append(lse)
    h = _attn_out_proj(outs, lses, b_w_o, h)
    h = _conv_ffn(h, ln_ffn[1], w_up1, f_conv_w[1], f_conv_b[1], f_w_down, 1, seq)
    return h.reshape(batch, seq, d)
```

```python
import functools
import math

import jax
import jax.numpy as jnp
from jax import lax
from jax.experimental import pallas as pl
from jax.experimental.pallas import tpu as pltpu

F32 = jnp.float32
BF16 = jnp.bfloat16

N_HEADS = 8
HEAD_DIM = 128
V_DIM = 2 * HEAD_DIM
SW_DILATIONS = (1, 4, 16)
SW_STEPS = 128
NUM_BUCKETS = 32
MAX_DISTANCE = 2048
EPS = 1e-6
NEG = -1e30

LANES = 128
COL_TILE = 256
ATTN_TILE = 256
CONV_HALO = 16
VMEM_LIMIT = 60 * 1024 * 1024


def _tiles():
    return dict(tm=1024, tn_qkv=1024, qkv_rows_chunk=256, tn_up=512, tn_down=512,
                norm_chunk=256, tm_attn_out=512, attn_out_chunk=256)


def _params(*sem):
    return pltpu.CompilerParams(dimension_semantics=sem, vmem_limit_bytes=VMEM_LIMIT)


def _rmsnorm_rows(x_ref, g_ref, out_ref, row0, rows, chunk):
    g = g_ref[...]
    for r in range(0, rows, chunk):
        x = x_ref[r:r + chunk, :]
        rs = lax.rsqrt(jnp.mean(x * x, axis=-1, keepdims=True) + EPS)
        out_ref[row0 + r:row0 + r + chunk, :] = (x * rs * g).astype(BF16)


def _t5_bucket(dist):
    max_exact = NUM_BUCKETS // 2
    n = jnp.maximum(dist, 0)
    nf = jnp.maximum(n, 1).astype(F32)
    large = max_exact + (jnp.log(nf / max_exact) / math.log(MAX_DISTANCE / max_exact)
                         * (NUM_BUCKETS - max_exact)).astype(jnp.int32)
    large = jnp.minimum(large, NUM_BUCKETS - 1)
    return jnp.where(n < max_exact, n, large)


def _bucket_range(dmin, dmax):
    def bucket(n):
        n = max(n, 0)
        if n < NUM_BUCKETS // 2:
            return n
        half = NUM_BUCKETS // 2
        return min(half + int(math.log(n / half) / math.log(MAX_DISTANCE / half) * half), NUM_BUCKETS - 1)
    return max(bucket(dmin) - 1, 0), min(bucket(dmax) + 1, NUM_BUCKETS - 1)


def _bias_lookup(rb_ref, bucket, h, bmin, bmax):
    val = jnp.full(bucket.shape, rb_ref[bmin, h], F32)
    for b in range(bmin + 1, bmax + 1):
        val = jnp.where(bucket == b, rb_ref[b, h], val)
    return val


def _diff_bias_kernel(rb_ref, o_ref):
    h = pl.program_id(0)
    t = ATTN_TILE
    delta = (lax.broadcasted_iota(jnp.int32, (t, t), 0) - lax.broadcasted_iota(jnp.int32, (t, t), 1))
    for d in range(o_ref.shape[1]):
        dist = d * t + delta
        val = _bias_lookup(rb_ref, _t5_bucket(dist), h, *_bucket_range(d * t - (t - 1), d * t + (t - 1)))
        o_ref[0, d] = jnp.where(dist >= 0, val, NEG) if d == 0 else val


def _diff_bias_table(rel_bias, seq):
    nd = seq // ATTN_TILE
    return pl.pallas_call(
        _diff_bias_kernel,
        out_shape=jax.ShapeDtypeStruct((N_HEADS, nd, ATTN_TILE, ATTN_TILE), F32),
        grid=(N_HEADS,),
        in_specs=[pl.BlockSpec(memory_space=pltpu.SMEM)],
        out_specs=pl.BlockSpec((1, nd, ATTN_TILE, ATTN_TILE), lambda h: (h, 0, 0, 0)),
        compiler_params=_params("parallel"),
        name="diff_bias_table",
    )(rel_bias)


def _dil_bias_kernel(rb_ref, o_ref):
    h = pl.program_id(0)
    v = pl.program_id(1)
    a = lax.broadcasted_iota(jnp.int32, (SW_STEPS, 2 * SW_STEPS), 0)
    b = lax.broadcasted_iota(jnp.int32, (SW_STEPS, 2 * SW_STEPS), 1)
    dist_sub = a + SW_STEPS * (1 - v) - b
    in_window = (dist_sub >= 0) & (dist_sub <= SW_STEPS)
    for g, dil in enumerate(SW_DILATIONS):
        dist = jnp.clip(dist_sub, 0, SW_STEPS) * dil
        val = _bias_lookup(rb_ref, _t5_bucket(dist), h, *_bucket_range(0, SW_STEPS * dil))
        o_ref[g, 0, 0] = jnp.where(in_window, val, NEG)


def _dil_bias_table(rel_bias):
    ng = len(SW_DILATIONS)
    return pl.pallas_call(
        _dil_bias_kernel,
        out_shape=jax.ShapeDtypeStruct((ng, N_HEADS, 2, SW_STEPS, 2 * SW_STEPS), F32),
        grid=(N_HEADS, 2),
        in_specs=[pl.BlockSpec(memory_space=pltpu.SMEM)],
        out_specs=pl.BlockSpec((ng, 1, 1, SW_STEPS, 2 * SW_STEPS), lambda h, v: (0, h, v, 0, 0)),
        compiler_params=_params("parallel", "parallel"),
        name="dil_bias_table",
    )(rel_bias)


def _qkv_kernel(flag_ref, x_ref, g_ref, w_ref, cg_ref, *refs, tm, tn, chunk, dils, steps_per_group):
    o_refs = refs[:len(dils)]
    xn_ref, acc_ref = refs[len(dils):]
    j = pl.program_id(1)

    @pl.when(j == 0)
    def _():
        _rmsnorm_rows(x_ref, g_ref, xn_ref, 0, tm, chunk)

    normed = flag_ref[j] != 0
    per_tile = COL_TILE // LANES
    rows_chunk = acc_ref.shape[1]

    def finish(o_ref, dil, acc, r0):
        n_rows = rows_chunk // dil
        out_rows = slice(r0 // dil, r0 // dil + n_rows)
        for c in range(tn // LANES):
            y_all = acc[:, c * LANES:(c + 1) * LANES]
            if dil > 1:
                acc_ref[c] = y_all
            for r in range(dil):
                y = y_all if dil == 1 else acc_ref[c, pl.ds(r, n_rows, stride=dil), :]
                rs = lax.rsqrt(jnp.mean(y * y, axis=-1, keepdims=True) + EPS)
                scale = jnp.where(normed, rs, 1.0)
                lo = r * COL_TILE + (c % per_tile) * LANES
                o_ref[c // per_tile, out_rows, lo:lo + LANES] = (
                    y * scale * cg_ref[:, c * LANES:(c + 1) * LANES]).astype(BF16)

    def project_group(o_ref, dil):
        wb = w_ref[...].astype(BF16)
        starts = list(range(0, tm, rows_chunk))

        def project(r0):
            return jnp.dot(xn_ref[r0:r0 + rows_chunk, :], wb, preferred_element_type=F32)

        acc_next = project(starts[0])
        for i, r0 in enumerate(starts):
            acc = acc_next
            if i + 1 < len(starts):
                acc_next = project(starts[i + 1])
            finish(o_ref, dil, acc, r0)

    if len(dils) == 1:
        project_group(o_refs[0], dils[0])
    else:
        for g, dil in enumerate(dils):
            pl.when(j // steps_per_group == g)(functools.partial(project_group, o_refs[g], dil))


def _qkv_proj(x2d, ln_g, w, col_gain, tile_flags, dils):
    t = _tiles()
    tm, tn = t["tm"], t["tn_qkv"]
    m, d = x2d.shape
    n = w.shape[2]
    ng = len(dils)
    steps_per_group = n // ng // tn
    group_tiles = n // ng // COL_TILE
    kern = functools.partial(_qkv_kernel, tm=tm, tn=tn, chunk=t["norm_chunk"], dils=dils,
                             steps_per_group=steps_per_group)

    def out_spec(g, dil):
        def index(i, j, f):
            return (jnp.clip(j - g * steps_per_group, 0, steps_per_group - 1), i, 0)
        return pl.BlockSpec((tn // COL_TILE, tm // dil, dil * COL_TILE), index)

    return pl.pallas_call(
        kern,
        out_shape=[jax.ShapeDtypeStruct((group_tiles, m // dil, dil * COL_TILE), BF16) for dil in dils],
        grid_spec=pltpu.PrefetchScalarGridSpec(
            num_scalar_prefetch=1,
            grid=(m // tm, n // tn),
            in_specs=[
                pl.BlockSpec((tm, d), lambda i, j, f: (i, 0)),
                pl.BlockSpec((1, d), lambda i, j, f: (0, 0)),
                pl.BlockSpec((None, d, tn), lambda i, j, f: (0, 0, j)),
                pl.BlockSpec((1, tn), lambda i, j, f: (0, j)),
            ],
            out_specs=[out_spec(g, dil) for g, dil in enumerate(dils)],
            scratch_shapes=[pltpu.VMEM((tm, d), BF16),
                            pltpu.VMEM((tn // LANES, t["qkv_rows_chunk"], LANES), F32)],
        ),
        compiler_params=_params("parallel", "arbitrary"),
        name="qkv_proj",
    )(tile_flags, x2d, ln_g.reshape(1, d), w, col_gain.reshape(1, n))


def _diff_attn_kernel(q_ref, k_ref, v_ref, bias_ref, lam_ref, sg_ref, wsrc_ref, o_ref, wdst_ref, *, lam_init, seq):
    wdst_ref[...] = wsrc_ref[...].astype(BF16)
    t = ATTN_TILE
    lam = lam_ref[...]
    lam_full = (jnp.exp(jnp.sum(lam[0:1] * lam[1:2], axis=-1, keepdims=True))
                - jnp.exp(jnp.sum(lam[2:3] * lam[3:4], axis=-1, keepdims=True)) + lam_init)
    gain = sg_ref[...] * (1.0 - lam_init)

    def logits(c, mp):
        width = (c + 1) * t
        lo = mp * HEAD_DIM
        bias = jnp.concatenate([bias_ref[0, c - ki] for ki in range(c + 1)], axis=1)
        return lax.dot_general(q_ref[0, c * t:(c + 1) * t, lo:lo + HEAD_DIM], k_ref[0, 0:width, lo:lo + HEAD_DIM],
                               (((1,), (1,)), ((), ())), preferred_element_type=F32) + bias

    def attend(c, s):
        p = jnp.exp(s - jnp.max(s, axis=-1, keepdims=True))
        den = jnp.sum(p, axis=-1, keepdims=True)
        return jnp.dot(p.astype(BF16), v_ref[0, 0:(c + 1) * t, :], preferred_element_type=F32) / den

    units = [(c, mp) for c in reversed(range(seq // t)) for mp in range(2)]
    s_next = logits(*units[0])
    maps = {}
    for i, (c, mp) in enumerate(units):
        s_cur = s_next
        if i + 1 < len(units):
            s_next = logits(*units[i + 1])
        maps[mp] = attend(c, s_cur)
        if mp == 1:
            o = maps[0] - lam_full * maps[1]
            rs = lax.rsqrt(jnp.mean(o * o, axis=-1, keepdims=True) + EPS)
            o_ref[c * t:(c + 1) * t, :] = (o * rs * gain).astype(BF16)


def _cast_side_job(w, layer, n_steps, step_of):
    _, rows, cols = w.shape
    packed_rows = 16
    slab = packed_rows * pl.cdiv(rows, packed_rows * n_steps)
    n_slabs = rows // slab
    assert n_slabs * slab == rows and n_slabs <= n_steps

    def slab_of(*g):
        return jnp.minimum(step_of(*g), n_slabs - 1)

    in_spec = pl.BlockSpec((None, slab, cols), lambda *g: (layer, slab_of(*g), 0))
    out_spec = pl.BlockSpec((slab, cols), lambda *g: (slab_of(*g), 0))
    return in_spec, out_spec, jax.ShapeDtypeStruct((rows, cols), BF16)


def _diff_attention(qkv, bias_tab, lam, subln_g, batch, seq, lam_init, w_cast, w_cast_layer):
    t = ATTN_TILE
    nq = seq // t
    m = batch * seq
    kern = functools.partial(_diff_attn_kernel, lam_init=lam_init, seq=seq)
    w_in, w_out, w_shape = _cast_side_job(w_cast, w_cast_layer, N_HEADS * batch, lambda h, b: h * batch + b)
    return pl.pallas_call(
        kern,
        out_shape=(jax.ShapeDtypeStruct((m, N_HEADS * V_DIM), BF16), w_shape),
        grid=(N_HEADS, batch),
        in_specs=[
            pl.BlockSpec((1, seq, COL_TILE), lambda h, b: (h, b, 0)),
            pl.BlockSpec((1, seq, COL_TILE), lambda h, b: (N_HEADS + h, b, 0)),
            pl.BlockSpec((1, seq, COL_TILE), lambda h, b: (2 * N_HEADS + h, b, 0)),
            pl.BlockSpec((1, nq, t, t), lambda h, b: (h, 0, 0, 0)),
            pl.BlockSpec((4, HEAD_DIM), lambda h, b: (0, 0)),
            pl.BlockSpec((1, V_DIM), lambda h, b: (0, 0)),
            w_in,
        ],
        out_specs=(pl.BlockSpec((seq, V_DIM), lambda h, b: (b, h)), w_out),
        compiler_params=_params("parallel", "parallel"),
        name="diff_attention",
    )(qkv, qkv, qkv, bias_tab, lam, subln_g.reshape(1, V_DIM), w_cast)


def _dil_attn_kernel(q_ref, k_ref, v_ref, bias_ref, o_ref, lse_ref, *scratch, sub_len, rows, dil, rps):
    blk = SW_STEPS
    c = pl.program_id(2)
    nblk = rows // blk
    single = sub_len == blk
    lane = lax.broadcasted_iota(jnp.int32, (blk, LANES), 1)

    def window(n):
        ng = c * nblk + n
        if single:
            return 0, 1, blk
        return pl.multiple_of(jnp.maximum(ng - 1, 0) * blk, blk), (ng == 0).astype(jnp.int32), 2 * blk

    def logits(rr, n, h):
        start, variant, width = window(n)
        tile, lo = h // 2, rr * COL_TILE + (h % 2) * HEAD_DIM
        qt = q_ref[tile, n * blk:(n + 1) * blk, lo:lo + HEAD_DIM]
        kw = k_ref[tile, pl.ds(start, width), lo:lo + HEAD_DIM]
        bias = bias_ref[0, h, variant][:, :width]
        return lax.dot_general(qt, kw, (((1,), (1,)), ((), ())), preferred_element_type=F32) + bias

    def attend(rr, n, h, s):
        start, _, width = window(n)
        mx = jnp.max(s, axis=-1, keepdims=True)
        p = jnp.exp(s - mx)
        den = jnp.sum(p, axis=-1, keepdims=True)
        vw = v_ref[h, pl.ds(start, width), rr * COL_TILE:(rr + 1) * COL_TILE]
        o = jnp.dot(p.astype(BF16), vw, preferred_element_type=F32) / den
        return o, mx + jnp.log(den)

    units = [(rr, n, h) for rr in range(rps) for n in range(nblk) for h in range(N_HEADS)]
    ahead = 2 if nblk == 1 else 1
    queue = [logits(*u) for u in units[:ahead]]
    lse_tile = None
    for i, (rr, n, h) in enumerate(units):
        s = queue.pop(0)
        if i + ahead < len(units):
            queue.append(logits(*units[i + ahead]))
        o, lse = attend(rr, n, h, s)
        if dil == 1:
            out_rows = slice(n * blk, (n + 1) * blk)
            o_ref[out_rows, h * V_DIM:(h + 1) * V_DIM] = o.astype(BF16)
        else:
            r = pl.program_id(1) * rps + rr
            out_rows = pl.ds(r + (c * nblk + n) * (blk * dil), blk, stride=dil)
            for part in range(V_DIM // LANES):
                scratch[0][h * (V_DIM // LANES) + part, out_rows, :] = o[:, part * LANES:(part + 1) * LANES]
        lse_tile = jnp.where(lane == h, lse, jnp.zeros((blk, LANES), F32) if h == 0 else lse_tile)
        if h == N_HEADS - 1:
            (lse_ref if dil == 1 else scratch[1])[out_rows, :] = lse_tile

    if dil > 1:
        @pl.when((pl.program_id(1) == pl.num_programs(1) - 1) & (c == pl.num_programs(2) - 1))
        def _():
            for slab in range(scratch[0].shape[0]):
                o_ref[:, slab * LANES:(slab + 1) * LANES] = scratch[0][slab].astype(BF16)
            lse_ref[...] = scratch[1][...]


def _dilated_group_attention(qkv, bias_tab, g, batch, seq):
    dil = SW_DILATIONS[g]
    sub_len = seq // dil
    rows = min(sub_len, 512)
    nc = sub_len // rows
    rps = max(1, 512 // sub_len)
    m = batch * seq
    width = N_HEADS * V_DIM
    kern = functools.partial(_dil_attn_kernel, sub_len=sub_len, rows=rows, dil=dil, rps=rps)
    if dil == 1:
        out_specs = (pl.BlockSpec((rows, width), lambda b, r, c: (b * nc + c, 0)),
                     pl.BlockSpec((rows, LANES), lambda b, r, c: (b * nc + c, 0)))
        scratch, sem = [], ("parallel", "parallel", "parallel")
    else:
        out_specs = (pl.BlockSpec((seq, width), lambda b, r, c: (b, 0)),
                     pl.BlockSpec((seq, LANES), lambda b, r, c: (b, 0)))
        scratch = [pltpu.VMEM((width // LANES, seq, LANES), F32), pltpu.VMEM((seq, LANES), F32)]
        sem = ("parallel", "arbitrary", "arbitrary")
    return pl.pallas_call(
        kern,
        out_shape=(jax.ShapeDtypeStruct((m, width), BF16), jax.ShapeDtypeStruct((m, LANES), F32)),
        grid=(batch, dil // rps, nc),
        in_specs=[
            pl.BlockSpec((4, rows, rps * COL_TILE), lambda b, r, c: (0, b * nc + c, r)),
            pl.BlockSpec((4, sub_len, rps * COL_TILE), lambda b, r, c: (1, b, r)),
            pl.BlockSpec((8, sub_len, rps * COL_TILE), lambda b, r, c: (1, b, r)),
            pl.BlockSpec((1, N_HEADS, 2, SW_STEPS, 2 * SW_STEPS), lambda b, r, c: (g, 0, 0, 0, 0)),
        ],
        out_specs=out_specs,
        scratch_shapes=scratch,
        compiler_params=_params(*sem),
        name=f"dilated_attention_g{g}",
    )(qkv, qkv, qkv, bias_tab)


def _proj_res_kernel(a_ref, w_ref, r_ref, o_ref):
    o_ref[...] = r_ref[...] + jnp.dot(a_ref[...], w_ref[...].astype(BF16), preferred_element_type=F32)


def _proj_residual(a, w, layer, res, tn):
    tm = _tiles()["tm"]
    m, k = a.shape
    n = w.shape[2]
    return pl.pallas_call(
        _proj_res_kernel,
        out_shape=jax.ShapeDtypeStruct((m, n), F32),
        grid=(m // tm, n // tn),
        in_specs=[
            pl.BlockSpec((tm, k), lambda i, j: (i, 0)),
            pl.BlockSpec((None, k, tn), lambda i, j: (layer, 0, j)),
            pl.BlockSpec((tm, tn), lambda i, j: (i, j)),
        ],
        out_specs=pl.BlockSpec((tm, tn), lambda i, j: (i, j)),
        compiler_params=_params("parallel", "arbitrary"),
        name="proj_residual",
    )(a, w, res)


def _attn_out_kernel(*refs, n_groups, tm, chunk):
    o_refs = refs[:n_groups]
    l_refs = refs[n_groups:2 * n_groups] if n_groups > 1 else ()
    w_ref, r_ref, out_ref, wb_ref = refs[-4:]

    @pl.when(pl.program_id(0) == 0)
    def _():
        wb_ref[...] = w_ref[...].astype(BF16)

    def lhs(r):
        if n_groups == 1:
            return o_refs[0][r:r + chunk, :]
        ls = [l_ref[r:r + chunk, :] for l_ref in l_refs]
        mx = functools.reduce(jnp.maximum, ls)
        es = [jnp.exp(l - mx) for l in ls]
        den = functools.reduce(jnp.add, es)
        alphas = [e / den for e in es]
        heads = []
        for h in range(N_HEADS):
            cols = slice(h * V_DIM, (h + 1) * V_DIM)
            terms = [al[:, h:h + 1] * o_ref[r:r + chunk, cols].astype(F32) for al, o_ref in zip(alphas, o_refs)]
            heads.append(functools.reduce(jnp.add, terms).astype(BF16))
        return jnp.concatenate(heads, axis=1)

    starts = list(range(0, tm, chunk))
    a_next = lhs(starts[0])
    for i, r in enumerate(starts):
        a_cur = a_next
        if i + 1 < len(starts):
            a_next = lhs(starts[i + 1])
        out_ref[r:r + chunk, :] = r_ref[r:r + chunk, :] + jnp.dot(a_cur, wb_ref[...], preferred_element_type=F32)


def _attn_out_proj(outs, lses, w, res):
    t = _tiles()
    tm, chunk = t["tm_attn_out"], t["attn_out_chunk"]
    m, k = outs[0].shape
    n = w.shape[2]
    kern = functools.partial(_attn_out_kernel, n_groups=len(outs), tm=tm, chunk=chunk)
    o_spec = pl.BlockSpec((tm, k), lambda i: (i, 0))
    l_spec = pl.BlockSpec((tm, LANES), lambda i: (i, 0))
    return pl.pallas_call(
        kern,
        out_shape=jax.ShapeDtypeStruct((m, n), F32),
        grid=(m // tm,),
        in_specs=[o_spec] * len(outs) + [l_spec] * len(lses) + [
            pl.BlockSpec((None, k, n), lambda i: (0, 0, 0), pipeline_mode=pl.Buffered(1)),
            pl.BlockSpec((tm, n), lambda i: (i, 0))],
        out_specs=pl.BlockSpec((tm, n), lambda i: (i, 0)),
        scratch_shapes=[pltpu.VMEM((k, n), BF16)],
        compiler_params=_params("arbitrary"),
        name="attn_out_proj",
    )(*outs, *lses, w, res)


def _ffn_up_kernel(x_ref, halo_ref, g_ref, wg0_ref, wgn_ref, wv_ref, cwg_ref, cwv_ref, cbg_ref, cbv_ref, *refs,
                   n_casts, tm, nj, tiles_per_seq, chunk):
    wsrc_refs = refs[:n_casts]
    o_ref = refs[n_casts]
    wdst_refs = refs[n_casts + 1:2 * n_casts + 1]
    xn_ref, ug_ref = refs[2 * n_casts + 1:]
    for wsrc_ref, wdst_ref in zip(wsrc_refs, wdst_refs):
        wdst_ref[...] = wsrc_ref[...].astype(BF16)
    i = pl.program_id(0)
    j = pl.program_id(1)
    halo = CONV_HALO

    @pl.when(j == 0)
    def _():
        _rmsnorm_rows(x_ref, g_ref, xn_ref, halo, tm, chunk)
        first = (i % tiles_per_seq) == 0

        @pl.when(first)
        def _():
            xn_ref[0:halo, :] = jnp.zeros((halo, xn_ref.shape[1]), BF16)

        @pl.when(jnp.logical_not(first))
        def _():
            _rmsnorm_rows(halo_ref, g_ref, xn_ref, 0, halo, halo)

        ug_ref[...] = jnp.dot(xn_ref[...], wg0_ref[...].astype(BF16), preferred_element_type=F32)

    def conv(u, cw_ref, cb_ref):
        cw = cw_ref[...]
        return (cw[0:1] * u[halo - 2:halo - 2 + tm] + cw[1:2] * u[halo - 1:halo - 1 + tm]
                + cw[2:3] * u[halo:halo + tm]) + cb_ref[...]

    def step(project_next_gate):
        xn = xn_ref[...]
        uv = jnp.dot(xn, wv_ref[...].astype(BF16), preferred_element_type=F32)
        gate = conv(ug_ref[...], cwg_ref, cbg_ref)
        if project_next_gate:
            ug_ref[...] = jnp.dot(xn, wgn_ref[...].astype(BF16), preferred_element_type=F32)
        val = conv(uv, cwv_ref, cbv_ref)
        o_ref[...] = (gate / (1.0 + jnp.exp(-gate)) * val).astype(BF16)

    pl.when(j < nj - 1)(functools.partial(step, True))
    pl.when(j == nj - 1)(functools.partial(step, False))


def _ffn_up(h2d, ln_g, w_up, layer, conv_w, conv_b, seq, casts):
    t = _tiles()
    tm, tn = t["tm"], t["tn_up"]
    m, d = h2d.shape
    dff = w_up.shape[2] // 2
    nj = dff // tn
    halo_blocks = tm // CONV_HALO
    kern = functools.partial(_ffn_up_kernel, n_casts=len(casts), tm=tm, nj=nj, tiles_per_seq=seq // tm,
                             chunk=t["norm_chunk"])
    jobs = [_cast_side_job(w, lyr, (m // tm) * nj, lambda i, j: i * nj + j) for w, lyr in casts]
    return pl.pallas_call(
        kern,
        out_shape=(jax.ShapeDtypeStruct((m, dff), BF16), *[job[2] for job in jobs]),
        grid=(m // tm, nj),
        in_specs=[
            pl.BlockSpec((tm, d), lambda i, j: (i, 0)),
            pl.BlockSpec((CONV_HALO, d), lambda i, j: (jnp.maximum(i * halo_blocks - 1, 0), 0)),
            pl.BlockSpec((1, d), lambda i, j: (0, 0)),
            pl.BlockSpec((None, d, tn), lambda i, j: (layer, 0, 0), pipeline_mode=pl.Buffered(1)),
            pl.BlockSpec((None, d, tn), lambda i, j: (layer, 0, jnp.minimum(j + 1, nj - 1))),
            pl.BlockSpec((None, d, tn), lambda i, j: (layer, 0, j + nj)),
            pl.BlockSpec((3, tn), lambda i, j: (0, j)),
            pl.BlockSpec((3, tn), lambda i, j: (0, j + nj)),
            pl.BlockSpec((1, tn), lambda i, j: (0, j)),
            pl.BlockSpec((1, tn), lambda i, j: (0, j + nj)),
            *[job[0] for job in jobs],
        ],
        out_specs=(pl.BlockSpec((tm, tn), lambda i, j: (i, j)), *[job[1] for job in jobs]),
        scratch_shapes=[pltpu.VMEM((tm + CONV_HALO, d), BF16), pltpu.VMEM((tm + CONV_HALO, tn), F32)],
        compiler_params=_params("parallel", "arbitrary"),
        name="ffn_up",
    )(h2d, h2d, ln_g.reshape(1, d), w_up, w_up, w_up, conv_w, conv_w, conv_b.reshape(1, -1), conv_b.reshape(1, -1),
      *[w for w, _ in casts])


def _conv_ffn(h2d, ln_g, w_up_bf16, conv_w, conv_b, w_down, layer, seq, extra_casts=()):
    act, w_down_bf16, *extra = _ffn_up(h2d, ln_g, w_up_bf16[None], 0, conv_w, conv_b, seq,
                                       [(w_down, layer), *extra_casts])
    return _proj_residual(act, w_down_bf16[None], 0, h2d, _tiles()["tn_down"]), extra


def _diff_col_gain(q_gain, k_gain):
    scale = HEAD_DIM ** -0.5
    nqk = 2 * N_HEADS
    gain = jnp.concatenate([jnp.tile(q_gain * scale, nqk), jnp.tile(k_gain, nqk),
                            jnp.ones((N_HEADS * V_DIM,), F32)])
    return gain


def _dil_col_gain(q_gain, k_gain):
    scale = HEAD_DIM ** -0.5
    parts = []
    for g in range(len(SW_DILATIONS)):
        parts += [jnp.tile(q_gain[g] * scale, N_HEADS), jnp.tile(k_gain[g], N_HEADS),
                  jnp.ones((N_HEADS * V_DIM,), F32)]
    return jnp.concatenate(parts)


def _tile_flags(n_cols, tn, normed_ranges):
    flags = []
    for j in range(n_cols // tn):
        lo = j * tn
        flags.append(int(any(a <= lo < b for a, b in normed_ranges)))
    return jnp.asarray(flags, jnp.int32)


def kernel(x, rel_bias, ln_mix, ln_ffn, a_w_qkv, a_q_norm, a_k_norm, a_lambda, a_subln, a_w_o,
           b_w_qkv, b_q_norm, b_k_norm, b_w_o, f_w_up, f_conv_w, f_conv_b, f_w_down):
    batch, seq, d = x.shape
    m = batch * seq
    t = _tiles()
    h = x.reshape(m, d)

    nqk = 2 * N_HEADS * HEAD_DIM
    flags0 = _tile_flags(a_w_qkv.shape[2], t["tn_qkv"], [(0, 2 * nqk)])
    (qkv0,) = _qkv_proj(h, ln_mix[0], a_w_qkv, _diff_col_gain(a_q_norm[0], a_k_norm[0]), flags0, (1,))
    lam_init = 0.8 - 0.6 * math.exp(-0.3 * 0)
    attn0, w_up0 = _diff_attention(qkv0, _diff_bias_table(rel_bias, seq), a_lambda[0], a_subln[0], batch, seq,
                                   lam_init, f_w_up, 0)
    h = _attn_out_proj([attn0], [], a_w_o, h)
    h, (w_up1, w_qkv1) = _conv_ffn(h, ln_ffn[0], w_up0, f_conv_w[0], f_conv_b[0], f_w_down, 0, seq,
                                   extra_casts=[(f_w_up, 1), (b_w_qkv, 0)])

    gcols = 2 * N_HEADS * HEAD_DIM + N_HEADS * V_DIM
    nq1 = N_HEADS * HEAD_DIM
    flags1 = _tile_flags(b_w_qkv.shape[2], t["tn_qkv"],
                         [(g * gcols, g * gcols + 2 * nq1) for g in range(len(SW_DILATIONS))])
    qkv1 = _qkv_proj(h, ln_mix[1], w_qkv1[None], _dil_col_gain(b_q_norm[0], b_k_norm[0]), flags1, SW_DILATIONS)
    dil_bias = _dil_bias_table(rel_bias)
    outs, lses = [], []
    for g in range(len(SW_DILATIONS)):
        o, lse = _dilated_group_attention(qkv1[g], dil_bias, g, batch, seq)
        outs.append(o)
        lses.append(lse)
    h = _attn_out_proj(outs, lses, b_w_o, h)
    h, _ = _conv_ffn(h, ln_ffn[1], w_up1, f_conv_w[1], f_conv_b[1], f_w_down, 1, seq)
    return h.reshape(batch, seq, d)
```

```python
import functools
import math

import jax
import jax.numpy as jnp
from jax import lax
from jax.experimental import pallas as pl
from jax.experimental.pallas import tpu as pltpu

F32 = jnp.float32
BF16 = jnp.bfloat16

N_HEADS = 8
HEAD_DIM = 128
V_DIM = 2 * HEAD_DIM
SW_DILATIONS = (1, 4, 16)
SW_STEPS = 128
NUM_BUCKETS = 32
MAX_DISTANCE = 2048
EPS = 1e-6
NEG = -1e30

LANES = 128
COL_TILE = 256
ATTN_TILE = 256
CONV_HALO = 16
VMEM_LIMIT = 60 * 1024 * 1024


def _tiles():
    return dict(tm=1024, tn_qkv=1024, qkv_rows_chunk=256, tn_up=512, tm_down=512, down_chunk=256,
                norm_chunk=256, tm_attn_out=512, attn_out_chunk=256)


def _params(*sem):
    return pltpu.CompilerParams(dimension_semantics=sem, vmem_limit_bytes=VMEM_LIMIT)


def _rmsnorm_rows(x_ref, g_ref, out_ref, row0, rows, chunk):
    g = g_ref[...]
    for r in range(0, rows, chunk):
        x = x_ref[r:r + chunk, :]
        rs = lax.rsqrt(jnp.mean(x * x, axis=-1, keepdims=True) + EPS)
        out_ref[row0 + r:row0 + r + chunk, :] = (x * rs * g).astype(BF16)


def _t5_bucket(dist):
    max_exact = NUM_BUCKETS // 2
    n = jnp.maximum(dist, 0)
    nf = jnp.maximum(n, 1).astype(F32)
    large = max_exact + (jnp.log(nf / max_exact) / math.log(MAX_DISTANCE / max_exact)
                         * (NUM_BUCKETS - max_exact)).astype(jnp.int32)
    large = jnp.minimum(large, NUM_BUCKETS - 1)
    return jnp.where(n < max_exact, n, large)


def _bucket_range(dmin, dmax):
    def bucket(n):
        n = max(n, 0)
        if n < NUM_BUCKETS // 2:
            return n
        half = NUM_BUCKETS // 2
        return min(half + int(math.log(n / half) / math.log(MAX_DISTANCE / half) * half), NUM_BUCKETS - 1)
    return max(bucket(dmin) - 1, 0), min(bucket(dmax) + 1, NUM_BUCKETS - 1)


def _bias_lookup(rb_ref, bucket, h, bmin, bmax):
    val = jnp.full(bucket.shape, rb_ref[bmin, h], F32)
    for b in range(bmin + 1, bmax + 1):
        val = jnp.where(bucket == b, rb_ref[b, h], val)
    return val


def _diff_bias_kernel(rb_ref, o_ref):
    h = pl.program_id(0)
    t = ATTN_TILE
    delta = (lax.broadcasted_iota(jnp.int32, (t, t), 0) - lax.broadcasted_iota(jnp.int32, (t, t), 1))
    for d in range(o_ref.shape[1]):
        dist = d * t + delta
        val = _bias_lookup(rb_ref, _t5_bucket(dist), h, *_bucket_range(d * t - (t - 1), d * t + (t - 1)))
        o_ref[0, d] = jnp.where(dist >= 0, val, NEG) if d == 0 else val


def _diff_bias_table(rel_bias, seq):
    nd = seq // ATTN_TILE
    return pl.pallas_call(
        _diff_bias_kernel,
        out_shape=jax.ShapeDtypeStruct((N_HEADS, nd, ATTN_TILE, ATTN_TILE), F32),
        grid=(N_HEADS,),
        in_specs=[pl.BlockSpec(memory_space=pltpu.SMEM)],
        out_specs=pl.BlockSpec((1, nd, ATTN_TILE, ATTN_TILE), lambda h: (h, 0, 0, 0)),
        compiler_params=_params("parallel"),
        name="diff_bias_table",
    )(rel_bias)


def _dil_bias_kernel(rb_ref, o_ref):
    h = pl.program_id(0)
    v = pl.program_id(1)
    a = lax.broadcasted_iota(jnp.int32, (SW_STEPS, 2 * SW_STEPS), 0)
    b = lax.broadcasted_iota(jnp.int32, (SW_STEPS, 2 * SW_STEPS), 1)
    dist_sub = a + SW_STEPS * (1 - v) - b
    in_window = (dist_sub >= 0) & (dist_sub <= SW_STEPS)
    for g, dil in enumerate(SW_DILATIONS):
        dist = jnp.clip(dist_sub, 0, SW_STEPS) * dil
        val = _bias_lookup(rb_ref, _t5_bucket(dist), h, *_bucket_range(0, SW_STEPS * dil))
        o_ref[g, 0, 0] = jnp.where(in_window, val, NEG)


def _dil_bias_table(rel_bias):
    ng = len(SW_DILATIONS)
    return pl.pallas_call(
        _dil_bias_kernel,
        out_shape=jax.ShapeDtypeStruct((ng, N_HEADS, 2, SW_STEPS, 2 * SW_STEPS), F32),
        grid=(N_HEADS, 2),
        in_specs=[pl.BlockSpec(memory_space=pltpu.SMEM)],
        out_specs=pl.BlockSpec((ng, 1, 1, SW_STEPS, 2 * SW_STEPS), lambda h, v: (0, h, v, 0, 0)),
        compiler_params=_params("parallel", "parallel"),
        name="dil_bias_table",
    )(rel_bias)


def _qkv_kernel(flag_ref, x_ref, g_ref, w_ref, cg_ref, *refs, tm, tn, chunk, dils, steps_per_group):
    o_refs = refs[:len(dils)]
    xn_ref, acc_ref = refs[len(dils):]
    j = pl.program_id(1)

    @pl.when(j == 0)
    def _():
        _rmsnorm_rows(x_ref, g_ref, xn_ref, 0, tm, chunk)

    normed = flag_ref[j] != 0
    per_tile = COL_TILE // LANES
    rows_chunk = acc_ref.shape[1]

    def finish(o_ref, dil, acc, r0):
        n_rows = rows_chunk // dil
        out_rows = slice(r0 // dil, r0 // dil + n_rows)
        for c in range(tn // LANES):
            y_all = acc[:, c * LANES:(c + 1) * LANES]
            if dil > 1:
                acc_ref[c] = y_all
            for r in range(dil):
                y = y_all if dil == 1 else acc_ref[c, pl.ds(r, n_rows, stride=dil), :]
                rs = lax.rsqrt(jnp.mean(y * y, axis=-1, keepdims=True) + EPS)
                scale = jnp.where(normed, rs, 1.0)
                lo = r * COL_TILE + (c % per_tile) * LANES
                o_ref[c // per_tile, out_rows, lo:lo + LANES] = (
                    y * scale * cg_ref[:, c * LANES:(c + 1) * LANES]).astype(BF16)

    def project_group(o_ref, dil):
        wb = w_ref[...].astype(BF16)
        starts = list(range(0, tm, rows_chunk))

        def project(r0):
            return jnp.dot(xn_ref[r0:r0 + rows_chunk, :], wb, preferred_element_type=F32)

        acc_next = project(starts[0])
        for i, r0 in enumerate(starts):
            acc = acc_next
            if i + 1 < len(starts):
                acc_next = project(starts[i + 1])
            finish(o_ref, dil, acc, r0)

    if len(dils) == 1:
        project_group(o_refs[0], dils[0])
    else:
        for g, dil in enumerate(dils):
            pl.when(j // steps_per_group == g)(functools.partial(project_group, o_refs[g], dil))


def _qkv_proj(x2d, ln_g, w, col_gain, tile_flags, dils):
    t = _tiles()
    tm, tn = t["tm"], t["tn_qkv"]
    m, d = x2d.shape
    n = w.shape[2]
    ng = len(dils)
    steps_per_group = n // ng // tn
    group_tiles = n // ng // COL_TILE
    kern = functools.partial(_qkv_kernel, tm=tm, tn=tn, chunk=t["norm_chunk"], dils=dils,
                             steps_per_group=steps_per_group)

    def out_spec(g, dil):
        def index(i, j, f):
            return (jnp.clip(j - g * steps_per_group, 0, steps_per_group - 1), i, 0)
        return pl.BlockSpec((tn // COL_TILE, tm // dil, dil * COL_TILE), index)

    return pl.pallas_call(
        kern,
        out_shape=[jax.ShapeDtypeStruct((group_tiles, m // dil, dil * COL_TILE), BF16) for dil in dils],
        grid_spec=pltpu.PrefetchScalarGridSpec(
            num_scalar_prefetch=1,
            grid=(m // tm, n // tn),
            in_specs=[
                pl.BlockSpec((tm, d), lambda i, j, f: (i, 0)),
                pl.BlockSpec((1, d), lambda i, j, f: (0, 0)),
                pl.BlockSpec((None, d, tn), lambda i, j, f: (0, 0, j)),
                pl.BlockSpec((1, tn), lambda i, j, f: (0, j)),
            ],
            out_specs=[out_spec(g, dil) for g, dil in enumerate(dils)],
            scratch_shapes=[pltpu.VMEM((tm, d), BF16),
                            pltpu.VMEM((tn // LANES, t["qkv_rows_chunk"], LANES), F32)],
        ),
        compiler_params=_params("parallel", "arbitrary"),
        name="qkv_proj",
    )(tile_flags, x2d, ln_g.reshape(1, d), w, col_gain.reshape(1, n))


def _diff_attn_kernel(q_ref, k_ref, v_ref, bias_ref, lam_ref, sg_ref, wsrc_ref, o_ref, wdst_ref, *, lam_init, seq):
    wdst_ref[...] = wsrc_ref[...].astype(BF16)
    t = ATTN_TILE
    lam = lam_ref[...]
    lam_full = (jnp.exp(jnp.sum(lam[0:1] * lam[1:2], axis=-1, keepdims=True))
                - jnp.exp(jnp.sum(lam[2:3] * lam[3:4], axis=-1, keepdims=True)) + lam_init)
    gain = sg_ref[...] * (1.0 - lam_init)

    def logits(c, mp):
        width = (c + 1) * t
        lo = mp * HEAD_DIM
        bias = jnp.concatenate([bias_ref[0, c - ki] for ki in range(c + 1)], axis=1)
        return lax.dot_general(q_ref[0, c * t:(c + 1) * t, lo:lo + HEAD_DIM], k_ref[0, 0:width, lo:lo + HEAD_DIM],
                               (((1,), (1,)), ((), ())), preferred_element_type=F32) + bias

    def attend(c, s):
        p = jnp.exp(s - jnp.max(s, axis=-1, keepdims=True))
        den = jnp.sum(p, axis=-1, keepdims=True)
        return jnp.dot(p.astype(BF16), v_ref[0, 0:(c + 1) * t, :], preferred_element_type=F32) / den

    units = [(c, mp) for c in reversed(range(seq // t)) for mp in range(2)]
    s_next = logits(*units[0])
    maps = {}
    for i, (c, mp) in enumerate(units):
        s_cur = s_next
        if i + 1 < len(units):
            s_next = logits(*units[i + 1])
        maps[mp] = attend(c, s_cur)
        if mp == 1:
            o = maps[0] - lam_full * maps[1]
            rs = lax.rsqrt(jnp.mean(o * o, axis=-1, keepdims=True) + EPS)
            o_ref[c * t:(c + 1) * t, :] = (o * rs * gain).astype(BF16)


def _cast_side_job(w, layer, n_steps, step_of):
    _, rows, cols = w.shape
    packed_rows = 16
    slab = packed_rows * pl.cdiv(rows, packed_rows * n_steps)
    n_slabs = rows // slab
    assert n_slabs * slab == rows and n_slabs <= n_steps

    def slab_of(*g):
        return jnp.minimum(step_of(*g), n_slabs - 1)

    in_spec = pl.BlockSpec((None, slab, cols), lambda *g: (layer, slab_of(*g), 0))
    out_spec = pl.BlockSpec((slab, cols), lambda *g: (slab_of(*g), 0))
    return in_spec, out_spec, jax.ShapeDtypeStruct((rows, cols), BF16)


def _diff_attention(qkv, bias_tab, lam, subln_g, batch, seq, lam_init, w_cast, w_cast_layer):
    t = ATTN_TILE
    nq = seq // t
    m = batch * seq
    kern = functools.partial(_diff_attn_kernel, lam_init=lam_init, seq=seq)
    w_in, w_out, w_shape = _cast_side_job(w_cast, w_cast_layer, N_HEADS * batch, lambda h, b: h * batch + b)
    return pl.pallas_call(
        kern,
        out_shape=(jax.ShapeDtypeStruct((m, N_HEADS * V_DIM), BF16), w_shape),
        grid=(N_HEADS, batch),
        in_specs=[
            pl.BlockSpec((1, seq, COL_TILE), lambda h, b: (h, b, 0)),
            pl.BlockSpec((1, seq, COL_TILE), lambda h, b: (N_HEADS + h, b, 0)),
            pl.BlockSpec((1, seq, COL_TILE), lambda h, b: (2 * N_HEADS + h, b, 0)),
            pl.BlockSpec((1, nq, t, t), lambda h, b: (h, 0, 0, 0)),
            pl.BlockSpec((4, HEAD_DIM), lambda h, b: (0, 0)),
            pl.BlockSpec((1, V_DIM), lambda h, b: (0, 0)),
            w_in,
        ],
        out_specs=(pl.BlockSpec((seq, V_DIM), lambda h, b: (b, h)), w_out),
        compiler_params=_params("parallel", "parallel"),
        name="diff_attention",
    )(qkv, qkv, qkv, bias_tab, lam, subln_g.reshape(1, V_DIM), w_cast)


def _dil_attn_kernel(q_ref, k_ref, v_ref, bias_ref, o_ref, lse_ref, *scratch, sub_len, rows, dil, rps):
    blk = SW_STEPS
    c = pl.program_id(2)
    nblk = rows // blk
    single = sub_len == blk
    lane = lax.broadcasted_iota(jnp.int32, (blk, LANES), 1)

    def window(n):
        ng = c * nblk + n
        if single:
            return 0, 1, blk
        return pl.multiple_of(jnp.maximum(ng - 1, 0) * blk, blk), (ng == 0).astype(jnp.int32), 2 * blk

    def logits(rr, n, h):
        start, variant, width = window(n)
        tile, lo = h // 2, rr * COL_TILE + (h % 2) * HEAD_DIM
        qt = q_ref[tile, n * blk:(n + 1) * blk, lo:lo + HEAD_DIM]
        kw = k_ref[tile, pl.ds(start, width), lo:lo + HEAD_DIM]
        bias = bias_ref[0, h, variant][:, :width]
        return lax.dot_general(qt, kw, (((1,), (1,)), ((), ())), preferred_element_type=F32) + bias

    def attend(rr, n, h, s):
        start, _, width = window(n)
        mx = jnp.max(s, axis=-1, keepdims=True)
        p = jnp.exp(s - mx)
        den = jnp.sum(p, axis=-1, keepdims=True)
        vw = v_ref[h, pl.ds(start, width), rr * COL_TILE:(rr + 1) * COL_TILE]
        o = jnp.dot(p.astype(BF16), vw, preferred_element_type=F32) / den
        return o, mx + jnp.log(den)

    units = [(rr, n, h) for rr in range(rps) for n in range(nblk) for h in range(N_HEADS)]
    ahead = 2 if nblk == 1 else 1
    queue = [logits(*u) for u in units[:ahead]]
    lse_tile = None
    for i, (rr, n, h) in enumerate(units):
        s = queue.pop(0)
        if i + ahead < len(units):
            queue.append(logits(*units[i + ahead]))
        o, lse = attend(rr, n, h, s)
        if dil == 1:
            out_rows = slice(n * blk, (n + 1) * blk)
            o_ref[out_rows, h * V_DIM:(h + 1) * V_DIM] = o.astype(BF16)
        else:
            r = pl.program_id(1) * rps + rr
            out_rows = pl.ds(r + (c * nblk + n) * (blk * dil), blk, stride=dil)
            for part in range(V_DIM // LANES):
                scratch[0][h * (V_DIM // LANES) + part, out_rows, :] = o[:, part * LANES:(part + 1) * LANES]
        lse_tile = jnp.where(lane == h, lse, jnp.zeros((blk, LANES), F32) if h == 0 else lse_tile)
        if h == N_HEADS - 1:
            (lse_ref if dil == 1 else scratch[1])[out_rows, :] = lse_tile

    if dil > 1:
        @pl.when((pl.program_id(1) == pl.num_programs(1) - 1) & (c == pl.num_programs(2) - 1))
        def _():
            for slab in range(scratch[0].shape[0]):
                o_ref[:, slab * LANES:(slab + 1) * LANES] = scratch[0][slab].astype(BF16)
            lse_ref[...] = scratch[1][...]


def _dilated_group_attention(qkv, bias_tab, g, batch, seq):
    dil = SW_DILATIONS[g]
    sub_len = seq // dil
    rows = min(sub_len, 512)
    nc = sub_len // rows
    rps = max(1, 512 // sub_len)
    m = batch * seq
    width = N_HEADS * V_DIM
    kern = functools.partial(_dil_attn_kernel, sub_len=sub_len, rows=rows, dil=dil, rps=rps)
    if dil == 1:
        out_specs = (pl.BlockSpec((rows, width), lambda b, r, c: (b * nc + c, 0)),
                     pl.BlockSpec((rows, LANES), lambda b, r, c: (b * nc + c, 0)))
        scratch, sem = [], ("parallel", "parallel", "parallel")
    else:
        out_specs = (pl.BlockSpec((seq, width), lambda b, r, c: (b, 0)),
                     pl.BlockSpec((seq, LANES), lambda b, r, c: (b, 0)))
        scratch = [pltpu.VMEM((width // LANES, seq, LANES), F32), pltpu.VMEM((seq, LANES), F32)]
        sem = ("parallel", "arbitrary", "arbitrary")
    return pl.pallas_call(
        kern,
        out_shape=(jax.ShapeDtypeStruct((m, width), BF16), jax.ShapeDtypeStruct((m, LANES), F32)),
        grid=(batch, dil // rps, nc),
        in_specs=[
            pl.BlockSpec((4, rows, rps * COL_TILE), lambda b, r, c: (0, b * nc + c, r)),
            pl.BlockSpec((4, sub_len, rps * COL_TILE), lambda b, r, c: (1, b, r)),
            pl.BlockSpec((8, sub_len, rps * COL_TILE), lambda b, r, c: (1, b, r)),
            pl.BlockSpec((1, N_HEADS, 2, SW_STEPS, 2 * SW_STEPS), lambda b, r, c: (g, 0, 0, 0, 0)),
        ],
        out_specs=out_specs,
        scratch_shapes=scratch,
        compiler_params=_params(*sem),
        name=f"dilated_attention_g{g}",
    )(qkv, qkv, qkv, bias_tab)


def _ffn_down_kernel(a_ref, w_ref, r_ref, o_ref, *, tm, chunk):
    for r in range(0, tm, chunk):
        o_ref[r:r + chunk, :] = r_ref[r:r + chunk, :] + jnp.dot(a_ref[r:r + chunk, :], w_ref[...],
                                                                preferred_element_type=F32)


def _ffn_down(a, w, res):
    t = _tiles()
    tm, chunk = t["tm_down"], t["down_chunk"]
    m, k = a.shape
    n = w.shape[1]
    return pl.pallas_call(
        functools.partial(_ffn_down_kernel, tm=tm, chunk=chunk),
        out_shape=jax.ShapeDtypeStruct((m, n), F32),
        grid=(m // tm,),
        in_specs=[
            pl.BlockSpec((tm, k), lambda i: (i, 0)),
            pl.BlockSpec((k, n), lambda i: (0, 0), pipeline_mode=pl.Buffered(1)),
            pl.BlockSpec((tm, n), lambda i: (i, 0)),
        ],
        out_specs=pl.BlockSpec((tm, n), lambda i: (i, 0)),
        compiler_params=_params("parallel"),
        name="ffn_down",
    )(a, w, res)


def _attn_out_kernel(*refs, n_groups, tm, chunk):
    o_refs = refs[:n_groups]
    l_refs = refs[n_groups:2 * n_groups] if n_groups > 1 else ()
    w_ref, r_ref, out_ref, wb_ref = refs[-4:]

    @pl.when(pl.program_id(0) == 0)
    def _():
        wb_ref[...] = w_ref[...].astype(BF16)

    def lhs(r):
        if n_groups == 1:
            return o_refs[0][r:r + chunk, :]
        ls = [l_ref[r:r + chunk, :] for l_ref in l_refs]
        mx = functools.reduce(jnp.maximum, ls)
        es = [jnp.exp(l - mx) for l in ls]
        den = functools.reduce(jnp.add, es)
        alphas = [e / den for e in es]
        heads = []
        for h in range(N_HEADS):
            cols = slice(h * V_DIM, (h + 1) * V_DIM)
            terms = [al[:, h:h + 1] * o_ref[r:r + chunk, cols].astype(F32) for al, o_ref in zip(alphas, o_refs)]
            heads.append(functools.reduce(jnp.add, terms).astype(BF16))
        return jnp.concatenate(heads, axis=1)

    starts = list(range(0, tm, chunk))
    a_next = lhs(starts[0])
    for i, r in enumerate(starts):
        a_cur = a_next
        if i + 1 < len(starts):
            a_next = lhs(starts[i + 1])
        out_ref[r:r + chunk, :] = r_ref[r:r + chunk, :] + jnp.dot(a_cur, wb_ref[...], preferred_element_type=F32)


def _attn_out_proj(outs, lses, w, res):
    t = _tiles()
    tm, chunk = t["tm_attn_out"], t["attn_out_chunk"]
    m, k = outs[0].shape
    n = w.shape[2]
    kern = functools.partial(_attn_out_kernel, n_groups=len(outs), tm=tm, chunk=chunk)
    o_spec = pl.BlockSpec((tm, k), lambda i: (i, 0))
    l_spec = pl.BlockSpec((tm, LANES), lambda i: (i, 0))
    return pl.pallas_call(
        kern,
        out_shape=jax.ShapeDtypeStruct((m, n), F32),
        grid=(m // tm,),
        in_specs=[o_spec] * len(outs) + [l_spec] * len(lses) + [
            pl.BlockSpec((None, k, n), lambda i: (0, 0, 0), pipeline_mode=pl.Buffered(1)),
            pl.BlockSpec((tm, n), lambda i: (i, 0))],
        out_specs=pl.BlockSpec((tm, n), lambda i: (i, 0)),
        scratch_shapes=[pltpu.VMEM((k, n), BF16)],
        compiler_params=_params("arbitrary"),
        name="attn_out_proj",
    )(*outs, *lses, w, res)


def _ffn_up_kernel(x_ref, halo_ref, g_ref, wg0_ref, wgn_ref, wv_ref, cwg_ref, cwv_ref, cbg_ref, cbv_ref, *refs,
                   n_casts, tm, nj, tiles_per_seq, chunk):
    wsrc_refs = refs[:n_casts]
    o_ref = refs[n_casts]
    wdst_refs = refs[n_casts + 1:2 * n_casts + 1]
    xn_ref, ug_ref = refs[2 * n_casts + 1:]
    for wsrc_ref, wdst_ref in zip(wsrc_refs, wdst_refs):
        wdst_ref[...] = wsrc_ref[...].astype(BF16)
    i = pl.program_id(0)
    j = pl.program_id(1)
    halo = CONV_HALO

    @pl.when(j == 0)
    def _():
        _rmsnorm_rows(x_ref, g_ref, xn_ref, halo, tm, chunk)
        first = (i % tiles_per_seq) == 0

        @pl.when(first)
        def _():
            xn_ref[0:halo, :] = jnp.zeros((halo, xn_ref.shape[1]), BF16)

        @pl.when(jnp.logical_not(first))
        def _():
            _rmsnorm_rows(halo_ref, g_ref, xn_ref, 0, halo, halo)

        ug_ref[...] = jnp.dot(xn_ref[...], wg0_ref[...].astype(BF16), preferred_element_type=F32)

    def conv(u, cw_ref, cb_ref):
        cw = cw_ref[...]
        return (cw[0:1] * u[halo - 2:halo - 2 + tm] + cw[1:2] * u[halo - 1:halo - 1 + tm]
                + cw[2:3] * u[halo:halo + tm]) + cb_ref[...]

    def step(project_next_gate):
        xn = xn_ref[...]
        uv = jnp.dot(xn, wv_ref[...].astype(BF16), preferred_element_type=F32)
        gate = conv(ug_ref[...], cwg_ref, cbg_ref)
        if project_next_gate:
            ug_ref[...] = jnp.dot(xn, wgn_ref[...].astype(BF16), preferred_element_type=F32)
        val = conv(uv, cwv_ref, cbv_ref)
        o_ref[...] = (gate / (1.0 + jnp.exp(-gate)) * val).astype(BF16)

    pl.when(j < nj - 1)(functools.partial(step, True))
    pl.when(j == nj - 1)(functools.partial(step, False))


def _ffn_up(h2d, ln_g, w_up, layer, conv_w, conv_b, seq, casts):
    t = _tiles()
    tm, tn = t["tm"], t["tn_up"]
    m, d = h2d.shape
    dff = w_up.shape[2] // 2
    nj = dff // tn
    halo_blocks = tm // CONV_HALO
    kern = functools.partial(_ffn_up_kernel, n_casts=len(casts), tm=tm, nj=nj, tiles_per_seq=seq // tm,
                             chunk=t["norm_chunk"])
    jobs = [_cast_side_job(w, lyr, (m // tm) * nj, lambda i, j: i * nj + j) for w, lyr in casts]
    return pl.pallas_call(
        kern,
        out_shape=(jax.ShapeDtypeStruct((m, dff), BF16), *[job[2] for job in jobs]),
        grid=(m // tm, nj),
        in_specs=[
            pl.BlockSpec((tm, d), lambda i, j: (i, 0)),
            pl.BlockSpec((CONV_HALO, d), lambda i, j: (jnp.maximum(i * halo_blocks - 1, 0), 0)),
            pl.BlockSpec((1, d), lambda i, j: (0, 0)),
            pl.BlockSpec((None, d, tn), lambda i, j: (layer, 0, 0), pipeline_mode=pl.Buffered(1)),
            pl.BlockSpec((None, d, tn), lambda i, j: (layer, 0, jnp.minimum(j + 1, nj - 1))),
            pl.BlockSpec((None, d, tn), lambda i, j: (layer, 0, j + nj)),
            pl.BlockSpec((3, tn), lambda i, j: (0, j)),
            pl.BlockSpec((3, tn), lambda i, j: (0, j + nj)),
            pl.BlockSpec((1, tn), lambda i, j: (0, j)),
            pl.BlockSpec((1, tn), lambda i, j: (0, j + nj)),
            *[job[0] for job in jobs],
        ],
        out_specs=(pl.BlockSpec((tm, tn), lambda i, j: (i, j)), *[job[1] for job in jobs]),
        scratch_shapes=[pltpu.VMEM((tm + CONV_HALO, d), BF16), pltpu.VMEM((tm + CONV_HALO, tn), F32)],
        compiler_params=_params("parallel", "arbitrary"),
        name="ffn_up",
    )(h2d, h2d, ln_g.reshape(1, d), w_up, w_up, w_up, conv_w, conv_w, conv_b.reshape(1, -1), conv_b.reshape(1, -1),
      *[w for w, _ in casts])


def _conv_ffn(h2d, ln_g, w_up_bf16, conv_w, conv_b, w_down, layer, seq, extra_casts=()):
    act, w_down_bf16, *extra = _ffn_up(h2d, ln_g, w_up_bf16[None], 0, conv_w, conv_b, seq,
                                       [(w_down, layer), *extra_casts])
    return _ffn_down(act, w_down_bf16, h2d), extra


def _diff_col_gain(q_gain, k_gain):
    scale = HEAD_DIM ** -0.5
    nqk = 2 * N_HEADS
    gain = jnp.concatenate([jnp.tile(q_gain * scale, nqk), jnp.tile(k_gain, nqk),
                            jnp.ones((N_HEADS * V_DIM,), F32)])
    return gain


def _dil_col_gain(q_gain, k_gain):
    scale = HEAD_DIM ** -0.5
    parts = []
    for g in range(len(SW_DILATIONS)):
        parts += [jnp.tile(q_gain[g] * scale, N_HEADS), jnp.tile(k_gain[g], N_HEADS),
                  jnp.ones((N_HEADS * V_DIM,), F32)]
    return jnp.concatenate(parts)


def _tile_flags(n_cols, tn, normed_ranges):
    flags = []
    for j in range(n_cols // tn):
        lo = j * tn
        flags.append(int(any(a <= lo < b for a, b in normed_ranges)))
    return jnp.asarray(flags, jnp.int32)


def kernel(x, rel_bias, ln_mix, ln_ffn, a_w_qkv, a_q_norm, a_k_norm, a_lambda, a_subln, a_w_o,
           b_w_qkv, b_q_norm, b_k_norm, b_w_o, f_w_up, f_conv_w, f_conv_b, f_w_down):
    batch, seq, d = x.shape
    m = batch * seq
    t = _tiles()
    h = x.reshape(m, d)

    nqk = 2 * N_HEADS * HEAD_DIM
    flags0 = _tile_flags(a_w_qkv.shape[2], t["tn_qkv"], [(0, 2 * nqk)])
    (qkv0,) = _qkv_proj(h, ln_mix[0], a_w_qkv, _diff_col_gain(a_q_norm[0], a_k_norm[0]), flags0, (1,))
    lam_init = 0.8 - 0.6 * math.exp(-0.3 * 0)
    attn0, w_up0 = _diff_attention(qkv0, _diff_bias_table(rel_bias, seq), a_lambda[0], a_subln[0], batch, seq,
                                   lam_init, f_w_up, 0)
    h = _attn_out_proj([attn0], [], a_w_o, h)
    h, (w_up1, w_qkv1) = _conv_ffn(h, ln_ffn[0], w_up0, f_conv_w[0], f_conv_b[0], f_w_down, 0, seq,
                                   extra_casts=[(f_w_up, 1), (b_w_qkv, 0)])

    gcols = 2 * N_HEADS * HEAD_DIM + N_HEADS * V_DIM
    nq1 = N_HEADS * HEAD_DIM
    flags1 = _tile_flags(b_w_qkv.shape[2], t["tn_qkv"],
                         [(g * gcols, g * gcols + 2 * nq1) for g in range(len(SW_DILATIONS))])
    qkv1 = _qkv_proj(h, ln_mix[1], w_qkv1[None], _dil_col_gain(b_q_norm[0], b_k_norm[0]), flags1, SW_DILATIONS)
    dil_bias = _dil_bias_table(rel_bias)
    outs, lses = [], []
    for g in range(len(SW_DILATIONS)):
        o, lse = _dilated_group_attention(qkv1[g], dil_bias, g, batch, seq)
        outs.append(o)
        lses.append(lse)
    h = _attn_out_proj(outs, lses, b_w_o, h)
    h, _ = _conv_ffn(h, ln_ffn[1], w_up1, f_conv_w[1], f_conv_b[1], f_w_down, 1, seq)
    return h.reshape(batch, seq, d)
```

```python
import functools
import math

import jax
import jax.numpy as jnp
from jax import lax
from jax.experimental import pallas as pl
from jax.experimental.pallas import tpu as pltpu

F32 = jnp.float32
BF16 = jnp.bfloat16

N_HEADS = 8
HEAD_DIM = 128
V_DIM = 2 * HEAD_DIM
SW_DILATIONS = (1, 4, 16)
SW_STEPS = 128
NUM_BUCKETS = 32
MAX_DISTANCE = 2048
EPS = 1e-6
NEG = -1e30

LANES = 128
COL_TILE = 256
ATTN_TILE = 256
CONV_HALO = 16
VMEM_LIMIT = 60 * 1024 * 1024


def _tiles():
    return dict(tm=1024, tn_qkv=1024, qkv_rows_chunk=256, tn_up=512, tm_down=512, down_chunk=256,
                norm_chunk=256, tm_attn_out=512, attn_out_chunk=256)


def _params(*sem):
    return pltpu.CompilerParams(dimension_semantics=sem, vmem_limit_bytes=VMEM_LIMIT)


def _rmsnorm_rows(x_ref, g_ref, out_ref, row0, rows, chunk):
    g = g_ref[...]
    for r in range(0, rows, chunk):
        x = x_ref[r:r + chunk, :]
        rs = lax.rsqrt(jnp.mean(x * x, axis=-1, keepdims=True) + EPS)
        out_ref[row0 + r:row0 + r + chunk, :] = (x * rs * g).astype(BF16)


def _t5_bucket(dist):
    max_exact = NUM_BUCKETS // 2
    n = jnp.maximum(dist, 0)
    nf = jnp.maximum(n, 1).astype(F32)
    large = max_exact + (jnp.log(nf / max_exact) / math.log(MAX_DISTANCE / max_exact)
                         * (NUM_BUCKETS - max_exact)).astype(jnp.int32)
    large = jnp.minimum(large, NUM_BUCKETS - 1)
    return jnp.where(n < max_exact, n, large)


def _bucket_range(dmin, dmax):
    def bucket(n):
        n = max(n, 0)
        if n < NUM_BUCKETS // 2:
            return n
        half = NUM_BUCKETS // 2
        return min(half + int(math.log(n / half) / math.log(MAX_DISTANCE / half) * half), NUM_BUCKETS - 1)
    return max(bucket(dmin) - 1, 0), min(bucket(dmax) + 1, NUM_BUCKETS - 1)


def _bias_lookup(rb_ref, bucket, h, bmin, bmax):
    val = jnp.full(bucket.shape, rb_ref[bmin, h], F32)
    for b in range(bmin + 1, bmax + 1):
        val = jnp.where(bucket == b, rb_ref[b, h], val)
    return val


def _diff_bias_kernel(rb_ref, wsrc_ref, o_ref, wdst_ref):
    wdst_ref[...] = wsrc_ref[...].astype(BF16)
    h = pl.program_id(0)
    t = ATTN_TILE
    delta = (lax.broadcasted_iota(jnp.int32, (t, t), 0) - lax.broadcasted_iota(jnp.int32, (t, t), 1))
    for d in range(o_ref.shape[1]):
        dist = d * t + delta
        val = _bias_lookup(rb_ref, _t5_bucket(dist), h, *_bucket_range(d * t - (t - 1), d * t + (t - 1)))
        o_ref[0, d] = jnp.where(dist >= 0, val, NEG) if d == 0 else val


def _diff_bias_table(rel_bias, seq, w_cast, w_cast_layer):
    nd = seq // ATTN_TILE
    w_in, w_out, w_shape = _cast_side_job(w_cast, w_cast_layer, N_HEADS, lambda h: h)
    return pl.pallas_call(
        _diff_bias_kernel,
        out_shape=(jax.ShapeDtypeStruct((N_HEADS, nd, ATTN_TILE, ATTN_TILE), F32), w_shape),
        grid=(N_HEADS,),
        in_specs=[pl.BlockSpec(memory_space=pltpu.SMEM), w_in],
        out_specs=(pl.BlockSpec((1, nd, ATTN_TILE, ATTN_TILE), lambda h: (h, 0, 0, 0)), w_out),
        compiler_params=_params("parallel"),
        name="diff_bias_table",
    )(rel_bias, w_cast)


def _dil_bias_kernel(rb_ref, o_ref):
    h = pl.program_id(0)
    v = pl.program_id(1)
    a = lax.broadcasted_iota(jnp.int32, (SW_STEPS, 2 * SW_STEPS), 0)
    b = lax.broadcasted_iota(jnp.int32, (SW_STEPS, 2 * SW_STEPS), 1)
    dist_sub = a + SW_STEPS * (1 - v) - b
    in_window = (dist_sub >= 0) & (dist_sub <= SW_STEPS)
    for g, dil in enumerate(SW_DILATIONS):
        dist = jnp.clip(dist_sub, 0, SW_STEPS) * dil
        val = _bias_lookup(rb_ref, _t5_bucket(dist), h, *_bucket_range(0, SW_STEPS * dil))
        o_ref[g, 0, 0] = jnp.where(in_window, val, NEG)


def _dil_bias_table(rel_bias):
    ng = len(SW_DILATIONS)
    return pl.pallas_call(
        _dil_bias_kernel,
        out_shape=jax.ShapeDtypeStruct((ng, N_HEADS, 2, SW_STEPS, 2 * SW_STEPS), F32),
        grid=(N_HEADS, 2),
        in_specs=[pl.BlockSpec(memory_space=pltpu.SMEM)],
        out_specs=pl.BlockSpec((ng, 1, 1, SW_STEPS, 2 * SW_STEPS), lambda h, v: (0, h, v, 0, 0)),
        compiler_params=_params("parallel", "parallel"),
        name="dil_bias_table",
    )(rel_bias)


def _qkv_kernel(flag_ref, x_ref, g_ref, w_ref, cg_ref, *refs, tm, tn, chunk, dils, steps_per_group):
    o_refs = refs[:len(dils)]
    xn_ref, acc_ref = refs[len(dils):]
    j = pl.program_id(1)
    normed = flag_ref[j] != 0
    per_tile = COL_TILE // LANES
    rows_chunk = acc_ref.shape[1]

    def finish(o_ref, dil, acc, r0):
        n_rows = rows_chunk // dil
        out_rows = slice(r0 // dil, r0 // dil + n_rows)
        for c in range(tn // LANES):
            y_all = acc[:, c * LANES:(c + 1) * LANES]
            if dil > 1:
                acc_ref[c] = y_all
            for r in range(dil):
                y = y_all if dil == 1 else acc_ref[c, pl.ds(r, n_rows, stride=dil), :]
                rs = lax.rsqrt(jnp.mean(y * y, axis=-1, keepdims=True) + EPS)
                scale = jnp.where(normed, rs, 1.0)
                lo = r * COL_TILE + (c % per_tile) * LANES
                o_ref[c // per_tile, out_rows, lo:lo + LANES] = (
                    y * scale * cg_ref[:, c * LANES:(c + 1) * LANES]).astype(BF16)

    def project_group(o_ref, dil, first_step):
        wb = w_ref[...].astype(BF16)
        starts = list(range(0, tm, rows_chunk))

        def project(r0):
            if first_step:
                _rmsnorm_rows(x_ref.at[r0:r0 + rows_chunk], g_ref, xn_ref, r0, rows_chunk, rows_chunk)
            return jnp.dot(xn_ref[r0:r0 + rows_chunk, :], wb, preferred_element_type=F32)

        acc_next = project(starts[0])
        for i, r0 in enumerate(starts):
            acc = acc_next
            if i + 1 < len(starts):
                acc_next = project(starts[i + 1])
            finish(o_ref, dil, acc, r0)

    pl.when(j == 0)(functools.partial(project_group, o_refs[0], dils[0], True))
    for g, dil in enumerate(dils):
        in_group = (j // steps_per_group == g) & (j > 0)
        pl.when(in_group)(functools.partial(project_group, o_refs[g], dil, False))


def _qkv_proj(x2d, ln_g, w, col_gain, tile_flags, dils):
    t = _tiles()
    tm, tn = t["tm"], t["tn_qkv"]
    m, d = x2d.shape
    n = w.shape[2]
    ng = len(dils)
    steps_per_group = n // ng // tn
    group_tiles = n // ng // COL_TILE
    kern = functools.partial(_qkv_kernel, tm=tm, tn=tn, chunk=t["norm_chunk"], dils=dils,
                             steps_per_group=steps_per_group)

    def out_spec(g, dil):
        def index(i, j, f):
            return (jnp.clip(j - g * steps_per_group, 0, steps_per_group - 1), i, 0)
        return pl.BlockSpec((tn // COL_TILE, tm // dil, dil * COL_TILE), index)

    return pl.pallas_call(
        kern,
        out_shape=[jax.ShapeDtypeStruct((group_tiles, m // dil, dil * COL_TILE), BF16) for dil in dils],
        grid_spec=pltpu.PrefetchScalarGridSpec(
            num_scalar_prefetch=1,
            grid=(m // tm, n // tn),
            in_specs=[
                pl.BlockSpec((tm, d), lambda i, j, f: (i, 0)),
                pl.BlockSpec((1, d), lambda i, j, f: (0, 0)),
                pl.BlockSpec((None, d, tn), lambda i, j, f: (0, 0, j)),
                pl.BlockSpec((1, tn), lambda i, j, f: (0, j)),
            ],
            out_specs=[out_spec(g, dil) for g, dil in enumerate(dils)],
            scratch_shapes=[pltpu.VMEM((tm, d), BF16),
                            pltpu.VMEM((tn // LANES, t["qkv_rows_chunk"], LANES), F32)],
        ),
        compiler_params=_params("parallel", "arbitrary"),
        name="qkv_proj",
    )(tile_flags, x2d, ln_g.reshape(1, d), w, col_gain.reshape(1, n))


def _diff_attn_kernel(q_ref, k_ref, v_ref, bias_ref, lam_ref, sg_ref, wsrc_ref, o_ref, wdst_ref, *, lam_init, seq):
    wdst_ref[...] = wsrc_ref[...].astype(BF16)
    t = ATTN_TILE
    lam = lam_ref[...]
    lam_full = (jnp.exp(jnp.sum(lam[0:1] * lam[1:2], axis=-1, keepdims=True))
                - jnp.exp(jnp.sum(lam[2:3] * lam[3:4], axis=-1, keepdims=True)) + lam_init)
    gain = sg_ref[...] * (1.0 - lam_init)

    def logits(c, mp):
        width = (c + 1) * t
        lo = mp * HEAD_DIM
        bias = jnp.concatenate([bias_ref[0, c - ki] for ki in range(c + 1)], axis=1)
        return lax.dot_general(q_ref[0, c * t:(c + 1) * t, lo:lo + HEAD_DIM], k_ref[0, 0:width, lo:lo + HEAD_DIM],
                               (((1,), (1,)), ((), ())), preferred_element_type=F32) + bias

    def attend(c, s):
        p = jnp.exp(s - jnp.max(s, axis=-1, keepdims=True))
        den = jnp.sum(p, axis=-1, keepdims=True)
        return jnp.dot(p.astype(BF16), v_ref[0, 0:(c + 1) * t, :], preferred_element_type=F32) / den

    units = [(c, mp) for c in reversed(range(seq // t)) for mp in range(2)]
    s_next = logits(*units[0])
    maps = {}
    for i, (c, mp) in enumerate(units):
        s_cur = s_next
        if i + 1 < len(units):
            s_next = logits(*units[i + 1])
        maps[mp] = attend(c, s_cur)
        if mp == 1:
            o = maps[0] - lam_full * maps[1]
            rs = lax.rsqrt(jnp.mean(o * o, axis=-1, keepdims=True) + EPS)
            o_ref[c * t:(c + 1) * t, :] = (o * rs * gain).astype(BF16)


def _cast_side_job(w, layer, n_steps, step_of):
    _, rows, cols = w.shape
    packed_rows = 16
    slab = packed_rows * pl.cdiv(rows, packed_rows * n_steps)
    n_slabs = rows // slab
    assert n_slabs * slab == rows and n_slabs <= n_steps

    def slab_of(*g):
        return jnp.minimum(step_of(*g), n_slabs - 1)

    in_spec = pl.BlockSpec((None, slab, cols), lambda *g: (layer, slab_of(*g), 0))
    out_spec = pl.BlockSpec((slab, cols), lambda *g: (slab_of(*g), 0))
    return in_spec, out_spec, jax.ShapeDtypeStruct((rows, cols), BF16)


def _diff_attention(qkv, bias_tab, lam, subln_g, batch, seq, lam_init, w_cast, w_cast_layer):
    t = ATTN_TILE
    nq = seq // t
    m = batch * seq
    kern = functools.partial(_diff_attn_kernel, lam_init=lam_init, seq=seq)
    w_in, w_out, w_shape = _cast_side_job(w_cast, w_cast_layer, N_HEADS * batch, lambda h, b: h * batch + b)
    return pl.pallas_call(
        kern,
        out_shape=(jax.ShapeDtypeStruct((m, N_HEADS * V_DIM), BF16), w_shape),
        grid=(N_HEADS, batch),
        in_specs=[
            pl.BlockSpec((1, seq, COL_TILE), lambda h, b: (h, b, 0)),
            pl.BlockSpec((1, seq, COL_TILE), lambda h, b: (N_HEADS + h, b, 0)),
            pl.BlockSpec((1, seq, COL_TILE), lambda h, b: (2 * N_HEADS + h, b, 0)),
            pl.BlockSpec((1, nq, t, t), lambda h, b: (h, 0, 0, 0)),
            pl.BlockSpec((4, HEAD_DIM), lambda h, b: (0, 0)),
            pl.BlockSpec((1, V_DIM), lambda h, b: (0, 0)),
            w_in,
        ],
        out_specs=(pl.BlockSpec((seq, V_DIM), lambda h, b: (b, h)), w_out),
        compiler_params=_params("parallel", "parallel"),
        name="diff_attention",
    )(qkv, qkv, qkv, bias_tab, lam, subln_g.reshape(1, V_DIM), w_cast)


def _dil_attn_kernel(q_ref, k_ref, v_ref, bias_ref, o_ref, lse_ref, *scratch, sub_len, rows, dil, rps):
    blk = SW_STEPS
    c = pl.program_id(2)
    nblk = rows // blk
    single = sub_len == blk
    lane = lax.broadcasted_iota(jnp.int32, (blk, LANES), 1)

    def window(n):
        ng = c * nblk + n
        if single:
            return 0, 1, blk
        return pl.multiple_of(jnp.maximum(ng - 1, 0) * blk, blk), (ng == 0).astype(jnp.int32), 2 * blk

    def logits(rr, n, h):
        start, variant, width = window(n)
        tile, lo = h // 2, rr * COL_TILE + (h % 2) * HEAD_DIM
        qt = q_ref[tile, n * blk:(n + 1) * blk, lo:lo + HEAD_DIM]
        kw = k_ref[tile, pl.ds(start, width), lo:lo + HEAD_DIM]
        bias = bias_ref[0, h, variant][:, :width]
        return lax.dot_general(qt, kw, (((1,), (1,)), ((), ())), preferred_element_type=F32) + bias

    def attend(rr, n, h, s):
        start, _, width = window(n)
        mx = jnp.max(s, axis=-1, keepdims=True)
        p = jnp.exp(s - mx)
        den = jnp.sum(p, axis=-1, keepdims=True)
        vw = v_ref[h, pl.ds(start, width), rr * COL_TILE:(rr + 1) * COL_TILE]
        o = jnp.dot(p.astype(BF16), vw, preferred_element_type=F32) / den
        return o, mx + jnp.log(den)

    units = [(rr, n, h) for rr in range(rps) for n in range(nblk) for h in range(N_HEADS)]
    ahead = 2 if nblk == 1 else 1
    queue = [logits(*u) for u in units[:ahead]]
    lse_tile = None
    for i, (rr, n, h) in enumerate(units):
        s = queue.pop(0)
        if i + ahead < len(units):
            queue.append(logits(*units[i + ahead]))
        o, lse = attend(rr, n, h, s)
        if dil == 1:
            out_rows = slice(n * blk, (n + 1) * blk)
            o_ref[out_rows, h * V_DIM:(h + 1) * V_DIM] = o.astype(BF16)
        else:
            r = pl.program_id(1) * rps + rr
            out_rows = pl.ds(r + (c * nblk + n) * (blk * dil), blk, stride=dil)
            for part in range(V_DIM // LANES):
                scratch[0][h * (V_DIM // LANES) + part, out_rows, :] = o[:, part * LANES:(part + 1) * LANES]
        lse_tile = jnp.where(lane == h, lse, jnp.zeros((blk, LANES), F32) if h == 0 else lse_tile)
        if h == N_HEADS - 1:
            (lse_ref if dil == 1 else scratch[1])[out_rows, :] = lse_tile

    if dil > 1:
        @pl.when((pl.program_id(1) == pl.num_programs(1) - 1) & (c == pl.num_programs(2) - 1))
        def _():
            for slab in range(scratch[0].shape[0]):
                o_ref[:, slab * LANES:(slab + 1) * LANES] = scratch[0][slab].astype(BF16)
            lse_ref[...] = scratch[1][...]


def _dilated_group_attention(qkv, bias_tab, g, batch, seq):
    dil = SW_DILATIONS[g]
    sub_len = seq // dil
    rows = min(sub_len, 512)
    nc = sub_len // rows
    rps = max(1, 512 // sub_len)
    m = batch * seq
    width = N_HEADS * V_DIM
    kern = functools.partial(_dil_attn_kernel, sub_len=sub_len, rows=rows, dil=dil, rps=rps)
    if dil == 1:
        out_specs = (pl.BlockSpec((rows, width), lambda b, r, c: (b * nc + c, 0)),
                     pl.BlockSpec((rows, LANES), lambda b, r, c: (b * nc + c, 0)))
        scratch, sem = [], ("parallel", "parallel", "parallel")
    else:
        out_specs = (pl.BlockSpec((seq, width), lambda b, r, c: (b, 0)),
                     pl.BlockSpec((seq, LANES), lambda b, r, c: (b, 0)))
        scratch = [pltpu.VMEM((width // LANES, seq, LANES), F32), pltpu.VMEM((seq, LANES), F32)]
        sem = ("parallel", "arbitrary", "arbitrary")
    return pl.pallas_call(
        kern,
        out_shape=(jax.ShapeDtypeStruct((m, width), BF16), jax.ShapeDtypeStruct((m, LANES), F32)),
        grid=(batch, dil // rps, nc),
        in_specs=[
            pl.BlockSpec((4, rows, rps * COL_TILE), lambda b, r, c: (0, b * nc + c, r)),
            pl.BlockSpec((4, sub_len, rps * COL_TILE), lambda b, r, c: (1, b, r)),
            pl.BlockSpec((8, sub_len, rps * COL_TILE), lambda b, r, c: (1, b, r)),
            pl.BlockSpec((1, N_HEADS, 2, SW_STEPS, 2 * SW_STEPS), lambda b, r, c: (g, 0, 0, 0, 0)),
        ],
        out_specs=out_specs,
        scratch_shapes=scratch,
        compiler_params=_params(*sem),
        name=f"dilated_attention_g{g}",
    )(qkv, qkv, qkv, bias_tab)


def _ffn_down_kernel(a_ref, w_ref, r_ref, o_ref, *, tm, chunk):
    for r in range(0, tm, chunk):
        o_ref[r:r + chunk, :] = r_ref[r:r + chunk, :] + jnp.dot(a_ref[r:r + chunk, :], w_ref[...],
                                                                preferred_element_type=F32)


def _ffn_down(a, w, res):
    t = _tiles()
    tm, chunk = t["tm_down"], t["down_chunk"]
    m, k = a.shape
    n = w.shape[1]
    return pl.pallas_call(
        functools.partial(_ffn_down_kernel, tm=tm, chunk=chunk),
        out_shape=jax.ShapeDtypeStruct((m, n), F32),
        grid=(m // tm,),
        in_specs=[
            pl.BlockSpec((tm, k), lambda i: (i, 0)),
            pl.BlockSpec((k, n), lambda i: (0, 0), pipeline_mode=pl.Buffered(1)),
            pl.BlockSpec((tm, n), lambda i: (i, 0)),
        ],
        out_specs=pl.BlockSpec((tm, n), lambda i: (i, 0)),
        compiler_params=_params("parallel"),
        name="ffn_down",
    )(a, w, res)


def _attn_out_kernel(*refs, n_groups, tm, chunk):
    o_refs = refs[:n_groups]
    l_refs = refs[n_groups:2 * n_groups] if n_groups > 1 else ()
    w_ref, r_ref, out_ref, wb_ref = refs[-4:]

    @pl.when(pl.program_id(0) == 0)
    def _():
        wb_ref[...] = w_ref[...].astype(BF16)

    def lhs(r):
        if n_groups == 1:
            return o_refs[0][r:r + chunk, :]
        ls = [l_ref[r:r + chunk, :] for l_ref in l_refs]
        mx = functools.reduce(jnp.maximum, ls)
        es = [jnp.exp(l - mx) for l in ls]
        den = functools.reduce(jnp.add, es)
        alphas = [e / den for e in es]
        heads = []
        for h in range(N_HEADS):
            cols = slice(h * V_DIM, (h + 1) * V_DIM)
            terms = [al[:, h:h + 1] * o_ref[r:r + chunk, cols].astype(F32) for al, o_ref in zip(alphas, o_refs)]
            heads.append(functools.reduce(jnp.add, terms).astype(BF16))
        return jnp.concatenate(heads, axis=1)

    starts = list(range(0, tm, chunk))
    a_next = lhs(starts[0])
    for i, r in enumerate(starts):
        a_cur = a_next
        if i + 1 < len(starts):
            a_next = lhs(starts[i + 1])
        out_ref[r:r + chunk, :] = r_ref[r:r + chunk, :] + jnp.dot(a_cur, wb_ref[...], preferred_element_type=F32)


def _attn_out_proj(outs, lses, w, res):
    t = _tiles()
    tm, chunk = t["tm_attn_out"], t["attn_out_chunk"]
    m, k = outs[0].shape
    n = w.shape[2]
    kern = functools.partial(_attn_out_kernel, n_groups=len(outs), tm=tm, chunk=chunk)
    o_spec = pl.BlockSpec((tm, k), lambda i: (i, 0))
    l_spec = pl.BlockSpec((tm, LANES), lambda i: (i, 0))
    return pl.pallas_call(
        kern,
        out_shape=jax.ShapeDtypeStruct((m, n), F32),
        grid=(m // tm,),
        in_specs=[o_spec] * len(outs) + [l_spec] * len(lses) + [
            pl.BlockSpec((None, k, n), lambda i: (0, 0, 0), pipeline_mode=pl.Buffered(1)),
            pl.BlockSpec((tm, n), lambda i: (i, 0))],
        out_specs=pl.BlockSpec((tm, n), lambda i: (i, 0)),
        scratch_shapes=[pltpu.VMEM((k, n), BF16)],
        compiler_params=_params("arbitrary"),
        name="attn_out_proj",
    )(*outs, *lses, w, res)


def _ffn_up_kernel(x_ref, halo_ref, g_ref, wg0_ref, wgn_ref, wv_ref, cwg_ref, cwv_ref, cbg_ref, cbv_ref, *refs,
                   n_casts, tm, nj, tiles_per_seq, chunk):
    wsrc_refs = refs[:n_casts]
    o_ref = refs[n_casts]
    wdst_refs = refs[n_casts + 1:2 * n_casts + 1]
    xn_ref, ug_ref = refs[2 * n_casts + 1:]
    for wsrc_ref, wdst_ref in zip(wsrc_refs, wdst_refs):
        wdst_ref[...] = wsrc_ref[...].astype(BF16)
    i = pl.program_id(0)
    j = pl.program_id(1)
    halo = CONV_HALO

    @pl.when(j == 0)
    def _():
        _rmsnorm_rows(x_ref, g_ref, xn_ref, halo, tm, chunk)
        first = (i % tiles_per_seq) == 0

        @pl.when(first)
        def _():
            xn_ref[0:halo, :] = jnp.zeros((halo, xn_ref.shape[1]), BF16)

        @pl.when(jnp.logical_not(first))
        def _():
            _rmsnorm_rows(halo_ref, g_ref, xn_ref, 0, halo, halo)

        ug_ref[...] = jnp.dot(xn_ref[...], wg0_ref[...].astype(BF16), preferred_element_type=F32)

    def conv(u, cw_ref, cb_ref):
        cw = cw_ref[...]
        return (cw[0:1] * u[halo - 2:halo - 2 + tm] + cw[1:2] * u[halo - 1:halo - 1 + tm]
                + cw[2:3] * u[halo:halo + tm]) + cb_ref[...]

    def step(project_next_gate):
        xn = xn_ref[...]
        uv = jnp.dot(xn, wv_ref[...].astype(BF16), preferred_element_type=F32)
        gate = conv(ug_ref[...], cwg_ref, cbg_ref)
        if project_next_gate:
            ug_ref[...] = jnp.dot(xn, wgn_ref[...].astype(BF16), preferred_element_type=F32)
        val = conv(uv, cwv_ref, cbv_ref)
        o_ref[...] = (gate / (1.0 + jnp.exp(-gate)) * val).astype(BF16)

    pl.when(j < nj - 1)(functools.partial(step, True))
    pl.when(j == nj - 1)(functools.partial(step, False))


def _ffn_up(h2d, ln_g, w_up, layer, conv_w, conv_b, seq, casts):
    t = _tiles()
    tm, tn = t["tm"], t["tn_up"]
    m, d = h2d.shape
    dff = w_up.shape[2] // 2
    nj = dff // tn
    halo_blocks = tm // CONV_HALO
    kern = functools.partial(_ffn_up_kernel, n_casts=len(casts), tm=tm, nj=nj, tiles_per_seq=seq // tm,
                             chunk=t["norm_chunk"])
    jobs = [_cast_side_job(w, lyr, (m // tm) * nj, lambda i, j: i * nj + j) for w, lyr in casts]
    return pl.pallas_call(
        kern,
        out_shape=(jax.ShapeDtypeStruct((m, dff), BF16), *[job[2] for job in jobs]),
        grid=(m // tm, nj),
        in_specs=[
            pl.BlockSpec((tm, d), lambda i, j: (i, 0)),
            pl.BlockSpec((CONV_HALO, d), lambda i, j: (jnp.maximum(i * halo_blocks - 1, 0), 0)),
            pl.BlockSpec((1, d), lambda i, j: (0, 0)),
            pl.BlockSpec((None, d, tn), lambda i, j: (layer, 0, 0), pipeline_mode=pl.Buffered(1)),
            pl.BlockSpec((None, d, tn), lambda i, j: (layer, 0, jnp.minimum(j + 1, nj - 1))),
            pl.BlockSpec((None, d, tn), lambda i, j: (layer, 0, j + nj)),
            pl.BlockSpec((3, tn), lambda i, j: (0, j)),
            pl.BlockSpec((3, tn), lambda i, j: (0, j + nj)),
            pl.BlockSpec((1, tn), lambda i, j: (0, j)),
            pl.BlockSpec((1, tn), lambda i, j: (0, j + nj)),
            *[job[0] for job in jobs],
        ],
        out_specs=(pl.BlockSpec((tm, tn), lambda i, j: (i, j)), *[job[1] for job in jobs]),
        scratch_shapes=[pltpu.VMEM((tm + CONV_HALO, d), BF16), pltpu.VMEM((tm + CONV_HALO, tn), F32)],
        compiler_params=_params("parallel", "arbitrary"),
        name="ffn_up",
    )(h2d, h2d, ln_g.reshape(1, d), w_up, w_up, w_up, conv_w, conv_w, conv_b.reshape(1, -1), conv_b.reshape(1, -1),
      *[w for w, _ in casts])


def _conv_ffn(h2d, ln_g, w_up_bf16, conv_w, conv_b, w_down, layer, seq, extra_casts=()):
    act, w_down_bf16, *extra = _ffn_up(h2d, ln_g, w_up_bf16[None], 0, conv_w, conv_b, seq,
                                       [(w_down, layer), *extra_casts])
    return _ffn_down(act, w_down_bf16, h2d), extra


def _diff_col_gain(q_gain, k_gain):
    scale = HEAD_DIM ** -0.5
    nqk = 2 * N_HEADS
    gain = jnp.concatenate([jnp.tile(q_gain * scale, nqk), jnp.tile(k_gain, nqk),
                            jnp.ones((N_HEADS * V_DIM,), F32)])
    return gain


def _dil_col_gain(q_gain, k_gain):
    scale = HEAD_DIM ** -0.5
    parts = []
    for g in range(len(SW_DILATIONS)):
        parts += [jnp.tile(q_gain[g] * scale, N_HEADS), jnp.tile(k_gain[g], N_HEADS),
                  jnp.ones((N_HEADS * V_DIM,), F32)]
    return jnp.concatenate(parts)


def _tile_flags(n_cols, tn, normed_ranges):
    flags = []
    for j in range(n_cols // tn):
        lo = j * tn
        flags.append(int(any(a <= lo < b for a, b in normed_ranges)))
    return jnp.asarray(flags, jnp.int32)


def kernel(x, rel_bias, ln_mix, ln_ffn, a_w_qkv, a_q_norm, a_k_norm, a_lambda, a_subln, a_w_o,
           b_w_qkv, b_q_norm, b_k_norm, b_w_o, f_w_up, f_conv_w, f_conv_b, f_w_down):
    batch, seq, d = x.shape
    m = batch * seq
    t = _tiles()
    h = x.reshape(m, d)

    nqk = 2 * N_HEADS * HEAD_DIM
    flags0 = _tile_flags(a_w_qkv.shape[2], t["tn_qkv"], [(0, 2 * nqk)])
    diff_bias, w_qkv0 = _diff_bias_table(rel_bias, seq, a_w_qkv, 0)
    (qkv0,) = _qkv_proj(h, ln_mix[0], w_qkv0[None], _diff_col_gain(a_q_norm[0], a_k_norm[0]), flags0, (1,))
    lam_init = 0.8 - 0.6 * math.exp(-0.3 * 0)
    attn0, w_up0 = _diff_attention(qkv0, diff_bias, a_lambda[0], a_subln[0], batch, seq, lam_init, f_w_up, 0)
    h = _attn_out_proj([attn0], [], a_w_o, h)
    h, (w_up1, w_qkv1) = _conv_ffn(h, ln_ffn[0], w_up0, f_conv_w[0], f_conv_b[0], f_w_down, 0, seq,
                                   extra_casts=[(f_w_up, 1), (b_w_qkv, 0)])

    gcols = 2 * N_HEADS * HEAD_DIM + N_HEADS * V_DIM
    nq1 = N_HEADS * HEAD_DIM
    flags1 = _tile_flags(b_w_qkv.shape[2], t["tn_qkv"],
                         [(g * gcols, g * gcols + 2 * nq1) for g in range(len(SW_DILATIONS))])
    qkv1 = _qkv_proj(h, ln_mix[1], w_qkv1[None], _dil_col_gain(b_q_norm[0], b_k_norm[0]), flags1, SW_DILATIONS)
    dil_bias = _dil_bias_table(rel_bias)
    outs, lses = [], []
    for g in range(len(SW_DILATIONS)):
        o, lse = _dilated_group_attention(qkv1[g], dil_bias, g, batch, seq)
        outs.append(o)
        lses.append(lse)
    h = _attn_out_proj(outs, lses, b_w_o, h)
    h, _ = _conv_ffn(h, ln_ffn[1], w_up1, f_conv_w[1], f_conv_b[1], f_w_down, 1, seq)
    return h.reshape(batch, seq, d)
```

```python
import functools
import math

import jax
import jax.numpy as jnp
from jax import lax
from jax.experimental import pallas as pl
from jax.experimental.pallas import tpu as pltpu

F32 = jnp.float32
BF16 = jnp.bfloat16

N_HEADS = 8
HEAD_DIM = 128
V_DIM = 2 * HEAD_DIM
SW_DILATIONS = (1, 4, 16)
SW_STEPS = 128
NUM_BUCKETS = 32
MAX_DISTANCE = 2048
EPS = 1e-6
NEG = -1e30

LANES = 128
COL_TILE = 256
ATTN_TILE = 256
CONV_HALO = 16
VMEM_LIMIT = 60 * 1024 * 1024


def _tiles():
    return dict(tm=1024, tn_qkv=1024, qkv_rows_chunk=256, tn_up=512, tm_down=512, down_chunk=256,
                norm_chunk=256, tm_attn_out=512, attn_out_chunk=256)


def _params(*sem):
    return pltpu.CompilerParams(dimension_semantics=sem, vmem_limit_bytes=VMEM_LIMIT)


def _rmsnorm_rows(x_ref, g_ref, out_ref, row0, rows, chunk):
    g = g_ref[...]
    for r in range(0, rows, chunk):
        x = x_ref[r:r + chunk, :]
        rs = lax.rsqrt(jnp.mean(x * x, axis=-1, keepdims=True) + EPS)
        out_ref[row0 + r:row0 + r + chunk, :] = (x * rs * g).astype(BF16)


def _t5_bucket(dist):
    max_exact = NUM_BUCKETS // 2
    n = jnp.maximum(dist, 0)
    nf = jnp.maximum(n, 1).astype(F32)
    large = max_exact + (jnp.log(nf / max_exact) / math.log(MAX_DISTANCE / max_exact)
                         * (NUM_BUCKETS - max_exact)).astype(jnp.int32)
    large = jnp.minimum(large, NUM_BUCKETS - 1)
    return jnp.where(n < max_exact, n, large)


def _bucket_range(dmin, dmax):
    def bucket(n):
        n = max(n, 0)
        if n < NUM_BUCKETS // 2:
            return n
        half = NUM_BUCKETS // 2
        return min(half + int(math.log(n / half) / math.log(MAX_DISTANCE / half) * half), NUM_BUCKETS - 1)
    return max(bucket(dmin) - 1, 0), min(bucket(dmax) + 1, NUM_BUCKETS - 1)


def _bias_lookup(rb_ref, bucket, h, bmin, bmax):
    val = jnp.full(bucket.shape, rb_ref[bmin, h], F32)
    for b in range(bmin + 1, bmax + 1):
        val = jnp.where(bucket == b, rb_ref[b, h], val)
    return val


def _diff_bias_kernel(rb_ref, wsrc_ref, o_ref, wdst_ref):
    wdst_ref[...] = wsrc_ref[...].astype(BF16)
    h = pl.program_id(0)
    t = ATTN_TILE
    delta = (lax.broadcasted_iota(jnp.int32, (t, t), 0) - lax.broadcasted_iota(jnp.int32, (t, t), 1))
    for d in range(o_ref.shape[1]):
        dist = d * t + delta
        val = _bias_lookup(rb_ref, _t5_bucket(dist), h, *_bucket_range(d * t - (t - 1), d * t + (t - 1)))
        o_ref[0, d] = jnp.where(dist >= 0, val, NEG) if d == 0 else val


def _diff_bias_table(rel_bias, seq, w_cast, w_cast_layer):
    nd = seq // ATTN_TILE
    w_in, w_out, w_shape = _cast_side_job(w_cast, w_cast_layer, N_HEADS, lambda h: h)
    return pl.pallas_call(
        _diff_bias_kernel,
        out_shape=(jax.ShapeDtypeStruct((N_HEADS, nd, ATTN_TILE, ATTN_TILE), F32), w_shape),
        grid=(N_HEADS,),
        in_specs=[pl.BlockSpec(memory_space=pltpu.SMEM), w_in],
        out_specs=(pl.BlockSpec((1, nd, ATTN_TILE, ATTN_TILE), lambda h: (h, 0, 0, 0)), w_out),
        compiler_params=_params("parallel"),
        name="diff_bias_table",
    )(rel_bias, w_cast)


def _dil_bias_kernel(rb_ref, o_ref):
    h = pl.program_id(0)
    v = pl.program_id(1)
    a = lax.broadcasted_iota(jnp.int32, (SW_STEPS, 2 * SW_STEPS), 0)
    b = lax.broadcasted_iota(jnp.int32, (SW_STEPS, 2 * SW_STEPS), 1)
    dist_sub = a + SW_STEPS * (1 - v) - b
    in_window = (dist_sub >= 0) & (dist_sub <= SW_STEPS)
    for g, dil in enumerate(SW_DILATIONS):
        dist = jnp.clip(dist_sub, 0, SW_STEPS) * dil
        val = _bias_lookup(rb_ref, _t5_bucket(dist), h, *_bucket_range(0, SW_STEPS * dil))
        o_ref[g, 0, 0] = jnp.where(in_window, val, NEG)


def _dil_bias_table(rel_bias):
    ng = len(SW_DILATIONS)
    return pl.pallas_call(
        _dil_bias_kernel,
        out_shape=jax.ShapeDtypeStruct((ng, N_HEADS, 2, SW_STEPS, 2 * SW_STEPS), F32),
        grid=(N_HEADS, 2),
        in_specs=[pl.BlockSpec(memory_space=pltpu.SMEM)],
        out_specs=pl.BlockSpec((ng, 1, 1, SW_STEPS, 2 * SW_STEPS), lambda h, v: (0, h, v, 0, 0)),
        compiler_params=_params("parallel", "parallel"),
        name="dil_bias_table",
    )(rel_bias)


def _qkv_kernel(flag_ref, x_ref, g_ref, w_ref, cg_ref, *refs, tm, tn, chunk, dils, steps_per_group):
    o_refs = refs[:len(dils)]
    xn_ref, acc_ref = refs[len(dils):]
    j = pl.program_id(1)
    normed = flag_ref[j] != 0
    per_tile = COL_TILE // LANES
    rows_chunk = acc_ref.shape[1]

    def finish(o_ref, dil, acc, r0):
        n_rows = rows_chunk // dil
        out_rows = slice(r0 // dil, r0 // dil + n_rows)
        for c in range(tn // LANES):
            y_all = acc[:, c * LANES:(c + 1) * LANES]
            if dil > 1:
                acc_ref[c] = y_all
            for r in range(dil):
                y = y_all if dil == 1 else acc_ref[c, pl.ds(r, n_rows, stride=dil), :]
                rs = lax.rsqrt(jnp.mean(y * y, axis=-1, keepdims=True) + EPS)
                scale = jnp.where(normed, rs, 1.0)
                lo = r * COL_TILE + (c % per_tile) * LANES
                o_ref[c // per_tile, out_rows, lo:lo + LANES] = (
                    y * scale * cg_ref[:, c * LANES:(c + 1) * LANES]).astype(BF16)

    def project_group(o_ref, dil, first_step):
        wb = w_ref[...].astype(BF16)
        starts = list(range(0, tm, rows_chunk))

        def project(r0):
            if first_step:
                _rmsnorm_rows(x_ref.at[r0:r0 + rows_chunk], g_ref, xn_ref, r0, rows_chunk, rows_chunk)
            return jnp.dot(xn_ref[r0:r0 + rows_chunk, :], wb, preferred_element_type=F32)

        acc_next = project(starts[0])
        for i, r0 in enumerate(starts):
            acc = acc_next
            if i + 1 < len(starts):
                acc_next = project(starts[i + 1])
            finish(o_ref, dil, acc, r0)

    pl.when(j == 0)(functools.partial(project_group, o_refs[0], dils[0], True))
    for g, dil in enumerate(dils):
        in_group = (j // steps_per_group == g) & (j > 0)
        pl.when(in_group)(functools.partial(project_group, o_refs[g], dil, False))


def _qkv_proj(x2d, ln_g, w, col_gain, tile_flags, dils):
    t = _tiles()
    tm, tn = t["tm"], t["tn_qkv"]
    m, d = x2d.shape
    n = w.shape[2]
    ng = len(dils)
    steps_per_group = n // ng // tn
    group_tiles = n // ng // COL_TILE
    kern = functools.partial(_qkv_kernel, tm=tm, tn=tn, chunk=t["norm_chunk"], dils=dils,
                             steps_per_group=steps_per_group)

    def out_spec(g, dil):
        def index(i, j, f):
            return (jnp.clip(j - g * steps_per_group, 0, steps_per_group - 1), i, 0)
        return pl.BlockSpec((tn // COL_TILE, tm // dil, dil * COL_TILE), index)

    return pl.pallas_call(
        kern,
        out_shape=[jax.ShapeDtypeStruct((group_tiles, m // dil, dil * COL_TILE), BF16) for dil in dils],
        grid_spec=pltpu.PrefetchScalarGridSpec(
            num_scalar_prefetch=1,
            grid=(m // tm, n // tn),
            in_specs=[
                pl.BlockSpec((tm, d), lambda i, j, f: (i, 0)),
                pl.BlockSpec((1, d), lambda i, j, f: (0, 0)),
                pl.BlockSpec((None, d, tn), lambda i, j, f: (0, 0, j)),
                pl.BlockSpec((1, tn), lambda i, j, f: (0, j)),
            ],
            out_specs=[out_spec(g, dil) for g, dil in enumerate(dils)],
            scratch_shapes=[pltpu.VMEM((tm, d), BF16),
                            pltpu.VMEM((tn // LANES, t["qkv_rows_chunk"], LANES), F32)],
        ),
        compiler_params=_params("parallel", "arbitrary"),
        name="qkv_proj",
    )(tile_flags, x2d, ln_g.reshape(1, d), w, col_gain.reshape(1, n))


def _diff_attn_kernel(q_ref, k_ref, v_ref, bias_ref, lam_ref, sg_ref, wsrc_ref, o_ref, wdst_ref, *, lam_init, seq):
    wdst_ref[...] = wsrc_ref[...].astype(BF16)
    t = ATTN_TILE
    lam = lam_ref[...]
    lam_full = (jnp.exp(jnp.sum(lam[0:1] * lam[1:2], axis=-1, keepdims=True))
                - jnp.exp(jnp.sum(lam[2:3] * lam[3:4], axis=-1, keepdims=True)) + lam_init)
    gain = sg_ref[...] * (1.0 - lam_init)

    def logits(c, mp):
        width = (c + 1) * t
        lo = mp * HEAD_DIM
        bias = jnp.concatenate([bias_ref[0, c - ki] for ki in range(c + 1)], axis=1)
        return lax.dot_general(q_ref[0, c * t:(c + 1) * t, lo:lo + HEAD_DIM], k_ref[0, 0:width, lo:lo + HEAD_DIM],
                               (((1,), (1,)), ((), ())), preferred_element_type=F32) + bias

    def attend(c, s):
        p = jnp.exp(s - jnp.max(s, axis=-1, keepdims=True))
        den = jnp.sum(p, axis=-1, keepdims=True)
        return jnp.dot(p.astype(BF16), v_ref[0, 0:(c + 1) * t, :], preferred_element_type=F32) / den

    units = [(c, mp) for c in reversed(range(seq // t)) for mp in range(2)]
    s_next = logits(*units[0])
    maps = {}
    for i, (c, mp) in enumerate(units):
        s_cur = s_next
        if i + 1 < len(units):
            s_next = logits(*units[i + 1])
        maps[mp] = attend(c, s_cur)
        if mp == 1:
            o = maps[0] - lam_full * maps[1]
            rs = lax.rsqrt(jnp.mean(o * o, axis=-1, keepdims=True) + EPS)
            o_ref[c * t:(c + 1) * t, :] = (o * rs * gain).astype(BF16)


def _cast_side_job(w, layer, n_steps, step_of):
    _, rows, cols = w.shape
    packed_rows = 16
    slab = packed_rows * pl.cdiv(rows, packed_rows * n_steps)
    n_slabs = rows // slab
    assert n_slabs * slab == rows and n_slabs <= n_steps

    def slab_of(*g):
        return jnp.minimum(step_of(*g), n_slabs - 1)

    in_spec = pl.BlockSpec((None, slab, cols), lambda *g: (layer, slab_of(*g), 0))
    out_spec = pl.BlockSpec((slab, cols), lambda *g: (slab_of(*g), 0))
    return in_spec, out_spec, jax.ShapeDtypeStruct((rows, cols), BF16)


def _diff_attention(qkv, bias_tab, lam, subln_g, batch, seq, lam_init, w_cast, w_cast_layer):
    t = ATTN_TILE
    nq = seq // t
    m = batch * seq
    kern = functools.partial(_diff_attn_kernel, lam_init=lam_init, seq=seq)
    w_in, w_out, w_shape = _cast_side_job(w_cast, w_cast_layer, N_HEADS * batch, lambda h, b: h * batch + b)
    return pl.pallas_call(
        kern,
        out_shape=(jax.ShapeDtypeStruct((m, N_HEADS * V_DIM), BF16), w_shape),
        grid=(N_HEADS, batch),
        in_specs=[
            pl.BlockSpec((1, seq, COL_TILE), lambda h, b: (h, b, 0)),
            pl.BlockSpec((1, seq, COL_TILE), lambda h, b: (N_HEADS + h, b, 0)),
            pl.BlockSpec((1, seq, COL_TILE), lambda h, b: (2 * N_HEADS + h, b, 0)),
            pl.BlockSpec((1, nq, t, t), lambda h, b: (h, 0, 0, 0)),
            pl.BlockSpec((4, HEAD_DIM), lambda h, b: (0, 0)),
            pl.BlockSpec((1, V_DIM), lambda h, b: (0, 0)),
            w_in,
        ],
        out_specs=(pl.BlockSpec((seq, V_DIM), lambda h, b: (b, h)), w_out),
        compiler_params=_params("parallel", "parallel"),
        name="diff_attention",
    )(qkv, qkv, qkv, bias_tab, lam, subln_g.reshape(1, V_DIM), w_cast)


def _dil_attn_kernel(q_ref, k_ref, v_ref, bias_ref, o_ref, lse_ref, *scratch, sub_len, rows, dil, rps):
    blk = SW_STEPS
    c = pl.program_id(2)
    nblk = rows // blk
    single = sub_len == blk
    lane = lax.broadcasted_iota(jnp.int32, (blk, LANES), 1)

    def window(n):
        ng = c * nblk + n
        if single:
            return 0, 1, blk
        return pl.multiple_of(jnp.maximum(ng - 1, 0) * blk, blk), (ng == 0).astype(jnp.int32), 2 * blk

    def logits(rr, n, h):
        start, variant, width = window(n)
        tile, lo = h // 2, rr * COL_TILE + (h % 2) * HEAD_DIM
        qt = q_ref[tile, n * blk:(n + 1) * blk, lo:lo + HEAD_DIM]
        kw = k_ref[tile, pl.ds(start, width), lo:lo + HEAD_DIM]
        bias = bias_ref[0, h, variant][:, :width]
        return lax.dot_general(qt, kw, (((1,), (1,)), ((), ())), preferred_element_type=F32) + bias

    def attend(rr, n, h, s):
        start, _, width = window(n)
        mx = jnp.max(s, axis=-1, keepdims=True)
        p = jnp.exp(s - mx)
        den = jnp.sum(p, axis=-1, keepdims=True)
        vw = v_ref[h, pl.ds(start, width), rr * COL_TILE:(rr + 1) * COL_TILE]
        o = jnp.dot(p.astype(BF16), vw, preferred_element_type=F32) / den
        return o, mx + jnp.log(den)

    units = [(rr, n, h) for rr in range(rps) for n in range(nblk) for h in range(N_HEADS)]
    ahead = 2 if nblk == 1 else 1
    queue = [logits(*u) for u in units[:ahead]]
    lse_tile = None
    for i, (rr, n, h) in enumerate(units):
        s = queue.pop(0)
        if i + ahead < len(units):
            queue.append(logits(*units[i + ahead]))
        o, lse = attend(rr, n, h, s)
        if dil == 1:
            out_rows = slice(n * blk, (n + 1) * blk)
            o_ref[out_rows, h * V_DIM:(h + 1) * V_DIM] = o.astype(BF16)
        else:
            r = pl.program_id(1) * rps + rr
            out_rows = pl.ds(r + (c * nblk + n) * (blk * dil), blk, stride=dil)
            for part in range(V_DIM // LANES):
                scratch[0][h * (V_DIM // LANES) + part, out_rows, :] = o[:, part * LANES:(part + 1) * LANES]
        lse_tile = jnp.where(lane == h, lse, jnp.zeros((blk, LANES), F32) if h == 0 else lse_tile)
        if h == N_HEADS - 1:
            (lse_ref if dil == 1 else scratch[1])[out_rows, :] = lse_tile

    if dil > 1:
        @pl.when((pl.program_id(1) == pl.num_programs(1) - 1) & (c == pl.num_programs(2) - 1))
        def _():
            for slab in range(scratch[0].shape[0]):
                o_ref[:, slab * LANES:(slab + 1) * LANES] = scratch[0][slab].astype(BF16)
            lse_ref[...] = scratch[1][...]


def _dilated_group_attention(qkv, bias_tab, g, batch, seq):
    dil = SW_DILATIONS[g]
    sub_len = seq // dil
    rows = min(sub_len, 512)
    nc = sub_len // rows
    rps = max(1, 512 // sub_len)
    m = batch * seq
    width = N_HEADS * V_DIM
    kern = functools.partial(_dil_attn_kernel, sub_len=sub_len, rows=rows, dil=dil, rps=rps)
    if dil == 1:
        out_specs = (pl.BlockSpec((rows, width), lambda b, r, c: (b * nc + c, 0)),
                     pl.BlockSpec((rows, LANES), lambda b, r, c: (b * nc + c, 0)))
        scratch, sem = [], ("parallel", "parallel", "parallel")
    else:
        out_specs = (pl.BlockSpec((seq, width), lambda b, r, c: (b, 0)),
                     pl.BlockSpec((seq, LANES), lambda b, r, c: (b, 0)))
        scratch = [pltpu.VMEM((width // LANES, seq, LANES), F32), pltpu.VMEM((seq, LANES), F32)]
        sem = ("parallel", "arbitrary", "arbitrary")
    return pl.pallas_call(
        kern,
        out_shape=(jax.ShapeDtypeStruct((m, width), BF16), jax.ShapeDtypeStruct((m, LANES), F32)),
        grid=(batch, dil // rps, nc),
        in_specs=[
            pl.BlockSpec((4, rows, rps * COL_TILE), lambda b, r, c: (0, b * nc + c, r)),
            pl.BlockSpec((4, sub_len, rps * COL_TILE), lambda b, r, c: (1, b, r)),
            pl.BlockSpec((8, sub_len, rps * COL_TILE), lambda b, r, c: (1, b, r)),
            pl.BlockSpec((1, N_HEADS, 2, SW_STEPS, 2 * SW_STEPS), lambda b, r, c: (g, 0, 0, 0, 0)),
        ],
        out_specs=out_specs,
        scratch_shapes=scratch,
        compiler_params=_params(*sem),
        name=f"dilated_attention_g{g}",
    )(qkv, qkv, qkv, bias_tab)


def _ffn_down_kernel(a_ref, w_ref, r_ref, o_ref, *, tm, chunk):
    for r in range(0, tm, chunk):
        o_ref[r:r + chunk, :] = r_ref[r:r + chunk, :] + jnp.dot(a_ref[r:r + chunk, :], w_ref[...],
                                                                preferred_element_type=F32)


def _ffn_down(a, w, res):
    t = _tiles()
    tm, chunk = t["tm_down"], t["down_chunk"]
    m, k = a.shape
    n = w.shape[1]
    return pl.pallas_call(
        functools.partial(_ffn_down_kernel, tm=tm, chunk=chunk),
        out_shape=jax.ShapeDtypeStruct((m, n), F32),
        grid=(m // tm,),
        in_specs=[
            pl.BlockSpec((tm, k), lambda i: (i, 0)),
            pl.BlockSpec((k, n), lambda i: (0, 0), pipeline_mode=pl.Buffered(1)),
            pl.BlockSpec((tm, n), lambda i: (i, 0)),
        ],
        out_specs=pl.BlockSpec((tm, n), lambda i: (i, 0)),
        compiler_params=_params("parallel"),
        name="ffn_down",
    )(a, w, res)


def _attn_out_kernel(*refs, n_groups, tm, chunk):
    o_refs = refs[:n_groups]
    l_refs = refs[n_groups:2 * n_groups] if n_groups > 1 else ()
    w_ref, r_ref, out_ref, wb_ref = refs[-4:]

    @pl.when(pl.program_id(0) == 0)
    def _():
        wb_ref[...] = w_ref[...].astype(BF16)

    def lhs(r):
        if n_groups == 1:
            return o_refs[0][r:r + chunk, :]
        ls = [l_ref[r:r + chunk, :] for l_ref in l_refs]
        mx = functools.reduce(jnp.maximum, ls)
        es = [jnp.exp(l - mx) for l in ls]
        den = functools.reduce(jnp.add, es)
        alphas = [e / den for e in es]
        heads = []
        for h in range(N_HEADS):
            cols = slice(h * V_DIM, (h + 1) * V_DIM)
            terms = [al[:, h:h + 1] * o_ref[r:r + chunk, cols].astype(F32) for al, o_ref in zip(alphas, o_refs)]
            heads.append(functools.reduce(jnp.add, terms).astype(BF16))
        return jnp.concatenate(heads, axis=1)

    starts = list(range(0, tm, chunk))
    a_next = lhs(starts[0])
    for i, r in enumerate(starts):
        a_cur = a_next
        if i + 1 < len(starts):
            a_next = lhs(starts[i + 1])
        out_ref[r:r + chunk, :] = r_ref[r:r + chunk, :] + jnp.dot(a_cur, wb_ref[...], preferred_element_type=F32)


def _attn_out_proj(outs, lses, w, res):
    t = _tiles()
    tm, chunk = t["tm_attn_out"], t["attn_out_chunk"]
    m, k = outs[0].shape
    n = w.shape[2]
    kern = functools.partial(_attn_out_kernel, n_groups=len(outs), tm=tm, chunk=chunk)
    o_spec = pl.BlockSpec((tm, k), lambda i: (i, 0))
    l_spec = pl.BlockSpec((tm, LANES), lambda i: (i, 0))
    return pl.pallas_call(
        kern,
        out_shape=jax.ShapeDtypeStruct((m, n), F32),
        grid=(m // tm,),
        in_specs=[o_spec] * len(outs) + [l_spec] * len(lses) + [
            pl.BlockSpec((None, k, n), lambda i: (0, 0, 0), pipeline_mode=pl.Buffered(1)),
            pl.BlockSpec((tm, n), lambda i: (i, 0))],
        out_specs=pl.BlockSpec((tm, n), lambda i: (i, 0)),
        scratch_shapes=[pltpu.VMEM((k, n), BF16)],
        compiler_params=_params("arbitrary"),
        name="attn_out_proj",
    )(*outs, *lses, w, res)


def _ffn_up_kernel(x_ref, halo_ref, g_ref, wg0_ref, wgn_ref, wv_ref, cwg_ref, cwv_ref, cbg_ref, cbv_ref, *refs,
                   n_casts, tm, nj, tiles_per_seq, chunk):
    wsrc_refs = refs[:n_casts]
    o_ref = refs[n_casts]
    wdst_refs = refs[n_casts + 1:2 * n_casts + 1]
    xn_ref, ug_ref = refs[2 * n_casts + 1:]
    for wsrc_ref, wdst_ref in zip(wsrc_refs, wdst_refs):
        wdst_ref[...] = wsrc_ref[...].astype(BF16)
    i = pl.program_id(0)
    j = pl.program_id(1)
    halo = CONV_HALO

    @pl.when(j == 0)
    def _():
        first = (i % tiles_per_seq) == 0

        @pl.when(first)
        def _():
            xn_ref[0:halo, :] = jnp.zeros((halo, xn_ref.shape[1]), BF16)

        @pl.when(jnp.logical_not(first))
        def _():
            _rmsnorm_rows(halo_ref, g_ref, xn_ref, 0, halo, halo)

        wb = wg0_ref[...].astype(BF16)
        starts = list(range(0, tm, chunk))
        _rmsnorm_rows(x_ref.at[0:chunk], g_ref, xn_ref, halo, chunk, chunk)
        for k, r0 in enumerate(starts):
            if k + 1 < len(starts):
                nxt = starts[k + 1]
                _rmsnorm_rows(x_ref.at[nxt:nxt + chunk], g_ref, xn_ref, halo + nxt, chunk, chunk)
            lo = 0 if r0 == 0 else halo + r0
            hi = halo + r0 + chunk
            ug_ref[lo:hi, :] = jnp.dot(xn_ref[lo:hi, :], wb, preferred_element_type=F32)

    def conv(u, cw_ref, cb_ref):
        cw = cw_ref[...]
        return (cw[0:1] * u[halo - 2:halo - 2 + tm] + cw[1:2] * u[halo - 1:halo - 1 + tm]
                + cw[2:3] * u[halo:halo + tm]) + cb_ref[...]

    def step(project_next_gate):
        xn = xn_ref[...]
        uv = jnp.dot(xn, wv_ref[...].astype(BF16), preferred_element_type=F32)
        gate = conv(ug_ref[...], cwg_ref, cbg_ref)
        if project_next_gate:
            ug_ref[...] = jnp.dot(xn, wgn_ref[...].astype(BF16), preferred_element_type=F32)
        val = conv(uv, cwv_ref, cbv_ref)
        o_ref[...] = (gate / (1.0 + jnp.exp(-gate)) * val).astype(BF16)

    pl.when(j < nj - 1)(functools.partial(step, True))
    pl.when(j == nj - 1)(functools.partial(step, False))


def _ffn_up(h2d, ln_g, w_up, layer, conv_w, conv_b, seq, casts):
    t = _tiles()
    tm, tn = t["tm"], t["tn_up"]
    m, d = h2d.shape
    dff = w_up.shape[2] // 2
    nj = dff // tn
    halo_blocks = tm // CONV_HALO
    kern = functools.partial(_ffn_up_kernel, n_casts=len(casts), tm=tm, nj=nj, tiles_per_seq=seq // tm,
                             chunk=t["norm_chunk"])
    jobs = [_cast_side_job(w, lyr, (m // tm) * nj, lambda i, j: i * nj + j) for w, lyr in casts]
    return pl.pallas_call(
        kern,
        out_shape=(jax.ShapeDtypeStruct((m, dff), BF16), *[job[2] for job in jobs]),
        grid=(m // tm, nj),
        in_specs=[
            pl.BlockSpec((tm, d), lambda i, j: (i, 0)),
            pl.BlockSpec((CONV_HALO, d), lambda i, j: (jnp.maximum(i * halo_blocks - 1, 0), 0)),
            pl.BlockSpec((1, d), lambda i, j: (0, 0)),
            pl.BlockSpec((None, d, tn), lambda i, j: (layer, 0, 0), pipeline_mode=pl.Buffered(1)),
            pl.BlockSpec((None, d, tn), lambda i, j: (layer, 0, jnp.minimum(j + 1, nj - 1))),
            pl.BlockSpec((None, d, tn), lambda i, j: (layer, 0, j + nj)),
            pl.BlockSpec((3, tn), lambda i, j: (0, j)),
            pl.BlockSpec((3, tn), lambda i, j: (0, j + nj)),
            pl.BlockSpec((1, tn), lambda i, j: (0, j)),
            pl.BlockSpec((1, tn), lambda i, j: (0, j + nj)),
            *[job[0] for job in jobs],
        ],
        out_specs=(pl.BlockSpec((tm, tn), lambda i, j: (i, j)), *[job[1] for job in jobs]),
        scratch_shapes=[pltpu.VMEM((tm + CONV_HALO, d), BF16), pltpu.VMEM((tm + CONV_HALO, tn), F32)],
        compiler_params=_params("parallel", "arbitrary"),
        name="ffn_up",
    )(h2d, h2d, ln_g.reshape(1, d), w_up, w_up, w_up, conv_w, conv_w, conv_b.reshape(1, -1), conv_b.reshape(1, -1),
      *[w for w, _ in casts])


def _conv_ffn(h2d, ln_g, w_up_bf16, conv_w, conv_b, w_down, layer, seq, extra_casts=()):
    act, w_down_bf16, *extra = _ffn_up(h2d, ln_g, w_up_bf16[None], 0, conv_w, conv_b, seq,
                                       [(w_down, layer), *extra_casts])
    return _ffn_down(act, w_down_bf16, h2d), extra


def _diff_col_gain(q_gain, k_gain):
    scale = HEAD_DIM ** -0.5
    nqk = 2 * N_HEADS
    gain = jnp.concatenate([jnp.tile(q_gain * scale, nqk), jnp.tile(k_gain, nqk),
                            jnp.ones((N_HEADS * V_DIM,), F32)])
    return gain


def _dil_col_gain(q_gain, k_gain):
    scale = HEAD_DIM ** -0.5
    parts = []
    for g in range(len(SW_DILATIONS)):
        parts += [jnp.tile(q_gain[g] * scale, N_HEADS), jnp.tile(k_gain[g], N_HEADS),
                  jnp.ones((N_HEADS * V_DIM,), F32)]
    return jnp.concatenate(parts)


def _tile_flags(n_cols, tn, normed_ranges):
    flags = []
    for j in range(n_cols // tn):
        lo = j * tn
        flags.append(int(any(a <= lo < b for a, b in normed_ranges)))
    return jnp.asarray(flags, jnp.int32)


def kernel(x, rel_bias, ln_mix, ln_ffn, a_w_qkv, a_q_norm, a_k_norm, a_lambda, a_subln, a_w_o,
           b_w_qkv, b_q_norm, b_k_norm, b_w_o, f_w_up, f_conv_w, f_conv_b, f_w_down):
    batch, seq, d = x.shape
    m = batch * seq
    t = _tiles()
    h = x.reshape(m, d)

    nqk = 2 * N_HEADS * HEAD_DIM
    flags0 = _tile_flags(a_w_qkv.shape[2], t["tn_qkv"], [(0, 2 * nqk)])
    diff_bias, w_qkv0 = _diff_bias_table(rel_bias, seq, a_w_qkv, 0)
    (qkv0,) = _qkv_proj(h, ln_mix[0], w_qkv0[None], _diff_col_gain(a_q_norm[0], a_k_norm[0]), flags0, (1,))
    lam_init = 0.8 - 0.6 * math.exp(-0.3 * 0)
    attn0, w_up0 = _diff_attention(qkv0, diff_bias, a_lambda[0], a_subln[0], batch, seq, lam_init, f_w_up, 0)
    h = _attn_out_proj([attn0], [], a_w_o, h)
    h, (w_up1, w_qkv1) = _conv_ffn(h, ln_ffn[0], w_up0, f_conv_w[0], f_conv_b[0], f_w_down, 0, seq,
                                   extra_casts=[(f_w_up, 1), (b_w_qkv, 0)])

    gcols = 2 * N_HEADS * HEAD_DIM + N_HEADS * V_DIM
    nq1 = N_HEADS * HEAD_DIM
    flags1 = _tile_flags(b_w_qkv.shape[2], t["tn_qkv"],
                         [(g * gcols, g * gcols + 2 * nq1) for g in range(len(SW_DILATIONS))])
    qkv1 = _qkv_proj(h, ln_mix[1], w_qkv1[None], _dil_col_gain(b_q_norm[0], b_k_norm[0]), flags1, SW_DILATIONS)
    dil_bias = _dil_bias_table(rel_bias)
    outs, lses = [], []
    for g in range(len(SW_DILATIONS)):
        o, lse = _dilated_group_attention(qkv1[g], dil_bias, g, batch, seq)
        outs.append(o)
        lses.append(lse)
    h = _attn_out_proj(outs, lses, b_w_o, h)
    h, _ = _conv_ffn(h, ln_ffn[1], w_up1, f_conv_w[1], f_conv_b[1], f_w_down, 1, seq)
    return h.reshape(batch, seq, d)
```

```python
import functools
import math

import jax
import jax.numpy as jnp
from jax import lax
from jax.experimental import pallas as pl
from jax.experimental.pallas import tpu as pltpu

F32 = jnp.float32
BF16 = jnp.bfloat16

N_HEADS = 8
HEAD_DIM = 128
V_DIM = 2 * HEAD_DIM
SW_DILATIONS = (1, 4, 16)
SW_STEPS = 128
NUM_BUCKETS = 32
MAX_DISTANCE = 2048
EPS = 1e-6
NEG = -1e30

LANES = 128
COL_TILE = 256
ATTN_TILE = 256
CONV_HALO = 16
VMEM_LIMIT = 60 * 1024 * 1024


def _tiles():
    return dict(tm=1024, tn_qkv=1024, qkv_rows_chunk=256, tn_up=512, tm_down=512, down_chunk=256,
                norm_chunk=256, tm_attn_out=512, attn_out_chunk=256)


def _params(*sem):
    return pltpu.CompilerParams(dimension_semantics=sem, vmem_limit_bytes=VMEM_LIMIT)


def _rmsnorm_rows(x_ref, g_ref, out_ref, row0, rows, chunk):
    g = g_ref[...]
    for r in range(0, rows, chunk):
        x = x_ref[r:r + chunk, :]
        rs = lax.rsqrt(jnp.mean(x * x, axis=-1, keepdims=True) + EPS)
        out_ref[row0 + r:row0 + r + chunk, :] = (x * rs * g).astype(BF16)


def _t5_bucket(dist):
    max_exact = NUM_BUCKETS // 2
    n = jnp.maximum(dist, 0)
    nf = jnp.maximum(n, 1).astype(F32)
    large = max_exact + (jnp.log(nf / max_exact) / math.log(MAX_DISTANCE / max_exact)
                         * (NUM_BUCKETS - max_exact)).astype(jnp.int32)
    large = jnp.minimum(large, NUM_BUCKETS - 1)
    return jnp.where(n < max_exact, n, large)


def _bucket_range(dmin, dmax):
    def bucket(n):
        n = max(n, 0)
        if n < NUM_BUCKETS // 2:
            return n
        half = NUM_BUCKETS // 2
        return min(half + int(math.log(n / half) / math.log(MAX_DISTANCE / half) * half), NUM_BUCKETS - 1)
    return max(bucket(dmin) - 1, 0), min(bucket(dmax) + 1, NUM_BUCKETS - 1)


def _bias_lookup(rb_ref, bucket, h, bmin, bmax):
    val = jnp.full(bucket.shape, rb_ref[bmin, h], F32)
    for b in range(bmin + 1, bmax + 1):
        val = jnp.where(bucket == b, rb_ref[b, h], val)
    return val


def _diff_bias_kernel(rb_ref, wsrc_ref, o_ref, wdst_ref):
    wdst_ref[...] = wsrc_ref[...].astype(BF16)
    h = pl.program_id(0)
    t = ATTN_TILE
    delta = (lax.broadcasted_iota(jnp.int32, (t, t), 0) - lax.broadcasted_iota(jnp.int32, (t, t), 1))
    for d in range(o_ref.shape[1]):
        dist = d * t + delta
        val = _bias_lookup(rb_ref, _t5_bucket(dist), h, *_bucket_range(d * t - (t - 1), d * t + (t - 1)))
        o_ref[0, d] = jnp.where(dist >= 0, val, NEG) if d == 0 else val


def _diff_bias_table(rel_bias, seq, w_cast, w_cast_layer):
    nd = seq // ATTN_TILE
    w_in, w_out, w_shape = _cast_side_job(w_cast, w_cast_layer, N_HEADS, lambda h: h)
    return pl.pallas_call(
        _diff_bias_kernel,
        out_shape=(jax.ShapeDtypeStruct((N_HEADS, nd, ATTN_TILE, ATTN_TILE), F32), w_shape),
        grid=(N_HEADS,),
        in_specs=[pl.BlockSpec(memory_space=pltpu.SMEM), w_in],
        out_specs=(pl.BlockSpec((1, nd, ATTN_TILE, ATTN_TILE), lambda h: (h, 0, 0, 0)), w_out),
        compiler_params=_params("parallel"),
        name="diff_bias_table",
    )(rel_bias, w_cast)


def _dil_bias_kernel(rb_ref, o_ref):
    h = pl.program_id(0)
    a = lax.broadcasted_iota(jnp.int32, (SW_STEPS, 2 * SW_STEPS), 0)
    b = lax.broadcasted_iota(jnp.int32, (SW_STEPS, 2 * SW_STEPS), 1)
    dist_sub = a + SW_STEPS - b
    in_window = (dist_sub >= 0) & (dist_sub <= SW_STEPS)
    masked = jnp.full((SW_STEPS, SW_STEPS), NEG, F32)
    for g, dil in enumerate(SW_DILATIONS):
        dist = jnp.clip(dist_sub, 0, SW_STEPS) * dil
        val = _bias_lookup(rb_ref, _t5_bucket(dist), h, *_bucket_range(0, SW_STEPS * dil))
        tile = jnp.where(in_window, val, NEG)
        o_ref[g, 0, 0] = tile
        o_ref[g, 0, 1] = jnp.concatenate([tile[:, SW_STEPS:], masked], axis=1)


def _dil_bias_table(rel_bias):
    ng = len(SW_DILATIONS)
    return pl.pallas_call(
        _dil_bias_kernel,
        out_shape=jax.ShapeDtypeStruct((ng, N_HEADS, 2, SW_STEPS, 2 * SW_STEPS), F32),
        grid=(N_HEADS,),
        in_specs=[pl.BlockSpec(memory_space=pltpu.SMEM)],
        out_specs=pl.BlockSpec((ng, 1, 2, SW_STEPS, 2 * SW_STEPS), lambda h: (0, h, 0, 0, 0)),
        compiler_params=_params("parallel"),
        name="dil_bias_table",
    )(rel_bias)


def _qkv_kernel(flag_ref, x_ref, g_ref, w_ref, cg_ref, *refs, tm, tn, chunk, dils, steps_per_group):
    o_refs = refs[:len(dils)]
    xn_ref, acc_ref = refs[len(dils):]
    j = pl.program_id(1)
    normed = flag_ref[j] != 0
    per_tile = COL_TILE // LANES
    rows_chunk = acc_ref.shape[1]

    def finish(o_ref, dil, acc, r0):
        n_rows = rows_chunk // dil
        out_rows = slice(r0 // dil, r0 // dil + n_rows)
        for c in range(tn // LANES):
            y_all = acc[:, c * LANES:(c + 1) * LANES]
            if dil > 1:
                acc_ref[c] = y_all
            for r in range(dil):
                y = y_all if dil == 1 else acc_ref[c, pl.ds(r, n_rows, stride=dil), :]
                rs = lax.rsqrt(jnp.mean(y * y, axis=-1, keepdims=True) + EPS)
                scale = jnp.where(normed, rs, 1.0)
                lo = r * COL_TILE + (c % per_tile) * LANES
                o_ref[c // per_tile, out_rows, lo:lo + LANES] = (
                    y * scale * cg_ref[:, c * LANES:(c + 1) * LANES]).astype(BF16)

    def project_group(o_ref, dil, first_step):
        wb = w_ref[...].astype(BF16)
        starts = list(range(0, tm, rows_chunk))

        def project(r0):
            if first_step:
                _rmsnorm_rows(x_ref.at[r0:r0 + rows_chunk], g_ref, xn_ref, r0, rows_chunk, rows_chunk)
            return jnp.dot(xn_ref[r0:r0 + rows_chunk, :], wb, preferred_element_type=F32)

        acc_next = project(starts[0])
        for i, r0 in enumerate(starts):
            acc = acc_next
            if i + 1 < len(starts):
                acc_next = project(starts[i + 1])
            finish(o_ref, dil, acc, r0)

    pl.when(j == 0)(functools.partial(project_group, o_refs[0], dils[0], True))
    for g, dil in enumerate(dils):
        in_group = (j // steps_per_group == g) & (j > 0)
        pl.when(in_group)(functools.partial(project_group, o_refs[g], dil, False))


def _qkv_proj(x2d, ln_g, w, col_gain, tile_flags, dils):
    t = _tiles()
    tm, tn = t["tm"], t["tn_qkv"]
    m, d = x2d.shape
    n = w.shape[2]
    ng = len(dils)
    steps_per_group = n // ng // tn
    group_tiles = n // ng // COL_TILE
    kern = functools.partial(_qkv_kernel, tm=tm, tn=tn, chunk=t["norm_chunk"], dils=dils,
                             steps_per_group=steps_per_group)

    def out_spec(g, dil):
        def index(i, j, f):
            return (jnp.clip(j - g * steps_per_group, 0, steps_per_group - 1), i, 0)
        return pl.BlockSpec((tn // COL_TILE, tm // dil, dil * COL_TILE), index)

    return pl.pallas_call(
        kern,
        out_shape=[jax.ShapeDtypeStruct((group_tiles, m // dil, dil * COL_TILE), BF16) for dil in dils],
        grid_spec=pltpu.PrefetchScalarGridSpec(
            num_scalar_prefetch=1,
            grid=(m // tm, n // tn),
            in_specs=[
                pl.BlockSpec((tm, d), lambda i, j, f: (i, 0)),
                pl.BlockSpec((1, d), lambda i, j, f: (0, 0)),
                pl.BlockSpec((None, d, tn), lambda i, j, f: (0, 0, j)),
                pl.BlockSpec((1, tn), lambda i, j, f: (0, j)),
            ],
            out_specs=[out_spec(g, dil) for g, dil in enumerate(dils)],
            scratch_shapes=[pltpu.VMEM((tm, d), BF16),
                            pltpu.VMEM((tn // LANES, t["qkv_rows_chunk"], LANES), F32)],
        ),
        compiler_params=_params("parallel", "arbitrary"),
        name="qkv_proj",
    )(tile_flags, x2d, ln_g.reshape(1, d), w, col_gain.reshape(1, n))


def _diff_attn_kernel(q_ref, k_ref, v_ref, bias_ref, lam_ref, sg_ref, wsrc_ref, o_ref, wdst_ref, *, lam_init, seq):
    wdst_ref[...] = wsrc_ref[...].astype(BF16)
    t = ATTN_TILE
    lam = lam_ref[...]
    lam_full = (jnp.exp(jnp.sum(lam[0:1] * lam[1:2], axis=-1, keepdims=True))
                - jnp.exp(jnp.sum(lam[2:3] * lam[3:4], axis=-1, keepdims=True)) + lam_init)
    gain = sg_ref[...] * (1.0 - lam_init)

    def logits(c, mp):
        width = (c + 1) * t
        lo = mp * HEAD_DIM
        bias = jnp.concatenate([bias_ref[0, c - ki] for ki in range(c + 1)], axis=1)
        return lax.dot_general(q_ref[0, c * t:(c + 1) * t, lo:lo + HEAD_DIM], k_ref[0, 0:width, lo:lo + HEAD_DIM],
                               (((1,), (1,)), ((), ())), preferred_element_type=F32) + bias

    def attend(c, s):
        p = jnp.exp(s - jnp.max(s, axis=-1, keepdims=True))
        den = jnp.sum(p, axis=-1, keepdims=True)
        return jnp.dot(p.astype(BF16), v_ref[0, 0:(c + 1) * t, :], preferred_element_type=F32) / den

    units = [(c, mp) for c in reversed(range(seq // t)) for mp in range(2)]
    s_next = logits(*units[0])
    maps = {}
    for i, (c, mp) in enumerate(units):
        s_cur = s_next
        if i + 1 < len(units):
            s_next = logits(*units[i + 1])
        maps[mp] = attend(c, s_cur)
        if mp == 1:
            o = maps[0] - lam_full * maps[1]
            rs = lax.rsqrt(jnp.mean(o * o, axis=-1, keepdims=True) + EPS)
            o_ref[c * t:(c + 1) * t, :] = (o * rs * gain).astype(BF16)


def _cast_side_job(w, layer, n_steps, step_of):
    _, rows, cols = w.shape
    packed_rows = 16
    slab = packed_rows * pl.cdiv(rows, packed_rows * n_steps)
    n_slabs = rows // slab
    assert n_slabs * slab == rows and n_slabs <= n_steps

    def slab_of(*g):
        return jnp.minimum(step_of(*g), n_slabs - 1)

    in_spec = pl.BlockSpec((None, slab, cols), lambda *g: (layer, slab_of(*g), 0))
    out_spec = pl.BlockSpec((slab, cols), lambda *g: (slab_of(*g), 0))
    return in_spec, out_spec, jax.ShapeDtypeStruct((rows, cols), BF16)


def _diff_attention(qkv, bias_tab, lam, subln_g, batch, seq, lam_init, w_cast, w_cast_layer):
    t = ATTN_TILE
    nq = seq // t
    m = batch * seq
    kern = functools.partial(_diff_attn_kernel, lam_init=lam_init, seq=seq)
    w_in, w_out, w_shape = _cast_side_job(w_cast, w_cast_layer, N_HEADS * batch, lambda h, b: h * batch + b)
    return pl.pallas_call(
        kern,
        out_shape=(jax.ShapeDtypeStruct((m, N_HEADS * V_DIM), BF16), w_shape),
        grid=(N_HEADS, batch),
        in_specs=[
            pl.BlockSpec((1, seq, COL_TILE), lambda h, b: (h, b, 0)),
            pl.BlockSpec((1, seq, COL_TILE), lambda h, b: (N_HEADS + h, b, 0)),
            pl.BlockSpec((1, seq, COL_TILE), lambda h, b: (2 * N_HEADS + h, b, 0)),
            pl.BlockSpec((1, nq, t, t), lambda h, b: (h, 0, 0, 0)),
            pl.BlockSpec((4, HEAD_DIM), lambda h, b: (0, 0)),
            pl.BlockSpec((1, V_DIM), lambda h, b: (0, 0)),
            w_in,
        ],
        out_specs=(pl.BlockSpec((seq, V_DIM), lambda h, b: (b, h)), w_out),
        compiler_params=_params("parallel", "parallel"),
        name="diff_attention",
    )(qkv, qkv, qkv, bias_tab, lam, subln_g.reshape(1, V_DIM), w_cast)


def _dil_attn_kernel(q_ref, k_ref, v_ref, bias_ref, o_ref, lse_ref, *scratch, sub_len, rows, dil, rps):
    blk = SW_STEPS
    c = pl.program_id(2)
    nblk = rows // blk
    single = sub_len == blk
    lane = lax.broadcasted_iota(jnp.int32, (blk, LANES), 1)

    def window(n):
        ng = c * nblk + n
        if single:
            return 0, 1, blk
        return pl.multiple_of(jnp.maximum(ng - 1, 0) * blk, blk), (ng == 0).astype(jnp.int32), 2 * blk

    def logits(rr, n, h):
        start, variant, width = window(n)
        tile, lo = h // 2, rr * COL_TILE + (h % 2) * HEAD_DIM
        qt = q_ref[tile, n * blk:(n + 1) * blk, lo:lo + HEAD_DIM]
        kw = k_ref[tile, pl.ds(start, width), lo:lo + HEAD_DIM]
        bias = bias_ref[0, h, variant][:, :width]
        return lax.dot_general(qt, kw, (((1,), (1,)), ((), ())), preferred_element_type=F32) + bias

    def attend(rr, n, h, s):
        start, _, width = window(n)
        mx = jnp.max(s, axis=-1, keepdims=True)
        p = jnp.exp(s - mx)
        den = jnp.sum(p, axis=-1, keepdims=True)
        vw = v_ref[h, pl.ds(start, width), rr * COL_TILE:(rr + 1) * COL_TILE]
        o = jnp.dot(p.astype(BF16), vw, preferred_element_type=F32) / den
        return o, mx + jnp.log(den)

    units = [(rr, n, h) for rr in range(rps) for n in range(nblk) for h in range(N_HEADS)]
    ahead = 2 if nblk == 1 else 1
    queue = [logits(*u) for u in units[:ahead]]
    lse_tile = None
    for i, (rr, n, h) in enumerate(units):
        s = queue.pop(0)
        if i + ahead < len(units):
            queue.append(logits(*units[i + ahead]))
        o, lse = attend(rr, n, h, s)
        if dil == 1:
            out_rows = slice(n * blk, (n + 1) * blk)
            o_ref[out_rows, h * V_DIM:(h + 1) * V_DIM] = o.astype(BF16)
        else:
            r = pl.program_id(1) * rps + rr
            out_rows = pl.ds(r + (c * nblk + n) * (blk * dil), blk, stride=dil)
            for part in range(V_DIM // LANES):
                scratch[0][h * (V_DIM // LANES) + part, out_rows, :] = o[:, part * LANES:(part + 1) * LANES]
        lse_tile = jnp.where(lane == h, lse, jnp.zeros((blk, LANES), F32) if h == 0 else lse_tile)
        if h == N_HEADS - 1:
            (lse_ref if dil == 1 else scratch[1])[out_rows, :] = lse_tile

    if dil > 1:
        @pl.when((pl.program_id(1) == pl.num_programs(1) - 1) & (c == pl.num_programs(2) - 1))
        def _():
            for slab in range(scratch[0].shape[0]):
                o_ref[:, slab * LANES:(slab + 1) * LANES] = scratch[0][slab].astype(BF16)
            lse_ref[...] = scratch[1][...]


def _dilated_group_attention(qkv, bias_tab, g, batch, seq):
    dil = SW_DILATIONS[g]
    sub_len = seq // dil
    rows = min(sub_len, 1024)
    nc = sub_len // rows
    rps = max(1, 512 // sub_len)
    m = batch * seq
    width = N_HEADS * V_DIM
    kern = functools.partial(_dil_attn_kernel, sub_len=sub_len, rows=rows, dil=dil, rps=rps)
    if dil == 1:
        out_specs = (pl.BlockSpec((rows, width), lambda b, r, c: (b * nc + c, 0)),
                     pl.BlockSpec((rows, LANES), lambda b, r, c: (b * nc + c, 0)))
        scratch, sem = [], ("parallel", "parallel", "parallel")
    else:
        out_specs = (pl.BlockSpec((seq, width), lambda b, r, c: (b, 0)),
                     pl.BlockSpec((seq, LANES), lambda b, r, c: (b, 0)))
        scratch = [pltpu.VMEM((width // LANES, seq, LANES), F32), pltpu.VMEM((seq, LANES), F32)]
        sem = ("parallel", "arbitrary", "arbitrary")
    return pl.pallas_call(
        kern,
        out_shape=(jax.ShapeDtypeStruct((m, width), BF16), jax.ShapeDtypeStruct((m, LANES), F32)),
        grid=(batch, dil // rps, nc),
        in_specs=[
            pl.BlockSpec((4, rows, rps * COL_TILE), lambda b, r, c: (0, b * nc + c, r)),
            pl.BlockSpec((4, sub_len, rps * COL_TILE), lambda b, r, c: (1, b, r)),
            pl.BlockSpec((8, sub_len, rps * COL_TILE), lambda b, r, c: (1, b, r)),
            pl.BlockSpec((1, N_HEADS, 2, SW_STEPS, 2 * SW_STEPS), lambda b, r, c: (g, 0, 0, 0, 0)),
        ],
        out_specs=out_specs,
        scratch_shapes=scratch,
        compiler_params=_params(*sem),
        name=f"dilated_attention_g{g}",
    )(qkv, qkv, qkv, bias_tab)


def _ffn_down_kernel(a_ref, w_ref, r_ref, o_ref, *, tm, chunk):
    for r in range(0, tm, chunk):
        o_ref[r:r + chunk, :] = r_ref[r:r + chunk, :] + jnp.dot(a_ref[r:r + chunk, :], w_ref[...],
                                                                preferred_element_type=F32)


def _ffn_down(a, w, res):
    t = _tiles()
    tm, chunk = t["tm_down"], t["down_chunk"]
    m, k = a.shape
    n = w.shape[1]
    return pl.pallas_call(
        functools.partial(_ffn_down_kernel, tm=tm, chunk=chunk),
        out_shape=jax.ShapeDtypeStruct((m, n), F32),
        grid=(m // tm,),
        in_specs=[
            pl.BlockSpec((tm, k), lambda i: (i, 0)),
            pl.BlockSpec((k, n), lambda i: (0, 0), pipeline_mode=pl.Buffered(1)),
            pl.BlockSpec((tm, n), lambda i: (i, 0)),
        ],
        out_specs=pl.BlockSpec((tm, n), lambda i: (i, 0)),
        compiler_params=_params("parallel"),
        name="ffn_down",
    )(a, w, res)


def _attn_out_kernel(*refs, n_groups, tm, chunk):
    o_refs = refs[:n_groups]
    l_refs = refs[n_groups:2 * n_groups] if n_groups > 1 else ()
    w_ref, r_ref, out_ref, wb_ref = refs[-4:]

    @pl.when(pl.program_id(0) == 0)
    def _():
        wb_ref[...] = w_ref[...].astype(BF16)

    def lhs(r):
        if n_groups == 1:
            return o_refs[0][r:r + chunk, :]
        ls = [l_ref[r:r + chunk, :] for l_ref in l_refs]
        mx = functools.reduce(jnp.maximum, ls)
        es = [jnp.exp(l - mx) for l in ls]
        den = functools.reduce(jnp.add, es)
        alphas = [e / den for e in es]
        heads = []
        for h in range(N_HEADS):
            cols = slice(h * V_DIM, (h + 1) * V_DIM)
            terms = [al[:, h:h + 1] * o_ref[r:r + chunk, cols].astype(F32) for al, o_ref in zip(alphas, o_refs)]
            heads.append(functools.reduce(jnp.add, terms).astype(BF16))
        return jnp.concatenate(heads, axis=1)

    starts = list(range(0, tm, chunk))
    a_next = lhs(starts[0])
    for i, r in enumerate(starts):
        a_cur = a_next
        if i + 1 < len(starts):
            a_next = lhs(starts[i + 1])
        out_ref[r:r + chunk, :] = r_ref[r:r + chunk, :] + jnp.dot(a_cur, wb_ref[...], preferred_element_type=F32)


def _attn_out_proj(outs, lses, w, res):
    t = _tiles()
    tm, chunk = t["tm_attn_out"], t["attn_out_chunk"]
    m, k = outs[0].shape
    n = w.shape[2]
    kern = functools.partial(_attn_out_kernel, n_groups=len(outs), tm=tm, chunk=chunk)
    o_spec = pl.BlockSpec((tm, k), lambda i: (i, 0))
    l_spec = pl.BlockSpec((tm, LANES), lambda i: (i, 0))
    return pl.pallas_call(
        kern,
        out_shape=jax.ShapeDtypeStruct((m, n), F32),
        grid=(m // tm,),
        in_specs=[o_spec] * len(outs) + [l_spec] * len(lses) + [
            pl.BlockSpec((None, k, n), lambda i: (0, 0, 0), pipeline_mode=pl.Buffered(1)),
            pl.BlockSpec((tm, n), lambda i: (i, 0))],
        out_specs=pl.BlockSpec((tm, n), lambda i: (i, 0)),
        scratch_shapes=[pltpu.VMEM((k, n), BF16)],
        compiler_params=_params("arbitrary"),
        name="attn_out_proj",
    )(*outs, *lses, w, res)


def _ffn_up_kernel(x_ref, halo_ref, g_ref, wg0_ref, wgn_ref, wv_ref, cwg_ref, cwv_ref, cbg_ref, cbv_ref, *refs,
                   n_casts, tm, nj, tiles_per_seq, chunk):
    wsrc_refs = refs[:n_casts]
    o_ref = refs[n_casts]
    wdst_refs = refs[n_casts + 1:2 * n_casts + 1]
    xn_ref, ug_ref = refs[2 * n_casts + 1:]
    for wsrc_ref, wdst_ref in zip(wsrc_refs, wdst_refs):
        wdst_ref[...] = wsrc_ref[...].astype(BF16)
    i = pl.program_id(0)
    j = pl.program_id(1)
    halo = CONV_HALO

    @pl.when(j == 0)
    def _():
        first = (i % tiles_per_seq) == 0

        @pl.when(first)
        def _():
            xn_ref[0:halo, :] = jnp.zeros((halo, xn_ref.shape[1]), BF16)

        @pl.when(jnp.logical_not(first))
        def _():
            _rmsnorm_rows(halo_ref, g_ref, xn_ref, 0, halo, halo)

        wb = wg0_ref[...].astype(BF16)
        starts = list(range(0, tm, chunk))
        _rmsnorm_rows(x_ref.at[0:chunk], g_ref, xn_ref, halo, chunk, chunk)
        for k, r0 in enumerate(starts):
            if k + 1 < len(starts):
                nxt = starts[k + 1]
                _rmsnorm_rows(x_ref.at[nxt:nxt + chunk], g_ref, xn_ref, halo + nxt, chunk, chunk)
            lo = 0 if r0 == 0 else halo + r0
            hi = halo + r0 + chunk
            ug_ref[lo:hi, :] = jnp.dot(xn_ref[lo:hi, :], wb, preferred_element_type=F32)

    def conv(u, cw_ref, cb_ref):
        cw = cw_ref[...]
        return (cw[0:1] * u[halo - 2:halo - 2 + tm] + cw[1:2] * u[halo - 1:halo - 1 + tm]
                + cw[2:3] * u[halo:halo + tm]) + cb_ref[...]

    def step(project_next_gate):
        xn = xn_ref[...]
        uv = jnp.dot(xn, wv_ref[...].astype(BF16), preferred_element_type=F32)
        gate = conv(ug_ref[...], cwg_ref, cbg_ref)
        if project_next_gate:
            ug_ref[...] = jnp.dot(xn, wgn_ref[...].astype(BF16), preferred_element_type=F32)
        val = conv(uv, cwv_ref, cbv_ref)
        o_ref[...] = (gate / (1.0 + jnp.exp(-gate)) * val).astype(BF16)

    pl.when(j < nj - 1)(functools.partial(step, True))
    pl.when(j == nj - 1)(functools.partial(step, False))


def _ffn_up(h2d, ln_g, w_up, layer, conv_w, conv_b, seq, casts):
    t = _tiles()
    tm, tn = t["tm"], t["tn_up"]
    m, d = h2d.shape
    dff = w_up.shape[2] // 2
    nj = dff // tn
    halo_blocks = tm // CONV_HALO
    kern = functools.partial(_ffn_up_kernel, n_casts=len(casts), tm=tm, nj=nj, tiles_per_seq=seq // tm,
                             chunk=t["norm_chunk"])
    jobs = [_cast_side_job(w, lyr, (m // tm) * nj, lambda i, j: i * nj + j) for w, lyr in casts]
    return pl.pallas_call(
        kern,
        out_shape=(jax.ShapeDtypeStruct((m, dff), BF16), *[job[2] for job in jobs]),
        grid=(m // tm, nj),
        in_specs=[
            pl.BlockSpec((tm, d), lambda i, j: (i, 0)),
            pl.BlockSpec((CONV_HALO, d), lambda i, j: (jnp.maximum(i * halo_blocks - 1, 0), 0)),
            pl.BlockSpec((1, d), lambda i, j: (0, 0)),
            pl.BlockSpec((None, d, tn), lambda i, j: (layer, 0, 0), pipeline_mode=pl.Buffered(1)),
            pl.BlockSpec((None, d, tn), lambda i, j: (layer, 0, jnp.minimum(j + 1, nj - 1))),
            pl.BlockSpec((None, d, tn), lambda i, j: (layer, 0, j + nj)),
            pl.BlockSpec((3, tn), lambda i, j: (0, j)),
            pl.BlockSpec((3, tn), lambda i, j: (0, j + nj)),
            pl.BlockSpec((1, tn), lambda i, j: (0, j)),
            pl.BlockSpec((1, tn), lambda i, j: (0, j + nj)),
            *[job[0] for job in jobs],
        ],
        out_specs=(pl.BlockSpec((tm, tn), lambda i, j: (i, j)), *[job[1] for job in jobs]),
        scratch_shapes=[pltpu.VMEM((tm + CONV_HALO, d), BF16), pltpu.VMEM((tm + CONV_HALO, tn), F32)],
        compiler_params=_params("parallel", "arbitrary"),
        name="ffn_up",
    )(h2d, h2d, ln_g.reshape(1, d), w_up, w_up, w_up, conv_w, conv_w, conv_b.reshape(1, -1), conv_b.reshape(1, -1),
      *[w for w, _ in casts])


def _conv_ffn(h2d, ln_g, w_up_bf16, conv_w, conv_b, w_down, layer, seq, extra_casts=()):
    act, w_down_bf16, *extra = _ffn_up(h2d, ln_g, w_up_bf16[None], 0, conv_w, conv_b, seq,
                                       [(w_down, layer), *extra_casts])
    return _ffn_down(act, w_down_bf16, h2d), extra


def _diff_col_gain(q_gain, k_gain):
    scale = HEAD_DIM ** -0.5
    nqk = 2 * N_HEADS
    gain = jnp.concatenate([jnp.tile(q_gain * scale, nqk), jnp.tile(k_gain, nqk),
                            jnp.ones((N_HEADS * V_DIM,), F32)])
    return gain


def _dil_col_gain(q_gain, k_gain):
    scale = HEAD_DIM ** -0.5
    parts = []
    for g in range(len(SW_DILATIONS)):
        parts += [jnp.tile(q_gain[g] * scale, N_HEADS), jnp.tile(k_gain[g], N_HEADS),
                  jnp.ones((N_HEADS * V_DIM,), F32)]
    return jnp.concatenate(parts)


def _tile_flags(n_cols, tn, normed_ranges):
    flags = []
    for j in range(n_cols // tn):
        lo = j * tn
        flags.append(int(any(a <= lo < b for a, b in normed_ranges)))
    return jnp.asarray(flags, jnp.int32)


def kernel(x, rel_bias, ln_mix, ln_ffn, a_w_qkv, a_q_norm, a_k_norm, a_lambda, a_subln, a_w_o,
           b_w_qkv, b_q_norm, b_k_norm, b_w_o, f_w_up, f_conv_w, f_conv_b, f_w_down):
    batch, seq, d = x.shape
    m = batch * seq
    t = _tiles()
    h = x.reshape(m, d)

    nqk = 2 * N_HEADS * HEAD_DIM
    flags0 = _tile_flags(a_w_qkv.shape[2], t["tn_qkv"], [(0, 2 * nqk)])
    diff_bias, w_qkv0 = _diff_bias_table(rel_bias, seq, a_w_qkv, 0)
    (qkv0,) = _qkv_proj(h, ln_mix[0], w_qkv0[None], _diff_col_gain(a_q_norm[0], a_k_norm[0]), flags0, (1,))
    lam_init = 0.8 - 0.6 * math.exp(-0.3 * 0)
    attn0, w_up0 = _diff_attention(qkv0, diff_bias, a_lambda[0], a_subln[0], batch, seq, lam_init, f_w_up, 0)
    h = _attn_out_proj([attn0], [], a_w_o, h)
    h, (w_up1, w_qkv1) = _conv_ffn(h, ln_ffn[0], w_up0, f_conv_w[0], f_conv_b[0], f_w_down, 0, seq,
                                   extra_casts=[(f_w_up, 1), (b_w_qkv, 0)])

    gcols = 2 * N_HEADS * HEAD_DIM + N_HEADS * V_DIM
    nq1 = N_HEADS * HEAD_DIM
    flags1 = _tile_flags(b_w_qkv.shape[2], t["tn_qkv"],
                         [(g * gcols, g * gcols + 2 * nq1) for g in range(len(SW_DILATIONS))])
    qkv1 = _qkv_proj(h, ln_mix[1], w_qkv1[None], _dil_col_gain(b_q_norm[0], b_k_norm[0]), flags1, SW_DILATIONS)
    dil_bias = _dil_bias_table(rel_bias)
    outs, lses = [], []
    for g in range(len(SW_DILATIONS)):
        o, lse = _dilated_group_attention(qkv1[g], dil_bias, g, batch, seq)
        outs.append(o)
        lses.append(lse)
    h = _attn_out_proj(outs, lses, b_w_o, h)
    h, _ = _conv_ffn(h, ln_ffn[1], w_up1, f_conv_w[1], f_conv_b[1], f_w_down, 1, seq)
    return h.reshape(batch, seq, d)
```

```python
import functools
import math

import jax
import jax.numpy as jnp
from jax import lax
from jax.experimental import pallas as pl
from jax.experimental.pallas import tpu as pltpu

F32 = jnp.float32
BF16 = jnp.bfloat16

N_HEADS = 8
HEAD_DIM = 128
V_DIM = 2 * HEAD_DIM
SW_DILATIONS = (1, 4, 16)
SW_STEPS = 128
NUM_BUCKETS = 32
MAX_DISTANCE = 2048
EPS = 1e-6
NEG = -1e30

LANES = 128
COL_TILE = 256
ATTN_TILE = 256
CONV_HALO = 16
VMEM_LIMIT = 60 * 1024 * 1024


def _tiles():
    return dict(tm=1024, tn_qkv=1024, qkv_rows_chunk=256, tn_up=512, tm_down=512, down_chunk=256,
                norm_chunk=256, tm_attn_out=512, attn_out_chunk=256,
                dil_rows_per_step=1024, dil_rows_per_residue_step=512)


def _params(*sem):
    return pltpu.CompilerParams(dimension_semantics=sem, vmem_limit_bytes=VMEM_LIMIT)


def _rmsnorm_rows(x_ref, g_ref, out_ref, row0, rows, chunk):
    g = g_ref[...]
    for r in range(0, rows, chunk):
        x = x_ref[r:r + chunk, :]
        rs = lax.rsqrt(jnp.mean(x * x, axis=-1, keepdims=True) + EPS)
        out_ref[row0 + r:row0 + r + chunk, :] = (x * rs * g).astype(BF16)


def _t5_bucket(dist):
    max_exact = NUM_BUCKETS // 2
    n = jnp.maximum(dist, 0)
    nf = jnp.maximum(n, 1).astype(F32)
    large = max_exact + (jnp.log(nf / max_exact) / math.log(MAX_DISTANCE / max_exact)
                         * (NUM_BUCKETS - max_exact)).astype(jnp.int32)
    large = jnp.minimum(large, NUM_BUCKETS - 1)
    return jnp.where(n < max_exact, n, large)


def _bucket_range(dmin, dmax):
    def bucket(n):
        n = max(n, 0)
        if n < NUM_BUCKETS // 2:
            return n
        half = NUM_BUCKETS // 2
        return min(half + int(math.log(n / half) / math.log(MAX_DISTANCE / half) * half), NUM_BUCKETS - 1)
    return max(bucket(dmin) - 1, 0), min(bucket(dmax) + 1, NUM_BUCKETS - 1)


def _bias_lookup(rb_ref, bucket, h, bmin, bmax):
    val = jnp.full(bucket.shape, rb_ref[bmin, h], F32)
    for b in range(bmin + 1, bmax + 1):
        val = jnp.where(bucket == b, rb_ref[b, h], val)
    return val


def _diff_bias_kernel(rb_ref, wsrc_ref, o_ref, wdst_ref):
    wdst_ref[...] = wsrc_ref[...].astype(BF16)
    h = pl.program_id(0)
    t = ATTN_TILE
    delta = (lax.broadcasted_iota(jnp.int32, (t, t), 0) - lax.broadcasted_iota(jnp.int32, (t, t), 1))
    for d in range(o_ref.shape[1]):
        dist = d * t + delta
        val = _bias_lookup(rb_ref, _t5_bucket(dist), h, *_bucket_range(d * t - (t - 1), d * t + (t - 1)))
        o_ref[0, d] = jnp.where(dist >= 0, val, NEG) if d == 0 else val


def _diff_bias_table(rel_bias, seq, w_cast, w_cast_layer):
    nd = seq // ATTN_TILE
    w_in, w_out, w_shape = _cast_side_job(w_cast, w_cast_layer, N_HEADS, lambda h: h)
    return pl.pallas_call(
        _diff_bias_kernel,
        out_shape=(jax.ShapeDtypeStruct((N_HEADS, nd, ATTN_TILE, ATTN_TILE), F32), w_shape),
        grid=(N_HEADS,),
        in_specs=[pl.BlockSpec(memory_space=pltpu.SMEM), w_in],
        out_specs=(pl.BlockSpec((1, nd, ATTN_TILE, ATTN_TILE), lambda h: (h, 0, 0, 0)), w_out),
        compiler_params=_params("parallel"),
        name="diff_bias_table",
    )(rel_bias, w_cast)


def _dil_bias_kernel(rb_ref, o_ref):
    h = pl.program_id(0)
    a = lax.broadcasted_iota(jnp.int32, (SW_STEPS, 2 * SW_STEPS), 0)
    b = lax.broadcasted_iota(jnp.int32, (SW_STEPS, 2 * SW_STEPS), 1)
    dist_sub = a + SW_STEPS - b
    in_window = (dist_sub >= 0) & (dist_sub <= SW_STEPS)
    masked = jnp.full((SW_STEPS, SW_STEPS), NEG, F32)
    for g, dil in enumerate(SW_DILATIONS):
        dist = jnp.clip(dist_sub, 0, SW_STEPS) * dil
        val = _bias_lookup(rb_ref, _t5_bucket(dist), h, *_bucket_range(0, SW_STEPS * dil))
        tile = jnp.where(in_window, val, NEG)
        o_ref[g, 0, 0] = tile
        o_ref[g, 0, 1] = jnp.concatenate([tile[:, SW_STEPS:], masked], axis=1)


def _dil_bias_table(rel_bias):
    ng = len(SW_DILATIONS)
    return pl.pallas_call(
        _dil_bias_kernel,
        out_shape=jax.ShapeDtypeStruct((ng, N_HEADS, 2, SW_STEPS, 2 * SW_STEPS), F32),
        grid=(N_HEADS,),
        in_specs=[pl.BlockSpec(memory_space=pltpu.SMEM)],
        out_specs=pl.BlockSpec((ng, 1, 2, SW_STEPS, 2 * SW_STEPS), lambda h: (0, h, 0, 0, 0)),
        compiler_params=_params("parallel"),
        name="dil_bias_table",
    )(rel_bias)


def _qkv_kernel(flag_ref, x_ref, g_ref, w_ref, cg_ref, *refs, tm, tn, chunk, dils, steps_per_group):
    o_refs = refs[:len(dils)]
    xn_ref, acc_ref = refs[len(dils):]
    j = pl.program_id(1)
    normed = flag_ref[j] != 0
    per_tile = COL_TILE // LANES
    rows_chunk = acc_ref.shape[1]

    def finish(o_ref, dil, acc, r0):
        n_rows = rows_chunk // dil
        out_rows = slice(r0 // dil, r0 // dil + n_rows)
        for c in range(tn // LANES):
            y_all = acc[:, c * LANES:(c + 1) * LANES]
            if dil > 1:
                acc_ref[c] = y_all
            for r in range(dil):
                y = y_all if dil == 1 else acc_ref[c, pl.ds(r, n_rows, stride=dil), :]
                rs = lax.rsqrt(jnp.mean(y * y, axis=-1, keepdims=True) + EPS)
                scale = jnp.where(normed, rs, 1.0)
                lo = r * COL_TILE + (c % per_tile) * LANES
                o_ref[c // per_tile, out_rows, lo:lo + LANES] = (
                    y * scale * cg_ref[:, c * LANES:(c + 1) * LANES]).astype(BF16)

    def project_group(o_ref, dil, first_step):
        wb = w_ref[...].astype(BF16)
        starts = list(range(0, tm, rows_chunk))

        def project(r0):
            if first_step:
                _rmsnorm_rows(x_ref.at[r0:r0 + rows_chunk], g_ref, xn_ref, r0, rows_chunk, rows_chunk)
            return jnp.dot(xn_ref[r0:r0 + rows_chunk, :], wb, preferred_element_type=F32)

        acc_next = project(starts[0])
        for i, r0 in enumerate(starts):
            acc = acc_next
            if i + 1 < len(starts):
                acc_next = project(starts[i + 1])
            finish(o_ref, dil, acc, r0)

    pl.when(j == 0)(functools.partial(project_group, o_refs[0], dils[0], True))
    for g, dil in enumerate(dils):
        in_group = (j // steps_per_group == g) & (j > 0)
        pl.when(in_group)(functools.partial(project_group, o_refs[g], dil, False))


def _qkv_proj(x2d, ln_g, w, col_gain, tile_flags, dils):
    t = _tiles()
    tm, tn = t["tm"], t["tn_qkv"]
    m, d = x2d.shape
    n = w.shape[2]
    ng = len(dils)
    steps_per_group = n // ng // tn
    group_tiles = n // ng // COL_TILE
    kern = functools.partial(_qkv_kernel, tm=tm, tn=tn, chunk=t["norm_chunk"], dils=dils,
                             steps_per_group=steps_per_group)

    def out_spec(g, dil):
        def index(i, j, f):
            return (jnp.clip(j - g * steps_per_group, 0, steps_per_group - 1), i, 0)
        return pl.BlockSpec((tn // COL_TILE, tm // dil, dil * COL_TILE), index)

    return pl.pallas_call(
        kern,
        out_shape=[jax.ShapeDtypeStruct((group_tiles, m // dil, dil * COL_TILE), BF16) for dil in dils],
        grid_spec=pltpu.PrefetchScalarGridSpec(
            num_scalar_prefetch=1,
            grid=(m // tm, n // tn),
            in_specs=[
                pl.BlockSpec((tm, d), lambda i, j, f: (i, 0)),
                pl.BlockSpec((1, d), lambda i, j, f: (0, 0)),
                pl.BlockSpec((None, d, tn), lambda i, j, f: (0, 0, j)),
                pl.BlockSpec((1, tn), lambda i, j, f: (0, j)),
            ],
            out_specs=[out_spec(g, dil) for g, dil in enumerate(dils)],
            scratch_shapes=[pltpu.VMEM((tm, d), BF16),
                            pltpu.VMEM((tn // LANES, t["qkv_rows_chunk"], LANES), F32)],
        ),
        compiler_params=_params("parallel", "arbitrary"),
        name="qkv_proj",
    )(tile_flags, x2d, ln_g.reshape(1, d), w, col_gain.reshape(1, n))


def _diff_attn_kernel(q_ref, k_ref, v_ref, bias_ref, lam_ref, sg_ref, wsrc_ref, o_ref, wdst_ref, *, lam_init, seq):
    wdst_ref[...] = wsrc_ref[...].astype(BF16)
    t = ATTN_TILE
    lam = lam_ref[...]
    lam_full = (jnp.exp(jnp.sum(lam[0:1] * lam[1:2], axis=-1, keepdims=True))
                - jnp.exp(jnp.sum(lam[2:3] * lam[3:4], axis=-1, keepdims=True)) + lam_init)
    gain = sg_ref[...] * (1.0 - lam_init)

    def logits(c, mp):
        width = (c + 1) * t
        lo = mp * HEAD_DIM
        bias = jnp.concatenate([bias_ref[0, c - ki] for ki in range(c + 1)], axis=1)
        return lax.dot_general(q_ref[0, c * t:(c + 1) * t, lo:lo + HEAD_DIM], k_ref[0, 0:width, lo:lo + HEAD_DIM],
                               (((1,), (1,)), ((), ())), preferred_element_type=F32) + bias

    def attend(c, s):
        p = jnp.exp(s - jnp.max(s, axis=-1, keepdims=True))
        den = jnp.sum(p, axis=-1, keepdims=True)
        return jnp.dot(p.astype(BF16), v_ref[0, 0:(c + 1) * t, :], preferred_element_type=F32) / den

    units = [(c, mp) for c in reversed(range(seq // t)) for mp in range(2)]
    s_next = logits(*units[0])
    maps = {}
    for i, (c, mp) in enumerate(units):
        s_cur = s_next
        if i + 1 < len(units):
            s_next = logits(*units[i + 1])
        maps[mp] = attend(c, s_cur)
        if mp == 1:
            o = maps[0] - lam_full * maps[1]
            rs = lax.rsqrt(jnp.mean(o * o, axis=-1, keepdims=True) + EPS)
            o_ref[c * t:(c + 1) * t, :] = (o * rs * gain).astype(BF16)


def _cast_side_job(w, layer, n_steps, step_of):
    _, rows, cols = w.shape
    packed_rows = 16
    slab = packed_rows * pl.cdiv(rows, packed_rows * n_steps)
    n_slabs = rows // slab
    assert n_slabs * slab == rows and n_slabs <= n_steps

    def slab_of(*g):
        return jnp.minimum(step_of(*g), n_slabs - 1)

    in_spec = pl.BlockSpec((None, slab, cols), lambda *g: (layer, slab_of(*g), 0))
    out_spec = pl.BlockSpec((slab, cols), lambda *g: (slab_of(*g), 0))
    return in_spec, out_spec, jax.ShapeDtypeStruct((rows, cols), BF16)


def _diff_attention(qkv, bias_tab, lam, subln_g, batch, seq, lam_init, w_cast, w_cast_layer):
    t = ATTN_TILE
    nq = seq // t
    m = batch * seq
    kern = functools.partial(_diff_attn_kernel, lam_init=lam_init, seq=seq)
    w_in, w_out, w_shape = _cast_side_job(w_cast, w_cast_layer, N_HEADS * batch, lambda h, b: h * batch + b)
    return pl.pallas_call(
        kern,
        out_shape=(jax.ShapeDtypeStruct((m, N_HEADS * V_DIM), BF16), w_shape),
        grid=(N_HEADS, batch),
        in_specs=[
            pl.BlockSpec((1, seq, COL_TILE), lambda h, b: (h, b, 0)),
            pl.BlockSpec((1, seq, COL_TILE), lambda h, b: (N_HEADS + h, b, 0)),
            pl.BlockSpec((1, seq, COL_TILE), lambda h, b: (2 * N_HEADS + h, b, 0)),
            pl.BlockSpec((1, nq, t, t), lambda h, b: (h, 0, 0, 0)),
            pl.BlockSpec((4, HEAD_DIM), lambda h, b: (0, 0)),
            pl.BlockSpec((1, V_DIM), lambda h, b: (0, 0)),
            w_in,
        ],
        out_specs=(pl.BlockSpec((seq, V_DIM), lambda h, b: (b, h)), w_out),
        compiler_params=_params("parallel", "parallel"),
        name="diff_attention",
    )(qkv, qkv, qkv, bias_tab, lam, subln_g.reshape(1, V_DIM), w_cast)


def _dil_attn_kernel(q_ref, k_ref, v_ref, bias_ref, o_ref, lse_ref, *scratch, sub_len, rows, dil, rps):
    blk = SW_STEPS
    c = pl.program_id(2)
    nblk = rows // blk
    single = sub_len == blk
    lane = lax.broadcasted_iota(jnp.int32, (blk, LANES), 1)

    def window(n):
        ng = c * nblk + n
        if single:
            return 0, 1, blk
        return pl.multiple_of(jnp.maximum(ng - 1, 0) * blk, blk), (ng == 0).astype(jnp.int32), 2 * blk

    def logits(rr, n, h):
        start, variant, width = window(n)
        tile, lo = h // 2, rr * COL_TILE + (h % 2) * HEAD_DIM
        qt = q_ref[tile, n * blk:(n + 1) * blk, lo:lo + HEAD_DIM]
        kw = k_ref[tile, pl.ds(start, width), lo:lo + HEAD_DIM]
        bias = bias_ref[0, h, variant][:, :width]
        return lax.dot_general(qt, kw, (((1,), (1,)), ((), ())), preferred_element_type=F32) + bias

    def attend(rr, n, h, s):
        start, _, width = window(n)
        mx = jnp.max(s, axis=-1, keepdims=True)
        p = jnp.exp(s - mx)
        den = jnp.sum(p, axis=-1, keepdims=True)
        vw = v_ref[h, pl.ds(start, width), rr * COL_TILE:(rr + 1) * COL_TILE]
        o = jnp.dot(p.astype(BF16), vw, preferred_element_type=F32) / den
        return o, mx + jnp.log(den)

    units = [(rr, n, h) for rr in range(rps) for n in range(nblk) for h in range(N_HEADS)]
    ahead = 2 if nblk == 1 else 1
    queue = [logits(*u) for u in units[:ahead]]
    lse_tile = None
    for i, (rr, n, h) in enumerate(units):
        s = queue.pop(0)
        if i + ahead < len(units):
            queue.append(logits(*units[i + ahead]))
        o, lse = attend(rr, n, h, s)
        if dil == 1:
            out_rows = slice(n * blk, (n + 1) * blk)
            o_ref[out_rows, h * V_DIM:(h + 1) * V_DIM] = o.astype(BF16)
        else:
            r = pl.program_id(1) * rps + rr
            out_rows = pl.ds(r + (c * nblk + n) * (blk * dil), blk, stride=dil)
            for part in range(V_DIM // LANES):
                scratch[0][h * (V_DIM // LANES) + part, out_rows, :] = o[:, part * LANES:(part + 1) * LANES]
        lse_tile = jnp.where(lane == h, lse, jnp.zeros((blk, LANES), F32) if h == 0 else lse_tile)
        if h == N_HEADS - 1:
            (lse_ref if dil == 1 else scratch[1])[out_rows, :] = lse_tile

    if dil > 1:
        @pl.when((pl.program_id(1) == pl.num_programs(1) - 1) & (c == pl.num_programs(2) - 1))
        def _():
            for slab in range(scratch[0].shape[0]):
                o_ref[:, slab * LANES:(slab + 1) * LANES] = scratch[0][slab].astype(BF16)
            lse_ref[...] = scratch[1][...]


def _dilated_group_attention(qkv, bias_tab, g, batch, seq):
    dil = SW_DILATIONS[g]
    sub_len = seq // dil
    rows = min(sub_len, _tiles()["dil_rows_per_step"])
    nc = sub_len // rows
    rps = max(1, _tiles()["dil_rows_per_residue_step"] // sub_len)
    m = batch * seq
    width = N_HEADS * V_DIM
    kern = functools.partial(_dil_attn_kernel, sub_len=sub_len, rows=rows, dil=dil, rps=rps)
    if dil == 1:
        out_specs = (pl.BlockSpec((rows, width), lambda b, r, c: (b * nc + c, 0)),
                     pl.BlockSpec((rows, LANES), lambda b, r, c: (b * nc + c, 0)))
        scratch, sem = [], ("parallel", "parallel", "parallel")
    else:
        out_specs = (pl.BlockSpec((seq, width), lambda b, r, c: (b, 0)),
                     pl.BlockSpec((seq, LANES), lambda b, r, c: (b, 0)))
        scratch = [pltpu.VMEM((width // LANES, seq, LANES), F32), pltpu.VMEM((seq, LANES), F32)]
        sem = ("parallel", "arbitrary", "arbitrary")
    return pl.pallas_call(
        kern,
        out_shape=(jax.ShapeDtypeStruct((m, width), BF16), jax.ShapeDtypeStruct((m, LANES), F32)),
        grid=(batch, dil // rps, nc),
        in_specs=[
            pl.BlockSpec((4, rows, rps * COL_TILE), lambda b, r, c: (0, b * nc + c, r)),
            pl.BlockSpec((4, sub_len, rps * COL_TILE), lambda b, r, c: (1, b, r)),
            pl.BlockSpec((8, sub_len, rps * COL_TILE), lambda b, r, c: (1, b, r)),
            pl.BlockSpec((1, N_HEADS, 2, SW_STEPS, 2 * SW_STEPS), lambda b, r, c: (g, 0, 0, 0, 0)),
        ],
        out_specs=out_specs,
        scratch_shapes=scratch,
        compiler_params=_params(*sem),
        name=f"dilated_attention_g{g}",
    )(qkv, qkv, qkv, bias_tab)


def _ffn_down_kernel(a_ref, w_ref, r_ref, o_ref, *, tm, chunk):
    for r in range(0, tm, chunk):
        o_ref[r:r + chunk, :] = r_ref[r:r + chunk, :] + jnp.dot(a_ref[r:r + chunk, :], w_ref[...],
                                                                preferred_element_type=F32)


def _ffn_down(a, w, res):
    t = _tiles()
    tm, chunk = t["tm_down"], t["down_chunk"]
    m, k = a.shape
    n = w.shape[1]
    return pl.pallas_call(
        functools.partial(_ffn_down_kernel, tm=tm, chunk=chunk),
        out_shape=jax.ShapeDtypeStruct((m, n), F32),
        grid=(m // tm,),
        in_specs=[
            pl.BlockSpec((tm, k), lambda i: (i, 0)),
            pl.BlockSpec((k, n), lambda i: (0, 0), pipeline_mode=pl.Buffered(1)),
            pl.BlockSpec((tm, n), lambda i: (i, 0)),
        ],
        out_specs=pl.BlockSpec((tm, n), lambda i: (i, 0)),
        compiler_params=_params("parallel"),
        name="ffn_down",
    )(a, w, res)


def _attn_out_kernel(*refs, n_groups, tm, chunk):
    o_refs = refs[:n_groups]
    l_refs = refs[n_groups:2 * n_groups] if n_groups > 1 else ()
    w_ref, r_ref, out_ref, wb_ref = refs[-4:]

    @pl.when(pl.program_id(0) == 0)
    def _():
        wb_ref[...] = w_ref[...].astype(BF16)

    def lhs(r):
        if n_groups == 1:
            return o_refs[0][r:r + chunk, :]
        ls = [l_ref[r:r + chunk, :] for l_ref in l_refs]
        mx = functools.reduce(jnp.maximum, ls)
        es = [jnp.exp(l - mx) for l in ls]
        den = functools.reduce(jnp.add, es)
        alphas = [e / den for e in es]
        heads = []
        for h in range(N_HEADS):
            cols = slice(h * V_DIM, (h + 1) * V_DIM)
            terms = [al[:, h:h + 1] * o_ref[r:r + chunk, cols].astype(F32) for al, o_ref in zip(alphas, o_refs)]
            heads.append(functools.reduce(jnp.add, terms).astype(BF16))
        return jnp.concatenate(heads, axis=1)

    starts = list(range(0, tm, chunk))
    a_next = lhs(starts[0])
    for i, r in enumerate(starts):
        a_cur = a_next
        if i + 1 < len(starts):
            a_next = lhs(starts[i + 1])
        out_ref[r:r + chunk, :] = r_ref[r:r + chunk, :] + jnp.dot(a_cur, wb_ref[...], preferred_element_type=F32)


def _attn_out_proj(outs, lses, w, res):
    t = _tiles()
    tm, chunk = t["tm_attn_out"], t["attn_out_chunk"]
    m, k = outs[0].shape
    n = w.shape[2]
    kern = functools.partial(_attn_out_kernel, n_groups=len(outs), tm=tm, chunk=chunk)
    o_spec = pl.BlockSpec((tm, k), lambda i: (i, 0))
    l_spec = pl.BlockSpec((tm, LANES), lambda i: (i, 0))
    return pl.pallas_call(
        kern,
        out_shape=jax.ShapeDtypeStruct((m, n), F32),
        grid=(m // tm,),
        in_specs=[o_spec] * len(outs) + [l_spec] * len(lses) + [
            pl.BlockSpec((None, k, n), lambda i: (0, 0, 0), pipeline_mode=pl.Buffered(1)),
            pl.BlockSpec((tm, n), lambda i: (i, 0))],
        out_specs=pl.BlockSpec((tm, n), lambda i: (i, 0)),
        scratch_shapes=[pltpu.VMEM((k, n), BF16)],
        compiler_params=_params("arbitrary"),
        name="attn_out_proj",
    )(*outs, *lses, w, res)


def _ffn_up_kernel(x_ref, halo_ref, g_ref, wg0_ref, wgn_ref, wv_ref, cwg_ref, cwv_ref, cbg_ref, cbv_ref, *refs,
                   n_casts, tm, nj, tiles_per_seq, chunk):
    wsrc_refs = refs[:n_casts]
    o_ref = refs[n_casts]
    wdst_refs = refs[n_casts + 1:2 * n_casts + 1]
    xn_ref, ug_ref = refs[2 * n_casts + 1:]
    for wsrc_ref, wdst_ref in zip(wsrc_refs, wdst_refs):
        wdst_ref[...] = wsrc_ref[...].astype(BF16)
    i = pl.program_id(0)
    j = pl.program_id(1)
    halo = CONV_HALO

    @pl.when(j == 0)
    def _():
        first = (i % tiles_per_seq) == 0

        @pl.when(first)
        def _():
            xn_ref[0:halo, :] = jnp.zeros((halo, xn_ref.shape[1]), BF16)

        @pl.when(jnp.logical_not(first))
        def _():
            _rmsnorm_rows(halo_ref, g_ref, xn_ref, 0, halo, halo)

        wb = wg0_ref[...].astype(BF16)
        starts = list(range(0, tm, chunk))
        _rmsnorm_rows(x_ref.at[0:chunk], g_ref, xn_ref, halo, chunk, chunk)
        for k, r0 in enumerate(starts):
            if k + 1 < len(starts):
                nxt = starts[k + 1]
                _rmsnorm_rows(x_ref.at[nxt:nxt + chunk], g_ref, xn_ref, halo + nxt, chunk, chunk)
            lo = 0 if r0 == 0 else halo + r0
            hi = halo + r0 + chunk
            ug_ref[lo:hi, :] = jnp.dot(xn_ref[lo:hi, :], wb, preferred_element_type=F32)

    def conv(u, cw_ref, cb_ref):
        cw = cw_ref[...]
        return (cw[0:1] * u[halo - 2:halo - 2 + tm] + cw[1:2] * u[halo - 1:halo - 1 + tm]
                + cw[2:3] * u[halo:halo + tm]) + cb_ref[...]

    def step(project_next_gate):
        xn = xn_ref[...]
        uv = jnp.dot(xn, wv_ref[...].astype(BF16), preferred_element_type=F32)
        gate = conv(ug_ref[...], cwg_ref, cbg_ref)
        if project_next_gate:
            ug_ref[...] = jnp.dot(xn, wgn_ref[...].astype(BF16), preferred_element_type=F32)
        val = conv(uv, cwv_ref, cbv_ref)
        o_ref[...] = (gate / (1.0 + jnp.exp(-gate)) * val).astype(BF16)

    pl.when(j < nj - 1)(functools.partial(step, True))
    pl.when(j == nj - 1)(functools.partial(step, False))


def _ffn_up(h2d, ln_g, w_up, layer, conv_w, conv_b, seq, casts):
    t = _tiles()
    tm, tn = t["tm"], t["tn_up"]
    m, d = h2d.shape
    dff = w_up.shape[2] // 2
    nj = dff // tn
    halo_blocks = tm // CONV_HALO
    kern = functools.partial(_ffn_up_kernel, n_casts=len(casts), tm=tm, nj=nj, tiles_per_seq=seq // tm,
                             chunk=t["norm_chunk"])
    jobs = [_cast_side_job(w, lyr, (m // tm) * nj, lambda i, j: i * nj + j) for w, lyr in casts]
    return pl.pallas_call(
        kern,
        out_shape=(jax.ShapeDtypeStruct((m, dff), BF16), *[job[2] for job in jobs]),
        grid=(m // tm, nj),
        in_specs=[
            pl.BlockSpec((tm, d), lambda i, j: (i, 0)),
            pl.BlockSpec((CONV_HALO, d), lambda i, j: (jnp.maximum(i * halo_blocks - 1, 0), 0)),
            pl.BlockSpec((1, d), lambda i, j: (0, 0)),
            pl.BlockSpec((None, d, tn), lambda i, j: (layer, 0, 0), pipeline_mode=pl.Buffered(1)),
            pl.BlockSpec((None, d, tn), lambda i, j: (layer, 0, jnp.minimum(j + 1, nj - 1))),
            pl.BlockSpec((None, d, tn), lambda i, j: (layer, 0, j + nj)),
            pl.BlockSpec((3, tn), lambda i, j: (0, j)),
            pl.BlockSpec((3, tn), lambda i, j: (0, j + nj)),
            pl.BlockSpec((1, tn), lambda i, j: (0, j)),
            pl.BlockSpec((1, tn), lambda i, j: (0, j + nj)),
            *[job[0] for job in jobs],
        ],
        out_specs=(pl.BlockSpec((tm, tn), lambda i, j: (i, j)), *[job[1] for job in jobs]),
        scratch_shapes=[pltpu.VMEM((tm + CONV_HALO, d), BF16), pltpu.VMEM((tm + CONV_HALO, tn), F32)],
        compiler_params=_params("parallel", "arbitrary"),
        name="ffn_up",
    )(h2d, h2d, ln_g.reshape(1, d), w_up, w_up, w_up, conv_w, conv_w, conv_b.reshape(1, -1), conv_b.reshape(1, -1),
      *[w for w, _ in casts])


def _conv_ffn(h2d, ln_g, w_up_bf16, conv_w, conv_b, w_down, layer, seq, extra_casts=()):
    act, w_down_bf16, *extra = _ffn_up(h2d, ln_g, w_up_bf16[None], 0, conv_w, conv_b, seq,
                                       [(w_down, layer), *extra_casts])
    return _ffn_down(act, w_down_bf16, h2d), extra


def _diff_col_gain(q_gain, k_gain):
    scale = HEAD_DIM ** -0.5
    nqk = 2 * N_HEADS
    gain = jnp.concatenate([jnp.tile(q_gain * scale, nqk), jnp.tile(k_gain, nqk),
                            jnp.ones((N_HEADS * V_DIM,), F32)])
    return gain


def _dil_col_gain(q_gain, k_gain):
    scale = HEAD_DIM ** -0.5
    parts = []
    for g in range(len(SW_DILATIONS)):
        parts += [jnp.tile(q_gain[g] * scale, N_HEADS), jnp.tile(k_gain[g], N_HEADS),
                  jnp.ones((N_HEADS * V_DIM,), F32)]
    return jnp.concatenate(parts)


def _tile_flags(n_cols, tn, normed_ranges):
    flags = []
    for j in range(n_cols // tn):
        lo = j * tn
        flags.append(int(any(a <= lo < b for a, b in normed_ranges)))
    return jnp.asarray(flags, jnp.int32)


def kernel(x, rel_bias, ln_mix, ln_ffn, a_w_qkv, a_q_norm, a_k_norm, a_lambda, a_subln, a_w_o,
           b_w_qkv, b_q_norm, b_k_norm, b_w_o, f_w_up, f_conv_w, f_conv_b, f_w_down):
    batch, seq, d = x.shape
    m = batch * seq
    t = _tiles()
    h = x.reshape(m, d)

    nqk = 2 * N_HEADS * HEAD_DIM
    flags0 = _tile_flags(a_w_qkv.shape[2], t["tn_qkv"], [(0, 2 * nqk)])
    diff_bias, w_qkv0 = _diff_bias_table(rel_bias, seq, a_w_qkv, 0)
    (qkv0,) = _qkv_proj(h, ln_mix[0], w_qkv0[None], _diff_col_gain(a_q_norm[0], a_k_norm[0]), flags0, (1,))
    lam_init = 0.8 - 0.6 * math.exp(-0.3 * 0)
    attn0, w_up0 = _diff_attention(qkv0, diff_bias, a_lambda[0], a_subln[0], batch, seq, lam_init, f_w_up, 0)
    h = _attn_out_proj([attn0], [], a_w_o, h)
    h, (w_up1, w_qkv1) = _conv_ffn(h, ln_ffn[0], w_up0, f_conv_w[0], f_conv_b[0], f_w_down, 0, seq,
                                   extra_casts=[(f_w_up, 1), (b_w_qkv, 0)])

    gcols = 2 * N_HEADS * HEAD_DIM + N_HEADS * V_DIM
    nq1 = N_HEADS * HEAD_DIM
    flags1 = _tile_flags(b_w_qkv.shape[2], t["tn_qkv"],
                         [(g * gcols, g * gcols + 2 * nq1) for g in range(len(SW_DILATIONS))])
    qkv1 = _qkv_proj(h, ln_mix[1], w_qkv1[None], _dil_col_gain(b_q_norm[0], b_k_norm[0]), flags1, SW_DILATIONS)
    dil_bias = _dil_bias_table(rel_bias)
    outs, lses = [], []
    for g in range(len(SW_DILATIONS)):
        o, lse = _dilated_group_attention(qkv1[g], dil_bias, g, batch, seq)
        outs.append(o)
        lses.append(lse)
    h = _attn_out_proj(outs, lses, b_w_o, h)
    h, _ = _conv_ffn(h, ln_ffn[1], w_up1, f_conv_w[1], f_conv_b[1], f_w_down, 1, seq)
    return h.reshape(batch, seq, d)
```

```python
import functools
import math

import jax
import jax.numpy as jnp
from jax import lax
from jax.experimental import pallas as pl
from jax.experimental.pallas import tpu as pltpu

F32 = jnp.float32
BF16 = jnp.bfloat16

N_HEADS = 8
HEAD_DIM = 128
V_DIM = 2 * HEAD_DIM
SW_DILATIONS = (1, 4, 16)
SW_STEPS = 128
NUM_BUCKETS = 32
MAX_DISTANCE = 2048
EPS = 1e-6
NEG = -1e30

LANES = 128
COL_TILE = 256
ATTN_TILE = 256
CONV_HALO = 16
VMEM_LIMIT = 60 * 1024 * 1024


def _tiles():
    return dict(tm=1024, tn_qkv=1024, qkv_rows_chunk=256, tn_up=512, tm_down=512, down_chunk=256,
                norm_chunk=256, tm_attn_out=512, attn_out_chunk=256,
                dil_rows_per_step=1024, dil_rows_per_residue_step=512)


def _params(*sem):
    return pltpu.CompilerParams(dimension_semantics=sem, vmem_limit_bytes=VMEM_LIMIT)


def _rmsnorm_rows(x_ref, g_ref, out_ref, row0, rows, chunk):
    g = g_ref[...]
    for r in range(0, rows, chunk):
        x = x_ref[r:r + chunk, :]
        rs = lax.rsqrt(jnp.mean(x * x, axis=-1, keepdims=True) + EPS)
        out_ref[row0 + r:row0 + r + chunk, :] = (x * rs * g).astype(BF16)


def _t5_bucket(dist):
    max_exact = NUM_BUCKETS // 2
    n = jnp.maximum(dist, 0)
    nf = jnp.maximum(n, 1).astype(F32)
    large = max_exact + (jnp.log(nf / max_exact) / math.log(MAX_DISTANCE / max_exact)
                         * (NUM_BUCKETS - max_exact)).astype(jnp.int32)
    large = jnp.minimum(large, NUM_BUCKETS - 1)
    return jnp.where(n < max_exact, n, large)


def _bucket_range(dmin, dmax):
    def bucket(n):
        n = max(n, 0)
        if n < NUM_BUCKETS // 2:
            return n
        half = NUM_BUCKETS // 2
        return min(half + int(math.log(n / half) / math.log(MAX_DISTANCE / half) * half), NUM_BUCKETS - 1)
    return max(bucket(dmin) - 1, 0), min(bucket(dmax) + 1, NUM_BUCKETS - 1)


def _bias_lookup(rb_ref, bucket, h, bmin, bmax):
    val = jnp.full(bucket.shape, rb_ref[bmin, h], F32)
    for b in range(bmin + 1, bmax + 1):
        val = jnp.where(bucket == b, rb_ref[b, h], val)
    return val


def _diff_bias_kernel(rb_ref, wsrc_ref, o_ref, wdst_ref):
    wdst_ref[...] = wsrc_ref[...].astype(BF16)
    h = pl.program_id(0)
    t = ATTN_TILE
    delta = (lax.broadcasted_iota(jnp.int32, (t, t), 0) - lax.broadcasted_iota(jnp.int32, (t, t), 1))
    for d in range(o_ref.shape[1]):
        dist = d * t + delta
        val = _bias_lookup(rb_ref, _t5_bucket(dist), h, *_bucket_range(d * t - (t - 1), d * t + (t - 1)))
        o_ref[0, d] = jnp.where(dist >= 0, val, NEG) if d == 0 else val


def _diff_bias_table(rel_bias, seq, w_cast, w_cast_layer):
    nd = seq // ATTN_TILE
    w_in, w_out, w_shape = _cast_side_job(w_cast, w_cast_layer, N_HEADS, lambda h: h)
    return pl.pallas_call(
        _diff_bias_kernel,
        out_shape=(jax.ShapeDtypeStruct((N_HEADS, nd, ATTN_TILE, ATTN_TILE), F32), w_shape),
        grid=(N_HEADS,),
        in_specs=[pl.BlockSpec(memory_space=pltpu.SMEM), w_in],
        out_specs=(pl.BlockSpec((1, nd, ATTN_TILE, ATTN_TILE), lambda h: (h, 0, 0, 0)), w_out),
        compiler_params=_params("parallel"),
        name="diff_bias_table",
    )(rel_bias, w_cast)


def _dil_bias_kernel(rb_ref, o_ref):
    h = pl.program_id(0)
    a = lax.broadcasted_iota(jnp.int32, (SW_STEPS, 2 * SW_STEPS), 0)
    b = lax.broadcasted_iota(jnp.int32, (SW_STEPS, 2 * SW_STEPS), 1)
    dist_sub = a + SW_STEPS - b
    in_window = (dist_sub >= 0) & (dist_sub <= SW_STEPS)
    masked = jnp.full((SW_STEPS, SW_STEPS), NEG, F32)
    for g, dil in enumerate(SW_DILATIONS):
        dist = jnp.clip(dist_sub, 0, SW_STEPS) * dil
        val = _bias_lookup(rb_ref, _t5_bucket(dist), h, *_bucket_range(0, SW_STEPS * dil))
        tile = jnp.where(in_window, val, NEG)
        o_ref[g, 0, 0] = tile
        o_ref[g, 0, 1] = jnp.concatenate([tile[:, SW_STEPS:], masked], axis=1)


def _dil_bias_table(rel_bias):
    ng = len(SW_DILATIONS)
    return pl.pallas_call(
        _dil_bias_kernel,
        out_shape=jax.ShapeDtypeStruct((ng, N_HEADS, 2, SW_STEPS, 2 * SW_STEPS), F32),
        grid=(N_HEADS,),
        in_specs=[pl.BlockSpec(memory_space=pltpu.SMEM)],
        out_specs=pl.BlockSpec((ng, 1, 2, SW_STEPS, 2 * SW_STEPS), lambda h: (0, h, 0, 0, 0)),
        compiler_params=_params("parallel"),
        name="dil_bias_table",
    )(rel_bias)


def _qkv_kernel(flag_ref, x_ref, g_ref, w_ref, cg_ref, *refs, tm, tn, chunk, dils, steps_per_group):
    o_refs = refs[:len(dils)]
    xn_ref, acc_ref = refs[len(dils):]
    j = pl.program_id(1)
    normed = flag_ref[j] != 0
    per_tile = COL_TILE // LANES
    rows_chunk = acc_ref.shape[1]

    def finish(o_ref, dil, acc, r0):
        n_rows = rows_chunk // dil
        out_rows = slice(r0 // dil, r0 // dil + n_rows)
        for c in range(tn // LANES):
            y_all = acc[:, c * LANES:(c + 1) * LANES]
            if dil > 1:
                acc_ref[c] = y_all
            for r in range(dil):
                y = y_all if dil == 1 else acc_ref[c, pl.ds(r, n_rows, stride=dil), :]
                rs = lax.rsqrt(jnp.mean(y * y, axis=-1, keepdims=True) + EPS)
                scale = jnp.where(normed, rs, 1.0)
                lo = r * COL_TILE + (c % per_tile) * LANES
                o_ref[c // per_tile, out_rows, lo:lo + LANES] = (
                    y * scale * cg_ref[:, c * LANES:(c + 1) * LANES]).astype(BF16)

    def project_group(o_ref, dil, first_step):
        wb = w_ref[...].astype(BF16)
        starts = list(range(0, tm, rows_chunk))

        def project(r0):
            if first_step:
                _rmsnorm_rows(x_ref.at[r0:r0 + rows_chunk], g_ref, xn_ref, r0, rows_chunk, rows_chunk)
            return jnp.dot(xn_ref[r0:r0 + rows_chunk, :], wb, preferred_element_type=F32)

        acc_next = project(starts[0])
        for i, r0 in enumerate(starts):
            acc = acc_next
            if i + 1 < len(starts):
                acc_next = project(starts[i + 1])
            finish(o_ref, dil, acc, r0)

    pl.when(j == 0)(functools.partial(project_group, o_refs[0], dils[0], True))
    for g, dil in enumerate(dils):
        in_group = (j // steps_per_group == g) & (j > 0)
        pl.when(in_group)(functools.partial(project_group, o_refs[g], dil, False))


def _qkv_proj(x2d, ln_g, w, col_gain, tile_flags, dils):
    t = _tiles()
    tm, tn = t["tm"], t["tn_qkv"]
    m, d = x2d.shape
    n = w.shape[2]
    ng = len(dils)
    steps_per_group = n // ng // tn
    group_tiles = n // ng // COL_TILE
    kern = functools.partial(_qkv_kernel, tm=tm, tn=tn, chunk=t["norm_chunk"], dils=dils,
                             steps_per_group=steps_per_group)

    def out_spec(g, dil):
        def index(i, j, f):
            return (jnp.clip(j - g * steps_per_group, 0, steps_per_group - 1), i, 0)
        return pl.BlockSpec((tn // COL_TILE, tm // dil, dil * COL_TILE), index)

    return pl.pallas_call(
        kern,
        out_shape=[jax.ShapeDtypeStruct((group_tiles, m // dil, dil * COL_TILE), BF16) for dil in dils],
        grid_spec=pltpu.PrefetchScalarGridSpec(
            num_scalar_prefetch=1,
            grid=(m // tm, n // tn),
            in_specs=[
                pl.BlockSpec((tm, d), lambda i, j, f: (i, 0)),
                pl.BlockSpec((1, d), lambda i, j, f: (0, 0)),
                pl.BlockSpec((None, d, tn), lambda i, j, f: (0, 0, j)),
                pl.BlockSpec((1, tn), lambda i, j, f: (0, j)),
            ],
            out_specs=[out_spec(g, dil) for g, dil in enumerate(dils)],
            scratch_shapes=[pltpu.VMEM((tm, d), BF16),
                            pltpu.VMEM((tn // LANES, t["qkv_rows_chunk"], LANES), F32)],
        ),
        compiler_params=_params("parallel", "arbitrary"),
        name="qkv_proj",
    )(tile_flags, x2d, ln_g.reshape(1, d), w, col_gain.reshape(1, n))


def _diff_attn_kernel(q_ref, k_ref, v_ref, bias_ref, lam_ref, sg_ref, wsrc_ref, o_ref, wdst_ref, *, lam_init, seq):
    wdst_ref[...] = wsrc_ref[...].astype(BF16)
    t = ATTN_TILE
    lam = lam_ref[...]
    lam_full = (jnp.exp(jnp.sum(lam[0:1] * lam[1:2], axis=-1, keepdims=True))
                - jnp.exp(jnp.sum(lam[2:3] * lam[3:4], axis=-1, keepdims=True)) + lam_init)
    gain = sg_ref[...] * (1.0 - lam_init)

    def logits(c, mp):
        width = (c + 1) * t
        lo = mp * HEAD_DIM
        bias = jnp.concatenate([bias_ref[0, c - ki] for ki in range(c + 1)], axis=1)
        return lax.dot_general(q_ref[0, c * t:(c + 1) * t, lo:lo + HEAD_DIM], k_ref[0, 0:width, lo:lo + HEAD_DIM],
                               (((1,), (1,)), ((), ())), preferred_element_type=F32) + bias

    def attend(c, s):
        p = jnp.exp(s - jnp.max(s, axis=-1, keepdims=True))
        den = jnp.sum(p, axis=-1, keepdims=True)
        return jnp.dot(p.astype(BF16), v_ref[0, 0:(c + 1) * t, :], preferred_element_type=F32) / den

    units = [(c, mp) for c in reversed(range(seq // t)) for mp in range(2)]
    s_next = logits(*units[0])
    maps = {}
    for i, (c, mp) in enumerate(units):
        s_cur = s_next
        if i + 1 < len(units):
            s_next = logits(*units[i + 1])
        maps[mp] = attend(c, s_cur)
        if mp == 1:
            o = maps[0] - lam_full * maps[1]
            rs = lax.rsqrt(jnp.mean(o * o, axis=-1, keepdims=True) + EPS)
            o_ref[c * t:(c + 1) * t, :] = (o * rs * gain).astype(BF16)


def _cast_side_job(w, layer, n_steps, step_of):
    _, rows, cols = w.shape
    packed_rows = 16
    slab = packed_rows * pl.cdiv(rows, packed_rows * n_steps)
    n_slabs = rows // slab
    assert n_slabs * slab == rows and n_slabs <= n_steps

    def slab_of(*g):
        return jnp.minimum(step_of(*g), n_slabs - 1)

    in_spec = pl.BlockSpec((None, slab, cols), lambda *g: (layer, slab_of(*g), 0))
    out_spec = pl.BlockSpec((slab, cols), lambda *g: (slab_of(*g), 0))
    return in_spec, out_spec, jax.ShapeDtypeStruct((rows, cols), BF16)


def _diff_attention(qkv, bias_tab, lam, subln_g, batch, seq, lam_init, w_cast, w_cast_layer):
    t = ATTN_TILE
    nq = seq // t
    m = batch * seq
    kern = functools.partial(_diff_attn_kernel, lam_init=lam_init, seq=seq)
    w_in, w_out, w_shape = _cast_side_job(w_cast, w_cast_layer, N_HEADS * batch, lambda h, b: h * batch + b)
    return pl.pallas_call(
        kern,
        out_shape=(jax.ShapeDtypeStruct((m, N_HEADS * V_DIM), BF16), w_shape),
        grid=(N_HEADS, batch),
        in_specs=[
            pl.BlockSpec((1, seq, COL_TILE), lambda h, b: (h, b, 0)),
            pl.BlockSpec((1, seq, COL_TILE), lambda h, b: (N_HEADS + h, b, 0)),
            pl.BlockSpec((1, seq, COL_TILE), lambda h, b: (2 * N_HEADS + h, b, 0)),
            pl.BlockSpec((1, nq, t, t), lambda h, b: (h, 0, 0, 0)),
            pl.BlockSpec((4, HEAD_DIM), lambda h, b: (0, 0)),
            pl.BlockSpec((1, V_DIM), lambda h, b: (0, 0)),
            w_in,
        ],
        out_specs=(pl.BlockSpec((seq, V_DIM), lambda h, b: (b, h)), w_out),
        compiler_params=_params("parallel", "parallel"),
        name="diff_attention",
    )(qkv, qkv, qkv, bias_tab, lam, subln_g.reshape(1, V_DIM), w_cast)


def _dil_attn_kernel(q_ref, k_ref, v_ref, bias_ref, o_ref, lse_ref, *scratch, sub_len, rows, dil, rps):
    blk = SW_STEPS
    c = pl.program_id(2)
    nblk = rows // blk
    single = sub_len == blk
    lane = lax.broadcasted_iota(jnp.int32, (blk, LANES), 1)

    def window(n):
        ng = c * nblk + n
        if single:
            return 0, 1, blk
        return pl.multiple_of(jnp.maximum(ng - 1, 0) * blk, blk), (ng == 0).astype(jnp.int32), 2 * blk

    def logits(rr, n, h):
        start, variant, width = window(n)
        tile, lo = h // 2, rr * COL_TILE + (h % 2) * HEAD_DIM
        qt = q_ref[tile, n * blk:(n + 1) * blk, lo:lo + HEAD_DIM]
        kw = k_ref[tile, pl.ds(start, width), lo:lo + HEAD_DIM]
        bias = bias_ref[0, h, variant][:, :width]
        return lax.dot_general(qt, kw, (((1,), (1,)), ((), ())), preferred_element_type=F32) + bias

    def attend(rr, n, h, s):
        start, _, width = window(n)
        mx = jnp.max(s, axis=-1, keepdims=True)
        p = jnp.exp(s - mx)
        den = jnp.sum(p, axis=-1, keepdims=True)
        vw = v_ref[h, pl.ds(start, width), rr * COL_TILE:(rr + 1) * COL_TILE]
        o = jnp.dot(p.astype(BF16), vw, preferred_element_type=F32) / den
        return o, mx + jnp.log(den)

    units = [(rr, n, h) for rr in range(rps) for n in range(nblk) for h in range(N_HEADS)]
    ahead = 2 if nblk == 1 else 1
    queue = [logits(*u) for u in units[:ahead]]
    lse_tile = None
    for i, (rr, n, h) in enumerate(units):
        s = queue.pop(0)
        if i + ahead < len(units):
            queue.append(logits(*units[i + ahead]))
        o, lse = attend(rr, n, h, s)
        if dil == 1:
            out_rows = slice(n * blk, (n + 1) * blk)
            o_ref[out_rows, h * V_DIM:(h + 1) * V_DIM] = o.astype(BF16)
        else:
            r = pl.program_id(1) * rps + rr
            out_rows = pl.ds(r + (c * nblk + n) * (blk * dil), blk, stride=dil)
            for part in range(V_DIM // LANES):
                scratch[0][h * (V_DIM // LANES) + part, out_rows, :] = o[:, part * LANES:(part + 1) * LANES]
        lse_tile = jnp.where(lane == h, lse, jnp.zeros((blk, LANES), F32) if h == 0 else lse_tile)
        if h == N_HEADS - 1:
            (lse_ref if dil == 1 else scratch[1])[out_rows, :] = lse_tile

    if dil > 1:
        @pl.when((pl.program_id(1) == pl.num_programs(1) - 1) & (c == pl.num_programs(2) - 1))
        def _():
            for slab in range(scratch[0].shape[0]):
                o_ref[:, slab * LANES:(slab + 1) * LANES] = scratch[0][slab].astype(BF16)
            lse_ref[...] = scratch[1][...]


def _dilated_group_attention(qkv, bias_tab, g, batch, seq):
    dil = SW_DILATIONS[g]
    sub_len = seq // dil
    rows = min(sub_len, _tiles()["dil_rows_per_step"])
    nc = sub_len // rows
    rps = max(1, _tiles()["dil_rows_per_residue_step"] // sub_len)
    m = batch * seq
    width = N_HEADS * V_DIM
    kern = functools.partial(_dil_attn_kernel, sub_len=sub_len, rows=rows, dil=dil, rps=rps)
    if dil == 1:
        out_specs = (pl.BlockSpec((rows, width), lambda b, r, c: (b * nc + c, 0)),
                     pl.BlockSpec((rows, LANES), lambda b, r, c: (b * nc + c, 0)))
        scratch, sem = [], ("parallel", "parallel", "parallel")
    else:
        out_specs = (pl.BlockSpec((seq, width), lambda b, r, c: (b, 0)),
                     pl.BlockSpec((seq, LANES), lambda b, r, c: (b, 0)))
        scratch = [pltpu.VMEM((width // LANES, seq, LANES), F32), pltpu.VMEM((seq, LANES), F32)]
        sem = ("parallel", "arbitrary", "arbitrary")
    return pl.pallas_call(
        kern,
        out_shape=(jax.ShapeDtypeStruct((m, width), BF16), jax.ShapeDtypeStruct((m, LANES), F32)),
        grid=(batch, dil // rps, nc),
        in_specs=[
            pl.BlockSpec((4, rows, rps * COL_TILE), lambda b, r, c: (0, b * nc + c, r)),
            pl.BlockSpec((4, sub_len, rps * COL_TILE), lambda b, r, c: (1, b, r)),
            pl.BlockSpec((8, sub_len, rps * COL_TILE), lambda b, r, c: (1, b, r)),
            pl.BlockSpec((1, N_HEADS, 2, SW_STEPS, 2 * SW_STEPS), lambda b, r, c: (g, 0, 0, 0, 0)),
        ],
        out_specs=out_specs,
        scratch_shapes=scratch,
        compiler_params=_params(*sem),
        name=f"dilated_attention_g{g}",
    )(qkv, qkv, qkv, bias_tab)


def _ffn_down_kernel(a_ref, w_ref, r_ref, o_ref, *, tm, chunk):
    for r in range(0, tm, chunk):
        o_ref[r:r + chunk, :] = r_ref[r:r + chunk, :] + jnp.dot(a_ref[r:r + chunk, :], w_ref[...],
                                                                preferred_element_type=F32)


def _ffn_down(a, w, res):
    t = _tiles()
    tm, chunk = t["tm_down"], t["down_chunk"]
    m, k = a.shape
    n = w.shape[1]
    return pl.pallas_call(
        functools.partial(_ffn_down_kernel, tm=tm, chunk=chunk),
        out_shape=jax.ShapeDtypeStruct((m, n), F32),
        grid=(m // tm,),
        in_specs=[
            pl.BlockSpec((tm, k), lambda i: (i, 0)),
            pl.BlockSpec((k, n), lambda i: (0, 0), pipeline_mode=pl.Buffered(1)),
            pl.BlockSpec((tm, n), lambda i: (i, 0)),
        ],
        out_specs=pl.BlockSpec((tm, n), lambda i: (i, 0)),
        compiler_params=_params("parallel"),
        name="ffn_down",
    )(a, w, res)


def _attn_out_kernel(*refs, n_groups, tm, chunk):
    o_refs = refs[:n_groups]
    l_refs = refs[n_groups:2 * n_groups] if n_groups > 1 else ()
    w_ref, r_ref, out_ref, wb_ref = refs[-4:]

    @pl.when(pl.program_id(0) == 0)
    def _():
        wb_ref[...] = w_ref[...].astype(BF16)

    def lhs(r):
        if n_groups == 1:
            return o_refs[0][r:r + chunk, :]
        ls = [l_ref[r:r + chunk, :] for l_ref in l_refs]
        mx = functools.reduce(jnp.maximum, ls)
        es = [jnp.exp(l - mx) for l in ls]
        den = functools.reduce(jnp.add, es)
        alphas = [e / den for e in es]
        heads = []
        for h in range(N_HEADS):
            cols = slice(h * V_DIM, (h + 1) * V_DIM)
            terms = [al[:, h:h + 1] * o_ref[r:r + chunk, cols].astype(F32) for al, o_ref in zip(alphas, o_refs)]
            heads.append(functools.reduce(jnp.add, terms).astype(BF16))
        return jnp.concatenate(heads, axis=1)

    starts = list(range(0, tm, chunk))
    a_next = lhs(starts[0])
    for i, r in enumerate(starts):
        a_cur = a_next
        if i + 1 < len(starts):
            a_next = lhs(starts[i + 1])
        out_ref[r:r + chunk, :] = r_ref[r:r + chunk, :] + jnp.dot(a_cur, wb_ref[...], preferred_element_type=F32)


def _attn_out_proj(outs, lses, w, res):
    t = _tiles()
    tm, chunk = t["tm_attn_out"], t["attn_out_chunk"]
    m, k = outs[0].shape
    n = w.shape[2]
    kern = functools.partial(_attn_out_kernel, n_groups=len(outs), tm=tm, chunk=chunk)
    o_spec = pl.BlockSpec((tm, k), lambda i: (i, 0))
    l_spec = pl.BlockSpec((tm, LANES), lambda i: (i, 0))
    return pl.pallas_call(
        kern,
        out_shape=jax.ShapeDtypeStruct((m, n), F32),
        grid=(m // tm,),
        in_specs=[o_spec] * len(outs) + [l_spec] * len(lses) + [
            pl.BlockSpec((None, k, n), lambda i: (0, 0, 0), pipeline_mode=pl.Buffered(1)),
            pl.BlockSpec((tm, n), lambda i: (i, 0))],
        out_specs=pl.BlockSpec((tm, n), lambda i: (i, 0)),
        scratch_shapes=[pltpu.VMEM((k, n), BF16)],
        compiler_params=_params("arbitrary"),
        name="attn_out_proj",
    )(*outs, *lses, w, res)


def _ffn_up_kernel(x_ref, halo_ref, g_ref, wg0_ref, wgn_ref, wv_ref, cwg_ref, cwv_ref, cbg_ref, cbv_ref, *refs,
                   n_casts, tm, nj, tiles_per_seq, chunk):
    wsrc_refs = refs[:n_casts]
    o_ref = refs[n_casts]
    wdst_refs = refs[n_casts + 1:2 * n_casts + 1]
    xn_ref, ug_ref = refs[2 * n_casts + 1:]
    for wsrc_ref, wdst_ref in zip(wsrc_refs, wdst_refs):
        wdst_ref[...] = wsrc_ref[...].astype(BF16)
    i = pl.program_id(0)
    j = pl.program_id(1)
    halo = CONV_HALO

    @pl.when(j == 0)
    def _():
        first = (i % tiles_per_seq) == 0

        @pl.when(first)
        def _():
            xn_ref[0:halo, :] = jnp.zeros((halo, xn_ref.shape[1]), BF16)

        @pl.when(jnp.logical_not(first))
        def _():
            _rmsnorm_rows(halo_ref, g_ref, xn_ref, 0, halo, halo)

        wb = wg0_ref[...].astype(BF16)
        starts = list(range(0, tm, chunk))
        _rmsnorm_rows(x_ref.at[0:chunk], g_ref, xn_ref, halo, chunk, chunk)
        for k, r0 in enumerate(starts):
            if k + 1 < len(starts):
                nxt = starts[k + 1]
                _rmsnorm_rows(x_ref.at[nxt:nxt + chunk], g_ref, xn_ref, halo + nxt, chunk, chunk)
            lo = 0 if r0 == 0 else halo + r0
            hi = halo + r0 + chunk
            ug_ref[lo:hi, :] = jnp.dot(xn_ref[lo:hi, :], wb, preferred_element_type=F32)

    def conv(u, cw_ref, cb_ref):
        cw = cw_ref[...]
        return (cw[0:1] * u[halo - 2:halo - 2 + tm] + cw[1:2] * u[halo - 1:halo - 1 + tm]
                + cw[2:3] * u[halo:halo + tm]) + cb_ref[...]

    def step(project_next_gate):
        xn = xn_ref[...]
        uv = jnp.dot(xn, wv_ref[...].astype(BF16), preferred_element_type=F32)
        gate = conv(ug_ref[...], cwg_ref, cbg_ref)
        sgate = gate / (1.0 + jnp.exp(-gate))
        if project_next_gate:
            tn = sgate.shape[1]
            folded = jnp.sum(sgate, axis=0, keepdims=True)
            bits = lax.bitcast_convert_type(folded, jnp.uint32)
            bits = functools.reduce(jnp.bitwise_or, [bits[:, c * LANES:(c + 1) * LANES] for c in range(tn // LANES)])
            zero =lax.shift_right_logical(lax.shift_right_logical(bits, jnp.uint32(16)), jnp.uint32(16))
            zero_row = jnp.tile(lax.bitcast_convert_type(zero, F32).astype(BF16), (1, xn.shape[1] // LANES))
            ug_ref[...] = jnp.dot(xn + zero_row, wgn_ref[...].astype(BF16), preferred_element_type=F32)
        val = conv(uv, cwv_ref, cbv_ref)
        o_ref[...] = (sgate * val).astype(BF16)

    pl.when(j < nj - 1)(functools.partial(step, True))
    pl.when(j == nj - 1)(functools.partial(step, False))


def _ffn_up(h2d, ln_g, w_up, layer, conv_w, conv_b, seq, casts):
    t = _tiles()
    tm, tn = t["tm"], t["tn_up"]
    m, d = h2d.shape
    dff = w_up.shape[2] // 2
    nj = dff // tn
    halo_blocks = tm // CONV_HALO
    kern = functools.partial(_ffn_up_kernel, n_casts=len(casts), tm=tm, nj=nj, tiles_per_seq=seq // tm,
                             chunk=t["norm_chunk"])
    jobs = [_cast_side_job(w, lyr, (m // tm) * nj, lambda i, j: i * nj + j) for w, lyr in casts]
    return pl.pallas_call(
        kern,
        out_shape=(jax.ShapeDtypeStruct((m, dff), BF16), *[job[2] for job in jobs]),
        grid=(m // tm, nj),
        in_specs=[
            pl.BlockSpec((tm, d), lambda i, j: (i, 0)),
            pl.BlockSpec((CONV_HALO, d), lambda i, j: (jnp.maximum(i * halo_blocks - 1, 0), 0)),
            pl.BlockSpec((1, d), lambda i, j: (0, 0)),
            pl.BlockSpec((None, d, tn), lambda i, j: (layer, 0, 0), pipeline_mode=pl.Buffered(1)),
            pl.BlockSpec((None, d, tn), lambda i, j: (layer, 0, jnp.minimum(j + 1, nj - 1))),
            pl.BlockSpec((None, d, tn), lambda i, j: (layer, 0, j + nj)),
            pl.BlockSpec((3, tn), lambda i, j: (0, j)),
            pl.BlockSpec((3, tn), lambda i, j: (0, j + nj)),
            pl.BlockSpec((1, tn), lambda i, j: (0, j)),
            pl.BlockSpec((1, tn), lambda i, j: (0, j + nj)),
            *[job[0] for job in jobs],
        ],
        out_specs=(pl.BlockSpec((tm, tn), lambda i, j: (i, j)), *[job[1] for job in jobs]),
        scratch_shapes=[pltpu.VMEM((tm + CONV_HALO, d), BF16), pltpu.VMEM((tm + CONV_HALO, tn), F32)],
        compiler_params=_params("parallel", "arbitrary"),
        name="ffn_up",
    )(h2d, h2d, ln_g.reshape(1, d), w_up, w_up, w_up, conv_w, conv_w, conv_b.reshape(1, -1), conv_b.reshape(1, -1),
      *[w for w, _ in casts])


def _conv_ffn(h2d, ln_g, w_up_bf16, conv_w, conv_b, w_down, layer, seq, extra_casts=()):
    act, w_down_bf16, *extra = _ffn_up(h2d, ln_g, w_up_bf16[None], 0, conv_w, conv_b, seq,
                                       [(w_down, layer), *extra_casts])
    return _ffn_down(act, w_down_bf16, h2d), extra


def _diff_col_gain(q_gain, k_gain):
    scale = HEAD_DIM ** -0.5
    nqk = 2 * N_HEADS
    gain = jnp.concatenate([jnp.tile(q_gain * scale, nqk), jnp.tile(k_gain, nqk),
                            jnp.ones((N_HEADS * V_DIM,), F32)])
    return gain


def _dil_col_gain(q_gain, k_gain):
    scale = HEAD_DIM ** -0.5
    parts = []
    for g in range(len(SW_DILATIONS)):
        parts += [jnp.tile(q_gain[g] * scale, N_HEADS), jnp.tile(k_gain[g], N_HEADS),
                  jnp.ones((N_HEADS * V_DIM,), F32)]
    return jnp.concatenate(parts)


def _tile_flags(n_cols, tn, normed_ranges):
    flags = []
    for j in range(n_cols // tn):
        lo = j * tn
        flags.append(int(any(a <= lo < b for a, b in normed_ranges)))
    return jnp.asarray(flags, jnp.int32)


def kernel(x, rel_bias, ln_mix, ln_ffn, a_w_qkv, a_q_norm, a_k_norm, a_lambda, a_subln, a_w_o,
           b_w_qkv, b_q_norm, b_k_norm, b_w_o, f_w_up, f_conv_w, f_conv_b, f_w_down):
    batch, seq, d = x.shape
    m = batch * seq
    t = _tiles()
    h = x.reshape(m, d)

    nqk = 2 * N_HEADS * HEAD_DIM
    flags0 = _tile_flags(a_w_qkv.shape[2], t["tn_qkv"], [(0, 2 * nqk)])
    diff_bias, w_qkv0 = _diff_bias_table(rel_bias, seq, a_w_qkv, 0)
    (qkv0,) = _qkv_proj(h, ln_mix[0], w_qkv0[None], _diff_col_gain(a_q_norm[0], a_k_norm[0]), flags0, (1,))
    lam_init = 0.8 - 0.6 * math.exp(-0.3 * 0)
    attn0, w_up0 = _diff_attention(qkv0, diff_bias, a_lambda[0], a_subln[0], batch, seq, lam_init, f_w_up, 0)
    h = _attn_out_proj([attn0], [], a_w_o, h)
    h, (w_up1, w_qkv1) = _conv_ffn(h, ln_ffn[0], w_up0, f_conv_w[0], f_conv_b[0], f_w_down, 0, seq,
                                   extra_casts=[(f_w_up, 1), (b_w_qkv, 0)])

    gcols = 2 * N_HEADS * HEAD_DIM + N_HEADS * V_DIM
    nq1 = N_HEADS * HEAD_DIM
    flags1 = _tile_flags(b_w_qkv.shape[2], t["tn_qkv"],
                         [(g * gcols, g * gcols + 2 * nq1) for g in range(len(SW_DILATIONS))])
    qkv1 = _qkv_proj(h, ln_mix[1], w_qkv1[None], _dil_col_gain(b_q_norm[0], b_k_norm[0]), flags1, SW_DILATIONS)
    dil_bias = _dil_bias_table(rel_bias)
    outs, lses = [], []
    for g in range(len(SW_DILATIONS)):
        o, lse = _dilated_group_attention(qkv1[g], dil_bias, g, batch, seq)
        outs.append(o)
        lses.append(lse)
    h = _attn_out_proj(outs, lses, b_w_o, h)
    h, _ = _conv_ffn(h, ln_ffn[1], w_up1, f_conv_w[1], f_conv_b[1], f_w_down, 1, seq)
    return h.reshape(batch, seq, d)
```
